```python
import jax, jax.numpy as jnp
from jax import lax
import numpy as np

D_MODEL = 1024
BATCH = 4
SEQ = 8192
DEPTH = 2

D_MIX = D_MODEL
D_FOURIER = D_MIX // 2
D_SSM = D_MIX - D_FOURIER
FOURIER_HEADS = 8
FOURIER_HEAD_DIM = D_FOURIER // FOURIER_HEADS
SSM_GROUP = 16
SSM_GROUPS = D_SSM // SSM_GROUP
SSM_STATE = 64
N_DIR = 2
D_FF = 4 * D_MODEL
DT_MIN = 1e-3
DT_MAX = 1e-1
EPS = 1e-6

kernel_name = "hybrid_fnet_s5_sandwich_encoder"


def rms_norm(x, g):
    x32 = x.astype(jnp.float32)
    y = x32 * lax.rsqrt(jnp.mean(x32 * x32, axis=-1, keepdims=True) + EPS)
    return (y * g.astype(jnp.float32)).astype(x.dtype)


def fourier_mixer(u, w_f):
    b, l, _ = u.shape
    uh = u.astype(jnp.float32).reshape(b, l, FOURIER_HEADS, FOURIER_HEAD_DIM)
    f = jnp.fft.fft(jnp.fft.fft(uh, axis=3, norm="ortho"), axis=1, norm="ortho").real
    y = jnp.einsum("blhc,hcd->blhd", f, w_f.astype(jnp.float32))
    return y.reshape(b, l, D_FOURIER).astype(u.dtype)


def _combine(left, right):
    a_l, b_l = left
    a_r, b_r = right
    return a_r * a_l, a_r * b_l + b_r


def s5_direction(u, lam_re, lam_im, log_dt, b_re, b_im, c_re, c_im, reverse):
    f32 = jnp.float32
    lam = lax.complex(lam_re.astype(f32), lam_im.astype(f32))
    dt = jnp.exp(log_dt.astype(f32))[:, None]
    lam_bar = jnp.exp(lam * dt)
    b_bar = ((lam_bar - 1.0) / lam)[..., None] * lax.complex(b_re.astype(f32), b_im.astype(f32))
    bu = jnp.einsum("blgh,gph->blgp", u, b_bar)
    a = jnp.broadcast_to(lam_bar, bu.shape)
    _, s = lax.associative_scan(_combine, (a, bu), axis=1, reverse=reverse)
    c = lax.complex(c_re.astype(f32), c_im.astype(f32))
    return jnp.einsum("blgp,ghp->blgh", s, c).real


def ssm_mixer(u, lam_re, lam_im, log_dt, b_re, b_im, c_re, c_im, d_skip, w_glu, b_glu):
    bsz, l, _ = u.shape
    u32 = u.astype(jnp.float32).reshape(bsz, l, SSM_GROUPS, SSM_GROUP)
    y = d_skip.astype(jnp.float32) * u32
    for k in range(N_DIR):
        y = y + s5_direction(u32, lam_re[k], lam_im[k], log_dt[k], b_re[k], b_im[k],
                             c_re[k], c_im[k], reverse=(k == 1))
    y = y.reshape(bsz, l, D_SSM)
    g = jax.nn.gelu(y)
    out = g * jax.nn.sigmoid(g @ w_glu.astype(jnp.float32) + b_glu.astype(jnp.float32))
    return out.astype(u.dtype)


def setup_inputs(seed: int = 0) -> dict:
    key = jax.random.key(seed)
    ks = jax.random.split(key, 24)
    f32 = jnp.float32
    nrm = lambda k, s, scale: (jax.random.normal(k, s, f32) * scale)
    gain = lambda k, s: 1.0 + 0.02 * jax.random.normal(k, s, f32)
    G, H, P = SSM_GROUPS, SSM_GROUP, SSM_STATE
    x = jax.random.normal(ks[0], (BATCH, SEQ, D_MODEL), f32)
    w_in = nrm(ks[1], (DEPTH, D_MODEL, D_MIX), D_MODEL ** -0.5)
    w_out = nrm(ks[2], (DEPTH, D_MIX, D_MODEL), D_MIX ** -0.5)
    pre_mix_g = gain(ks[3], (DEPTH, D_MODEL))
    post_mix_g = gain(ks[4], (DEPTH, D_MODEL))
    pre_mlp_g = gain(ks[5], (DEPTH, D_MODEL))
    post_mlp_g = gain(ks[6], (DEPTH, D_MODEL))
    fourier_out_g = gain(ks[7], (DEPTH, D_FOURIER))
    ssm_out_g = gain(ks[8], (DEPTH, D_SSM))
    w_fourier = nrm(ks[9], (DEPTH, FOURIER_HEADS, FOURIER_HEAD_DIM, FOURIER_HEAD_DIM),
                    FOURIER_HEAD_DIM ** -0.5)
    lam_re = -0.5 + 0.01 * jax.random.normal(ks[10], (DEPTH, N_DIR, G, P), f32)
    lam_im = (jnp.arange(P, dtype=f32) * np.pi)[None, None, None, :] + \
        0.01 * jax.random.normal(ks[11], (DEPTH, N_DIR, G, P), f32)
    log_dt = jax.random.uniform(ks[12], (DEPTH, N_DIR, G), f32,
                                minval=np.log(DT_MIN), maxval=np.log(DT_MAX))
    b_re = nrm(ks[13], (DEPTH, N_DIR, G, P, H), (2.0 * H) ** -0.5)
    b_im = nrm(ks[14], (DEPTH, N_DIR, G, P, H), (2.0 * H) ** -0.5)
    c_re = nrm(ks[15], (DEPTH, N_DIR, G, H, P), 0.5 ** 0.5)
    c_im = nrm(ks[16], (DEPTH, N_DIR, G, H, P), 0.5 ** 0.5)
    d_skip = nrm(ks[17], (DEPTH, G, H), 1.0)
    w_glu = nrm(ks[18], (DEPTH, D_SSM, D_SSM), D_SSM ** -0.5)
    b_glu = nrm(ks[19], (DEPTH, D_SSM), 0.01)
    w_ff1 = nrm(ks[20], (DEPTH, D_MODEL, D_FF), D_MODEL ** -0.5)
    w_ff2 = nrm(ks[21], (DEPTH, D_FF, D_MODEL), D_FF ** -0.5)
    return {"x": x, "w_in": w_in, "w_out": w_out, "pre_mix_g": pre_mix_g,
            "post_mix_g": post_mix_g, "pre_mlp_g": pre_mlp_g, "post_mlp_g": post_mlp_g,
            "fourier_out_g": fourier_out_g, "ssm_out_g": ssm_out_g, "w_fourier": w_fourier,
            "lam_re": lam_re, "lam_im": lam_im, "log_dt": log_dt, "b_re": b_re, "b_im": b_im,
            "c_re": c_re, "c_im": c_im, "d_skip": d_skip, "w_glu": w_glu, "b_glu": b_glu,
            "w_ff1": w_ff1, "w_ff2": w_ff2}


def reference(x, w_in, w_out, pre_mix_g, post_mix_g, pre_mlp_g, post_mlp_g,
              fourier_out_g, ssm_out_g, w_fourier, lam_re, lam_im, log_dt,
              b_re, b_im, c_re, c_im, d_skip, w_glu, b_glu, w_ff1, w_ff2):
    for i in range(DEPTH):
        h = rms_norm(x, pre_mix_g[i])
        z = h @ w_in[i]
        zf = z[..., :D_FOURIER]
        zs = z[..., D_FOURIER:]
        yf = rms_norm(fourier_mixer(zf, w_fourier[i]), fourier_out_g[i])
        ys = rms_norm(ssm_mixer(zs, lam_re[i], lam_im[i], log_dt[i], b_re[i], b_im[i],
                                c_re[i], c_im[i], d_skip[i], w_glu[i], b_glu[i]),
                      ssm_out_g[i])
        m = jnp.concatenate([yf, ys], axis=-1) @ w_out[i]
        x = x + rms_norm(m, post_mix_g[i])
        h = rms_norm(x, pre_mlp_g[i])
        f = jnp.square(jax.nn.relu(h @ w_ff1[i])) @ w_ff2[i]
        x = x + rms_norm(f, post_mlp_g[i])
    return x
```

```python
import functools
import math

import jax
import jax.numpy as jnp
import numpy as np
from jax.experimental import pallas as pl
from jax.experimental.pallas import tpu as pltpu

EPS = 1e-6
D_MODEL = 1024
D_FOURIER = 512
D_SSM = 512
FOURIER_HEADS = 8
FOURIER_HEAD_DIM = 64
SSM_GROUP = 16
SSM_GROUPS = 32
SSM_STATE = 64
D_FF = 4096

FFT_N1 = 64
CHUNK = 16
LANES = 128
SUBLANES = 8
VMEM_LIMIT = 56 * 1024 * 1024

BF16 = jnp.bfloat16
F32 = jnp.float32


def _rms(x, g):
    return x * jax.lax.rsqrt(jnp.mean(x * x, axis=-1, keepdims=True) + EPS) * g


def _dot(a, b):
    return jnp.dot(a, b, preferred_element_type=F32)


def _const_spec(shape):
    nd = len(shape)
    return pl.BlockSpec(shape, lambda *_: (0,) * nd, pipeline_mode=pl.Buffered(1))


def _in_proj_kernel(x_ref, g_ref, w_ref, zf_ref, zs_ref):
    h = _rms(x_ref[...], g_ref[...]).astype(BF16)
    z = _dot(h, w_ref[...])
    zf_ref[...] = z[:, :D_FOURIER].astype(BF16)
    zs_ref[...] = z[:, D_FOURIER:].astype(BF16)


def _in_proj(x2d, g, w_bf16, tm):
    n = x2d.shape[0]
    return pl.pallas_call(
        _in_proj_kernel,
        grid=(n // tm,),
        in_specs=[pl.BlockSpec((tm, D_MODEL), lambda i: (i, 0)),
                  _const_spec((1, D_MODEL)),
                  _const_spec((D_MODEL, D_MODEL))],
        out_specs=[pl.BlockSpec((tm, D_FOURIER), lambda i: (i, 0)),
                   pl.BlockSpec((tm, D_SSM), lambda i: (i, 0))],
        out_shape=[jax.ShapeDtypeStruct((n, D_FOURIER), BF16),
                   jax.ShapeDtypeStruct((n, D_SSM), BF16)],
        compiler_params=pltpu.CompilerParams(
            dimension_semantics=("parallel",), vmem_limit_bytes=VMEM_LIMIT),
        name="in_proj",
    )(x2d, g, w_bf16)


def _fft1_kernel(w_ref, u_ref, y_ref):
    y_ref[...] = _dot(w_ref[...], u_ref[...]).astype(BF16)


def _fft_stage1(u, w1, cols_blk):
    b, n1, cols = u.shape
    return pl.pallas_call(
        _fft1_kernel,
        grid=(b, cols // cols_blk),
        in_specs=[_const_spec((2 * n1, n1)),
                  pl.BlockSpec((None, n1, cols_blk), lambda i, j: (i, 0, j))],
        out_specs=pl.BlockSpec((None, 2 * n1, cols_blk), lambda i, j: (i, 0, j)),
        out_shape=jax.ShapeDtypeStruct((b, 2 * n1, cols), BF16),
        compiler_params=pltpu.CompilerParams(
            dimension_semantics=("parallel", "parallel"), vmem_limit_bytes=VMEM_LIMIT),
        name="fft_stage1",
    )(w1, u)


def _fft2_kernel(m_ref, y_ref, g_ref, *, kb):
    for j in range(kb):
        g_ref[j] = _dot(m_ref[j], y_ref[j]).astype(BF16)


def _fft_stage2(y, m2, kb):
    b, n1, rows, c = y.shape
    return pl.pallas_call(
        functools.partial(_fft2_kernel, kb=kb),
        grid=(b, n1 // kb),
        in_specs=[pl.BlockSpec((kb, rows, rows), lambda i, j: (j, 0, 0)),
                  pl.BlockSpec((None, kb, rows, c), lambda i, j: (i, j, 0, 0))],
        out_specs=pl.BlockSpec((None, kb, rows, c), lambda i, j: (i, j, 0, 0)),
        out_shape=jax.ShapeDtypeStruct((b, n1, rows, c), BF16),
        compiler_params=pltpu.CompilerParams(
            dimension_semantics=("parallel", "parallel"), vmem_limit_bytes=VMEM_LIMIT),
        name="fft_stage2",
    )(m2, y)


def _fft_tables(seq):
    n1, n2 = FFT_N1, seq // FFT_N1
    k1 = np.arange(n1)
    ang1 = 2.0 * np.pi * ((k1[:, None] * k1[None, :]) % n1) / n1
    w1 = np.stack([np.cos(ang1), -np.sin(ang1)], axis=1).reshape(2 * n1, n1)
    k2 = np.arange(n2)
    freq = k1[:, None, None] + n1 * k2[None, :, None]
    ang2 = 2.0 * np.pi * ((freq * k2[None, None, :]) % seq) / seq
    mr, mi = np.cos(ang2), -np.sin(ang2)
    m2 = np.concatenate([np.concatenate([mr, -mi], axis=2),
                         np.concatenate([mi, mr], axis=2)], axis=1)
    return jnp.asarray(w1, F32).astype(BF16), jnp.asarray(m2, F32).astype(BF16)


def _cmul(rr, sw, x):
    return rr * x + sw * pltpu.roll(x, SSM_STATE, axis=1)


def _ssm_kernel(u_ref, mi_ref, bs_ref, cs_ref, tab_ref, y_ref, s_ref, car_ref, *, nb, nc):
    u = u_ref[...]
    s_ref[...] = _dot(u, bs_ref[...])
    row = jax.lax.broadcasted_iota(jnp.int32, (SUBLANES, LANES), 0)
    nblk = nc // SUBLANES
    fw = [tab_ref[0, k] for k in range(8)]
    bw = [tab_ref[1, k] for k in range(8)]

    def fwd_block(s, carry):
        s = s + _cmul(fw[0], fw[1], jnp.where(row >= 1, pltpu.roll(s, 1, axis=0), 0.0))
        s = s + _cmul(fw[2], fw[3], jnp.where(row >= 2, pltpu.roll(s, 2, axis=0), 0.0))
        s = s + _cmul(fw[4], fw[5], jnp.where(row >= 4, pltpu.roll(s, 4, axis=0), 0.0))
        full = s + _cmul(fw[6], fw[7], carry)
        prev = jnp.where(row >= 1, pltpu.roll(full, 1, axis=0), carry)
        return prev, jnp.broadcast_to(full[SUBLANES - 1:SUBLANES, :], (SUBLANES, LANES))

    def bwd_block(s, carry):
        s = s + _cmul(bw[0], bw[1], jnp.where(row <= 6, pltpu.roll(s, 7, axis=0), 0.0))
        s = s + _cmul(bw[2], bw[3], jnp.where(row <= 5, pltpu.roll(s, 6, axis=0), 0.0))
        s = s + _cmul(bw[4], bw[5], jnp.where(row <= 3, pltpu.roll(s, 4, axis=0), 0.0))
        full = s + _cmul(bw[6], bw[7], carry)
        nxt = jnp.where(row <= 6, pltpu.roll(full, 7, axis=0), carry)
        return nxt, jnp.broadcast_to(full[0:1, :], (SUBLANES, LANES))

    def body(i, carry):
        new = []
        for bi in range(nb):
            rf = pl.multiple_of(bi * nc + i * SUBLANES, SUBLANES)
            prev, cf = fwd_block(s_ref[pl.ds(rf, SUBLANES), 0:LANES], carry[2 * bi])
            car_ref[pl.ds(rf, SUBLANES), 0:LANES] = prev
            rb = pl.multiple_of(bi * nc + (nblk - 1 - i) * SUBLANES, SUBLANES)
            nxt, cb = bwd_block(s_ref[pl.ds(rb, SUBLANES), LANES:2 * LANES], carry[2 * bi + 1])
            car_ref[pl.ds(rb, SUBLANES), LANES:2 * LANES] = nxt
            new += [cf, cb]
        return tuple(new)

    zero = jnp.zeros((SUBLANES, LANES), F32)
    jax.lax.fori_loop(0, nblk, body, (zero,) * (2 * nb))
    y = _dot(u, mi_ref[...]) + _dot(car_ref[...].astype(BF16), cs_ref[...])
    y_ref[...] = y.astype(BF16)


def _ssm_chunked(u2, m_intra, b_st, c_st, tab, nb, nc):
    g, rows, k = u2.shape
    return pl.pallas_call(
        functools.partial(_ssm_kernel, nb=nb, nc=nc),
        grid=(g,),
        in_specs=[pl.BlockSpec((None, rows, k), lambda i: (i, 0, 0)),
                  pl.BlockSpec((None, k, k), lambda i: (i, 0, 0)),
                  pl.BlockSpec((None, k, k), lambda i: (i, 0, 0)),
                  pl.BlockSpec((None, k, k), lambda i: (i, 0, 0)),
                  pl.BlockSpec((None, 2, 8, SUBLANES, LANES), lambda i: (i, 0, 0, 0, 0))],
        out_specs=pl.BlockSpec((None, rows, k), lambda i: (i, 0, 0)),
        out_shape=jax.ShapeDtypeStruct((g, rows, k), BF16),
        scratch_shapes=[pltpu.VMEM((rows, k), F32), pltpu.VMEM((rows, k), F32)],
        compiler_params=pltpu.CompilerParams(
            dimension_semantics=("parallel",), vmem_limit_bytes=VMEM_LIMIT),
        name="ssm_chunk",
    )(u2, m_intra, b_st, c_st, tab)


def _ssm_params(lam_re, lam_im, log_dt, b_re, b_im, c_re, c_im, d_skip):
    t = CHUNK
    hp = jax.lax.Precision.HIGHEST
    dt = jnp.exp(log_dt.astype(F32))[..., None]
    are = lam_re.astype(F32) * dt
    aim = lam_im.astype(F32) * dt

    def power(k):
        kk = jnp.asarray(k, F32).reshape((-1, 1, 1, 1))
        mag = jnp.exp(are[None] * kk)
        return mag * jnp.cos(aim[None] * kk), mag * jnp.sin(aim[None] * kk)

    lbr, lbi = power([1.0])
    lbr, lbi = lbr[0], lbi[0]
    den = lam_re * lam_re + lam_im * lam_im
    cr = ((lbr - 1.0) * lam_re + lbi * lam_im) / den
    ci = (lbi * lam_re - (lbr - 1.0) * lam_im) / den
    bbr = cr[..., None] * b_re - ci[..., None] * b_im
    bbi = cr[..., None] * b_im + ci[..., None] * b_re

    pr, pi = power(np.arange(t + 1))
    cpr = c_re[None] * pr[:, :, :, None, :] - c_im[None] * pi[:, :, :, None, :]
    cpi = c_re[None] * pi[:, :, :, None, :] + c_im[None] * pr[:, :, :, None, :]
    kern = (jnp.einsum("ldgop,dgpi->ldgoi", cpr, bbr, precision=hp)
            - jnp.einsum("ldgop,dgpi->ldgoi", cpi, bbi, precision=hp))
    eye = jnp.eye(SSM_GROUP, dtype=F32)
    k0 = kern[0, 0] + kern[0, 1] + d_skip.astype(F32)[:, :, None] * eye[None]
    lags = jnp.concatenate([kern[1:t, 1][::-1], k0[None], kern[1:t, 0]], axis=0)
    tt = np.arange(t)
    idx = tt[None, :] - tt[:, None] + (t - 1)
    m_intra = lags[idx]
    m_intra = m_intra.transpose(2, 0, 4, 1, 3).reshape(SSM_GROUPS, t * SSM_GROUP, t * SSM_GROUP)

    pw_r = jnp.stack([pr[:t, 0][::-1], pr[:t, 1]], axis=1)
    pw_i = jnp.stack([pi[:t, 0][::-1], pi[:t, 1]], axis=1)
    sbr = pw_r[..., None] * bbr[None] - pw_i[..., None] * bbi[None]
    sbi = pw_r[..., None] * bbi[None] + pw_i[..., None] * bbr[None]
    b_st = jnp.stack([sbr, sbi], axis=2)
    b_st = b_st.transpose(3, 0, 5, 1, 2, 4).reshape(SSM_GROUPS, t * SSM_GROUP, 4 * SSM_STATE)

    qw_r = jnp.stack([pr[1:, 0], pr[1:, 1][::-1]], axis=1)
    qw_i = jnp.stack([pi[1:, 0], pi[1:, 1][::-1]], axis=1)
    cqr = c_re[None] * qw_r[:, :, :, None, :] - c_im[None] * qw_i[:, :, :, None, :]
    cqi = c_re[None] * qw_i[:, :, :, None, :] + c_im[None] * qw_r[:, :, :, None, :]
    c_st = jnp.stack([cqr, -cqi], axis=2)
    c_st = c_st.transpose(3, 1, 2, 5, 0, 4).reshape(SSM_GROUPS, 4 * SSM_STATE, t * SSM_GROUP)

    ar, ai = power(t * np.arange(1, SUBLANES + 1))

    def pair(re, im):
        return jnp.concatenate([re, re], -1), jnp.concatenate([-im, im], -1)

    def rows8(x):
        return jnp.broadcast_to(x[:, :, None, :], x.shape[:2] + (SUBLANES, x.shape[-1]))

    slots = []
    for k in (0, 1, 3):
        rr, sw = pair(ar[k], ai[k])
        slots += [rows8(rr), rows8(sw)]
    prr, psw = pair(ar, ai)
    prr = prr.transpose(1, 2, 0, 3)
    psw = psw.transpose(1, 2, 0, 3)
    prr = jnp.stack([prr[0], prr[1][:, ::-1]], axis=0)
    psw = jnp.stack([psw[0], psw[1][:, ::-1]], axis=0)
    slots += [prr, psw]
    tab = jnp.stack(slots, axis=2).transpose(1, 0, 2, 3, 4)
    return m_intra.astype(BF16), b_st.astype(BF16), c_st.astype(BF16), tab.astype(F32)


def _mix_mlp_kernel(x_ref, gr_ref, gi_ref, ys_ref, bdr_ref, bdi_ref, fg_ref, wg_ref, bg_ref,
                    sg_ref, wo_ref, pmg_ref, plg_ref, w1_ref, w2_ref, pog_ref, o_ref, *, ff_blk):
    yf = _dot(gr_ref[...], bdr_ref[...]) + _dot(gi_ref[...], bdi_ref[...])
    yf = _rms(yf, fg_ref[...])
    y = ys_ref[...].astype(F32)
    gl = jax.nn.gelu(y, approximate=True)
    gate = _dot(gl.astype(BF16), wg_ref[...]) + bg_ref[...]
    ys = _rms(gl * jax.nn.sigmoid(gate), sg_ref[...])
    cat = jnp.concatenate([yf.astype(BF16), ys.astype(BF16)], axis=-1)
    x1 = x_ref[...] + _rms(_dot(cat, wo_ref[...]), pmg_ref[...])
    h = _rms(x1, plg_ref[...]).astype(BF16)
    acc = jnp.zeros(x1.shape, F32)
    for j in range(D_FF // ff_blk):
        a = _dot(h, w1_ref[:, j * ff_blk:(j + 1) * ff_blk])
        a = jnp.square(jnp.maximum(a, 0.0)).astype(BF16)
        acc = acc + _dot(a, w2_ref[j * ff_blk:(j + 1) * ff_blk, :])
    o_ref[...] = x1 + _rms(acc, pog_ref[...])


def _mix_mlp(x2d, gri, ys, bdr, bdi, fg, wg, bg, sg, wo, pmg, plg, w1, w2, pog, tm, ff_blk):
    n = x2d.shape[0]
    row = lambda i: (i, 0)
    return pl.pallas_call(
        functools.partial(_mix_mlp_kernel, ff_blk=ff_blk),
        grid=(n // tm,),
        in_specs=[pl.BlockSpec((tm, D_MODEL), row),
                  pl.BlockSpec((None, tm, D_FOURIER), lambda i: (0, i, 0)),
                  pl.BlockSpec((None, tm, D_FOURIER), lambda i: (1, i, 0)),
                  pl.BlockSpec((tm, D_SSM), row),
                  _const_spec((D_FOURIER, D_FOURIER)), _const_spec((D_FOURIER, D_FOURIER)),
                  _const_spec((1, D_FOURIER)),
                  _const_spec((D_SSM, D_SSM)), _const_spec((1, D_SSM)), _const_spec((1, D_SSM)),
                  _const_spec((D_MODEL, D_MODEL)), _const_spec((1, D_MODEL)),
                  _const_spec((1, D_MODEL)),
                  _const_spec((D_MODEL, D_FF)), _const_spec((D_FF, D_MODEL)),
                  _const_spec((1, D_MODEL))],
        out_specs=pl.BlockSpec((tm, D_MODEL), row),
        out_shape=jax.ShapeDtypeStruct((n, D_MODEL), F32),
        compiler_params=pltpu.CompilerParams(
            dimension_semantics=("parallel",), vmem_limit_bytes=VMEM_LIMIT),
        name="mix_mlp",
    )(x2d, gri, gri, ys, bdr, bdi, fg, wg, bg, sg, wo, pmg, plg, w1, w2, pog)


def _fourier_channel_maps(w_f, seq):
    c = np.arange(FOURIER_HEAD_DIM)
    ang = 2.0 * np.pi * ((c[:, None] * c[None, :]) % FOURIER_HEAD_DIM) / FOURIER_HEAD_DIM
    scale = 1.0 / math.sqrt(seq * FOURIER_HEAD_DIM)
    hp = jax.lax.Precision.HIGHEST
    cw = jnp.einsum("cd,hde->hce", jnp.asarray(np.cos(ang) * scale, F32), w_f.astype(F32), precision=hp)
    sw = jnp.einsum("cd,hde->hce", jnp.asarray(np.sin(ang) * scale, F32), w_f.astype(F32), precision=hp)
    eye = jnp.eye(FOURIER_HEADS, dtype=F32)
    bd = lambda m: jnp.einsum("hce,hk->hcke", m, eye).reshape(D_FOURIER, D_FOURIER)
    return bd(cw).astype(BF16), bd(sw).astype(BF16)


def kernel(x, w_in, w_out, pre_mix_g, post_mix_g, pre_mlp_g, post_mlp_g, fourier_out_g, ssm_out_g,
           w_fourier, lam_re, lam_im, log_dt, b_re, b_im, c_re, c_im, d_skip, w_glu, b_glu,
           w_ff1, w_ff2):
    bsz, seq, _ = x.shape
    depth = w_in.shape[0]
    n = bsz * seq
    n1, n2 = FFT_N1, seq // FFT_N1
    nc = seq // CHUNK
    assert seq % (FFT_N1 * SUBLANES) == 0 and nc % SUBLANES == 0
    tm = min(512, n)
    w1_tab, m2_tab = _fft_tables(seq)
    row = lambda v: v.reshape(1, -1).astype(F32)

    x2d = x.reshape(n, D_MODEL).astype(F32)
    for i in range(depth):
        zf, zs = _in_proj(x2d, row(pre_mix_g[i]), w_in[i].astype(BF16), tm)

        u = zf.reshape(bsz, n1, n2 * D_FOURIER)
        y = _fft_stage1(u, w1_tab, min(8192, n2 * D_FOURIER))
        y = y.reshape(bsz, n1, 2 * n2, D_FOURIER)
        g = _fft_stage2(y, m2_tab, kb=4)
        gri = g.reshape(bsz, n1, 2, n2, D_FOURIER).transpose(2, 0, 3, 1, 4).reshape(2, n, D_FOURIER)

        m_intra, b_st, c_st, tab = _ssm_params(lam_re[i], lam_im[i], log_dt[i], b_re[i], b_im[i],
                                               c_re[i], c_im[i], d_skip[i])
        u2 = zs.reshape(bsz * nc, CHUNK, SSM_GROUPS, SSM_GROUP).transpose(2, 0, 1, 3)
        u2 = u2.reshape(SSM_GROUPS, bsz * nc, CHUNK * SSM_GROUP)
        y2 = _ssm_chunked(u2, m_intra, b_st, c_st, tab, bsz, nc)
        ysr = y2.reshape(SSM_GROUPS, bsz * nc, CHUNK, SSM_GROUP).transpose(1, 2, 0, 3).reshape(n, D_SSM)

        bdr, bdi = _fourier_channel_maps(w_fourier[i], seq)
        x2d = _mix_mlp(x2d, gri, ysr, bdr, bdi, row(fourier_out_g[i]), w_glu[i].astype(BF16),
                       row(b_glu[i]), row(ssm_out_g[i]), w_out[i].astype(BF16),
                       row(post_mix_g[i]), row(pre_mlp_g[i]), w_ff1[i].astype(BF16),
                       w_ff2[i].astype(BF16), row(post_mlp_g[i]), tm, ff_blk=1024)
    return x2d.reshape(bsz, seq, D_MODEL).astype(x.dtype)
```

```python
import functools
import math

import jax
import jax.numpy as jnp
import numpy as np
from jax.experimental import pallas as pl
from jax.experimental.pallas import tpu as pltpu

EPS = 1e-6
D_MODEL = 1024
D_FOURIER = 512
D_SSM = 512
FOURIER_HEADS = 8
FOURIER_HEAD_DIM = 64
SSM_GROUP = 16
SSM_GROUPS = 32
SSM_STATE = 64
D_FF = 4096

FFT_N1 = 64
CHUNK = 16
LANES = 128
SUBLANES = 8
VMEM_LIMIT = 56 * 1024 * 1024

BF16 = jnp.bfloat16
F32 = jnp.float32


def _rms(x, g):
    return x * jax.lax.rsqrt(jnp.mean(x * x, axis=-1, keepdims=True) + EPS) * g


def _dot(a, b):
    return jnp.dot(a, b, preferred_element_type=F32)


def _const_spec(shape):
    nd = len(shape)
    return pl.BlockSpec(shape, lambda *_: (0,) * nd, pipeline_mode=pl.Buffered(1))


def _in_proj_kernel(x_ref, g_ref, w_ref, zf_ref, zs_ref):
    h = _rms(x_ref[...], g_ref[...]).astype(BF16)
    z = _dot(h, w_ref[...])
    zf_ref[...] = z[:, :D_FOURIER].astype(BF16)
    zs_ref[...] = z[:, D_FOURIER:]


def _in_proj(x2d, g, w_bf16, tm):
    n = x2d.shape[0]
    return pl.pallas_call(
        _in_proj_kernel,
        grid=(n // tm,),
        in_specs=[pl.BlockSpec((tm, D_MODEL), lambda i: (i, 0)),
                  _const_spec((1, D_MODEL)),
                  _const_spec((D_MODEL, D_MODEL))],
        out_specs=[pl.BlockSpec((tm, D_FOURIER), lambda i: (i, 0)),
                   pl.BlockSpec((tm, D_SSM), lambda i: (i, 0))],
        out_shape=[jax.ShapeDtypeStruct((n, D_FOURIER), BF16),
                   jax.ShapeDtypeStruct((n, D_SSM), F32)],
        compiler_params=pltpu.CompilerParams(
            dimension_semantics=("parallel",), vmem_limit_bytes=VMEM_LIMIT),
        name="in_proj",
    )(x2d, g, w_bf16)


def _fft1_kernel(w_ref, u_ref, y_ref):
    y_ref[...] = _dot(w_ref[...], u_ref[...]).astype(BF16)


def _fft_stage1(u, w1, cols_blk):
    b, n1, cols = u.shape
    return pl.pallas_call(
        _fft1_kernel,
        grid=(b, cols // cols_blk),
        in_specs=[_const_spec((2 * n1, n1)),
                  pl.BlockSpec((None, n1, cols_blk), lambda i, j: (i, 0, j))],
        out_specs=pl.BlockSpec((None, 2 * n1, cols_blk), lambda i, j: (i, 0, j)),
        out_shape=jax.ShapeDtypeStruct((b, 2 * n1, cols), BF16),
        compiler_params=pltpu.CompilerParams(
            dimension_semantics=("parallel", "parallel"), vmem_limit_bytes=VMEM_LIMIT),
        name="fft_stage1",
    )(w1, u)


def _fft2_kernel(m_ref, y_ref, g_ref, *, kb):
    for j in range(kb):
        g_ref[j] = _dot(m_ref[j], y_ref[j]).astype(BF16)


def _fft_stage2(y, m2, kb):
    b, n1, rows, c = y.shape
    return pl.pallas_call(
        functools.partial(_fft2_kernel, kb=kb),
        grid=(b, n1 // kb),
        in_specs=[pl.BlockSpec((kb, rows, rows), lambda i, j: (j, 0, 0)),
                  pl.BlockSpec((None, kb, rows, c), lambda i, j: (i, j, 0, 0))],
        out_specs=pl.BlockSpec((None, kb, rows, c), lambda i, j: (i, j, 0, 0)),
        out_shape=jax.ShapeDtypeStruct((b, n1, rows, c), BF16),
        compiler_params=pltpu.CompilerParams(
            dimension_semantics=("parallel", "parallel"), vmem_limit_bytes=VMEM_LIMIT),
        name="fft_stage2",
    )(m2, y)


def _fft_tables(seq):
    n1, n2 = FFT_N1, seq // FFT_N1
    k1 = np.arange(n1)
    ang1 = 2.0 * np.pi * ((k1[:, None] * k1[None, :]) % n1) / n1
    w1 = np.stack([np.cos(ang1), -np.sin(ang1)], axis=1).reshape(2 * n1, n1)
    k2 = np.arange(n2)
    freq = k1[:, None, None] + n1 * k2[None, :, None]
    ang2 = 2.0 * np.pi * ((freq * k2[None, None, :]) % seq) / seq
    mr, mi = np.cos(ang2), -np.sin(ang2)
    m2 = np.concatenate([np.concatenate([mr, -mi], axis=2),
                         np.concatenate([mi, mr], axis=2)], axis=1)
    return jnp.asarray(w1, F32).astype(BF16), jnp.asarray(m2, F32).astype(BF16)


GROUPS_PER_BLOCK = LANES // SSM_GROUP
PAIRS_PER_BLOCK = GROUPS_PER_BLOCK // 2


def _ssm_kernel(z_ref, mi_ref, bs_ref, cs_ref, tab_ref, y_ref,
                zr_ref, u2_ref, s_ref, car_ref, yg_ref, *, nc):
    gpb, npair = GROUPS_PER_BLOCK, PAIRS_PER_BLOCK
    t16 = 2 * SUBLANES
    rows_it = t16 * CHUNK
    lane_blk = jax.lax.broadcasted_iota(jnp.int32, (t16, LANES), 1) // SSM_GROUP
    masks = [lane_blk == m for m in range(gpb)]
    row_it = jax.lax.broadcasted_iota(jnp.int32, (rows_it, LANES), 0)

    def relayout_in(i, _):
        r0 = pl.multiple_of(i * rows_it, rows_it)
        zr_ref[pl.ds(r0, rows_it), :] = pltpu.roll(
            z_ref[pl.ds(r0, rows_it), :], 0, axis=1, stride=SSM_GROUP, stride_axis=0)
        ut = [zr_ref[pl.ds(r0 + t, t16, stride=CHUNK), :] for t in range(CHUNK)]
        c0 = pl.multiple_of(i * t16, t16)
        for g in range(gpb):
            for j in range(CHUNK // gpb):
                acc = ut[gpb * j]
                for tt in range(1, gpb):
                    acc = jnp.where(masks[(g + tt) % gpb], ut[gpb * j + tt], acc)
                if g:
                    acc = pltpu.roll(acc, LANES - SSM_GROUP * g, axis=1)
                u2_ref[g, pl.ds(c0, t16), j * LANES:(j + 1) * LANES] = acc.astype(BF16)
        return 0

    jax.lax.fori_loop(0, nc // t16, relayout_in, 0)

    for q in range(npair):
        up = jnp.concatenate([u2_ref[2 * q], u2_ref[2 * q + 1]], axis=1)
        s_ref[q] = _dot(up, bs_ref[q])

    row = jax.lax.broadcasted_iota(jnp.int32, (SUBLANES, LANES), 0)
    nblk = nc // SUBLANES

    def scan_block(tab, sr, si, cr, ci, down):
        keep = (lambda sh: row >= sh) if down else (lambda sh: row <= SUBLANES - 1 - sh)
        amt = (lambda sh: sh) if down else (lambda sh: SUBLANES - sh)
        for k, sh in enumerate((1, 2, 4)):
            tr = jnp.where(keep(sh), pltpu.roll(sr, amt(sh), axis=0), 0.0)
            ti = jnp.where(keep(sh), pltpu.roll(si, amt(sh), axis=0), 0.0)
            ar, ai = tab[2 * k], tab[2 * k + 1]
            sr, si = sr + (ar * tr - ai * ti), si + (ar * ti + ai * tr)
        fr = sr + (tab[6] * cr - tab[7] * ci)
        fi = si + (tab[6] * ci + tab[7] * cr)
        outr = jnp.where(keep(1), pltpu.roll(fr, amt(1), axis=0), cr)
        outi = jnp.where(keep(1), pltpu.roll(fi, amt(1), axis=0), ci)
        edge = SUBLANES - 1 if down else 0
        ncr = jnp.broadcast_to(fr[edge:edge + 1, :], (SUBLANES, LANES))
        nci = jnp.broadcast_to(fi[edge:edge + 1, :], (SUBLANES, LANES))
        return outr, outi, ncr, nci

    def scan_body(i, carry):
        new = []
        for q in range(npair):
            for d in range(2):
                blk = i if d == 0 else nblk - 1 - i
                r0 = pl.multiple_of(blk * SUBLANES, SUBLANES)
                lo = 2 * d * LANES
                tab = [tab_ref[q, d, k] for k in range(8)]
                cr, ci = carry[4 * q + 2 * d], carry[4 * q + 2 * d + 1]
                outr, outi, cr, ci = scan_block(
                    tab, s_ref[q, pl.ds(r0, SUBLANES), lo:lo + LANES],
                    s_ref[q, pl.ds(r0, SUBLANES), lo + LANES:lo + 2 * LANES], cr, ci, d == 0)
                car_ref[q, pl.ds(r0, SUBLANES), lo:lo + LANES] = outr
                car_ref[q, pl.ds(r0, SUBLANES), lo + LANES:lo + 2 * LANES] = outi
                new += [cr, ci]
        return tuple(new)

    zero = jnp.zeros((SUBLANES, LANES), F32)
    jax.lax.fori_loop(0, nblk, scan_body, (zero,) * (4 * npair))

    kk = CHUNK * SSM_GROUP
    for q in range(npair):
        yi = _dot(car_ref[q].astype(BF16), cs_ref[q])
        for e in range(2):
            g = 2 * q + e
            yg_ref[g] = _dot(u2_ref[g], mi_ref[g]) + yi[:, e * kk:(e + 1) * kk]

    def relayout_out(i, _):
        c0 = pl.multiple_of(i * t16, t16)
        r0 = pl.multiple_of(i * rows_it, rows_it)
        yrot = []
        for g in range(gpb):
            halves = []
            for j in range(CHUNK // gpb):
                v = yg_ref[g, pl.ds(c0, t16), j * LANES:(j + 1) * LANES]
                halves.append(pltpu.roll(v, SSM_GROUP * g, axis=1) if g else v)
            yrot.append(halves)
        for t in range(CHUNK):
            j, tt = divmod(t, gpb)
            acc = yrot[0][j]
            for g in range(1, gpb):
                acc = jnp.where(masks[(g + tt) % gpb], yrot[g][j], acc)
            zr_ref[pl.ds(r0 + t, t16, stride=CHUNK), :] = acc
        v = zr_ref[pl.ds(r0, rows_it), :]
        for bit in (1, 2, 4):
            v = jnp.where((row_it & bit) != 0,
                          pltpu.roll(v, LANES - SSM_GROUP * bit, axis=1), v)
        y_ref[pl.ds(r0, rows_it), :] = v.astype(BF16)
        return 0

    jax.lax.fori_loop(0, nc // t16, relayout_out, 0)


def _ssm_chunked(zs, m_intra, b_pair, c_pair, tab, bsz, seq):
    nc = seq // CHUNK
    gpb, npair = GROUPS_PER_BLOCK, PAIRS_PER_BLOCK
    k = CHUNK * SSM_GROUP
    return pl.pallas_call(
        functools.partial(_ssm_kernel, nc=nc),
        grid=(D_SSM // LANES, bsz),
        in_specs=[pl.BlockSpec((seq, LANES), lambda j, b: (b, j)),
                  pl.BlockSpec((gpb, k, k), lambda j, b: (j, 0, 0)),
                  pl.BlockSpec((npair, 2 * k, 2 * k), lambda j, b: (j, 0, 0)),
                  pl.BlockSpec((npair, 2 * k, 2 * k), lambda j, b: (j, 0, 0)),
                  pl.BlockSpec((npair, 2, 8, SUBLANES, LANES), lambda j, b: (j, 0, 0, 0, 0))],
        out_specs=pl.BlockSpec((seq, LANES), lambda j, b: (b, j)),
        out_shape=jax.ShapeDtypeStruct((bsz * seq, D_SSM), BF16),
        scratch_shapes=[pltpu.VMEM((seq, LANES), F32),
                        pltpu.VMEM((gpb, nc, k), BF16),
                        pltpu.VMEM((npair, nc, 2 * k), F32),
                        pltpu.VMEM((npair, nc, 2 * k), F32),
                        pltpu.VMEM((gpb, nc, k), F32)],
        compiler_params=pltpu.CompilerParams(
            dimension_semantics=("parallel", "parallel"), vmem_limit_bytes=VMEM_LIMIT),
        name="ssm_chunk",
    )(zs, m_intra, b_pair, c_pair, tab)


def _ssm_params(lam_re, lam_im, log_dt, b_re, b_im, c_re, c_im, d_skip):
    t = CHUNK
    hp = jax.lax.Precision.HIGHEST
    dt = jnp.exp(log_dt.astype(F32))[..., None]
    are = lam_re.astype(F32) * dt
    aim = lam_im.astype(F32) * dt

    def power(k):
        kk = jnp.asarray(k, F32).reshape((-1, 1, 1, 1))
        mag = jnp.exp(are[None] * kk)
        return mag * jnp.cos(aim[None] * kk), mag * jnp.sin(aim[None] * kk)

    lbr, lbi = power([1.0])
    lbr, lbi = lbr[0], lbi[0]
    den = lam_re * lam_re + lam_im * lam_im
    cr = ((lbr - 1.0) * lam_re + lbi * lam_im) / den
    ci = (lbi * lam_re - (lbr - 1.0) * lam_im) / den
    bbr = cr[..., None] * b_re - ci[..., None] * b_im
    bbi = cr[..., None] * b_im + ci[..., None] * b_re

    pr, pi = power(np.arange(t + 1))
    cpr = c_re[None] * pr[:, :, :, None, :] - c_im[None] * pi[:, :, :, None, :]
    cpi = c_re[None] * pi[:, :, :, None, :] + c_im[None] * pr[:, :, :, None, :]
    kern = (jnp.einsum("ldgop,dgpi->ldgoi", cpr, bbr, precision=hp)
            - jnp.einsum("ldgop,dgpi->ldgoi", cpi, bbi, precision=hp))
    eye = jnp.eye(SSM_GROUP, dtype=F32)
    k0 = kern[0, 0] + kern[0, 1] + d_skip.astype(F32)[:, :, None] * eye[None]
    lags = jnp.concatenate([kern[1:t, 1][::-1], k0[None], kern[1:t, 0]], axis=0)
    tt = np.arange(t)
    idx = tt[None, :] - tt[:, None] + (t - 1)
    m_intra = lags[idx]
    m_intra = m_intra.transpose(2, 0, 4, 1, 3).reshape(SSM_GROUPS, t * SSM_GROUP, t * SSM_GROUP)

    pw_r = jnp.stack([pr[:t, 0][::-1], pr[:t, 1]], axis=1)
    pw_i = jnp.stack([pi[:t, 0][::-1], pi[:t, 1]], axis=1)
    sbr = pw_r[..., None] * bbr[None] - pw_i[..., None] * bbi[None]
    sbi = pw_r[..., None] * bbi[None] + pw_i[..., None] * bbr[None]
    b_st = jnp.stack([sbr, sbi], axis=2)
    b_st = b_st.transpose(3, 0, 5, 1, 2, 4).reshape(SSM_GROUPS, t * SSM_GROUP, 4 * SSM_STATE)

    qw_r = jnp.stack([pr[1:, 0], pr[1:, 1][::-1]], axis=1)
    qw_i = jnp.stack([pi[1:, 0], pi[1:, 1][::-1]], axis=1)
    cqr = c_re[None] * qw_r[:, :, :, None, :] - c_im[None] * qw_i[:, :, :, None, :]
    cqi = c_re[None] * qw_i[:, :, :, None, :] + c_im[None] * qw_r[:, :, :, None, :]
    c_st = jnp.stack([cqr, -cqi], axis=2)
    c_st = c_st.transpose(3, 1, 2, 5, 0, 4).reshape(SSM_GROUPS, 4 * SSM_STATE, t * SSM_GROUP)

    ar, ai = power(t * np.arange(1, SUBLANES + 1))

    nq = SSM_GROUPS // 2
    ar = ar.reshape(SUBLANES, 2, nq, LANES)
    ai = ai.reshape(SUBLANES, 2, nq, LANES)

    def rows8(x):
        return jnp.broadcast_to(x[:, :, None, :], x.shape[:2] + (SUBLANES, x.shape[-1]))

    def edge_rows(x):
        x = x.transpose(1, 2, 0, 3)
        return jnp.stack([x[0], x[1][:, ::-1]], axis=0)

    slots = []
    for k in (0, 1, 3):
        slots += [rows8(ar[k]), rows8(ai[k])]
    slots += [edge_rows(ar), edge_rows(ai)]
    tab = jnp.stack(slots, axis=2).transpose(1, 0, 2, 3, 4)

    eye2 = jnp.eye(2, dtype=F32)
    kk = t * SSM_GROUP
    b_pair = jnp.einsum("qgrbp,gh->qgrbhp", b_st.reshape(nq, 2, kk, 4, SSM_STATE), eye2)
    b_pair = b_pair.reshape(nq, 2 * kk, 8 * SSM_STATE)
    c_pair = jnp.einsum("qgbpc,gh->qbhpgc", c_st.reshape(nq, 2, 4, SSM_STATE, kk), eye2)
    c_pair = c_pair.reshape(nq, 8 * SSM_STATE, 2 * kk)
    return m_intra.astype(BF16), b_pair.astype(BF16), c_pair.astype(BF16), tab.astype(F32)


def _mix_mlp_kernel(x_ref, gr_ref, gi_ref, ys_ref, bdr_ref, bdi_ref, fg_ref, wg_ref, bg_ref,
                    sg_ref, wo_ref, pmg_ref, plg_ref, w1_ref, w2_ref, pog_ref, o_ref, *, ff_blk):
    yf = _dot(gr_ref[...], bdr_ref[...]) + _dot(gi_ref[...], bdi_ref[...])
    yf = _rms(yf, fg_ref[...])
    y = ys_ref[...].astype(F32)
    gl = jax.nn.gelu(y, approximate=True)
    gate = _dot(gl.astype(BF16), wg_ref[...]) + bg_ref[...]
    ys = _rms(gl * jax.nn.sigmoid(gate), sg_ref[...])
    cat = jnp.concatenate([yf.astype(BF16), ys.astype(BF16)], axis=-1)
    x1 = x_ref[...] + _rms(_dot(cat, wo_ref[...]), pmg_ref[...])
    h = _rms(x1, plg_ref[...]).astype(BF16)
    acc = jnp.zeros(x1.shape, F32)
    for j in range(D_FF // ff_blk):
        a = _dot(h, w1_ref[:, j * ff_blk:(j + 1) * ff_blk])
        a = jnp.square(jnp.maximum(a, 0.0)).astype(BF16)
        acc = acc + _dot(a, w2_ref[j * ff_blk:(j + 1) * ff_blk, :])
    o_ref[...] = x1 + _rms(acc, pog_ref[...])


def _mix_mlp(x2d, gri, ys, bdr, bdi, fg, wg, bg, sg, wo, pmg, plg, w1, w2, pog, tm, ff_blk):
    n = x2d.shape[0]
    row = lambda i: (i, 0)
    return pl.pallas_call(
        functools.partial(_mix_mlp_kernel, ff_blk=ff_blk),
        grid=(n // tm,),
        in_specs=[pl.BlockSpec((tm, D_MODEL), row),
                  pl.BlockSpec((None, tm, D_FOURIER), lambda i: (0, i, 0)),
                  pl.BlockSpec((None, tm, D_FOURIER), lambda i: (1, i, 0)),
                  pl.BlockSpec((tm, D_SSM), row),
                  _const_spec((D_FOURIER, D_FOURIER)), _const_spec((D_FOURIER, D_FOURIER)),
                  _const_spec((1, D_FOURIER)),
                  _const_spec((D_SSM, D_SSM)), _const_spec((1, D_SSM)), _const_spec((1, D_SSM)),
                  _const_spec((D_MODEL, D_MODEL)), _const_spec((1, D_MODEL)),
                  _const_spec((1, D_MODEL)),
                  _const_spec((D_MODEL, D_FF)), _const_spec((D_FF, D_MODEL)),
                  _const_spec((1, D_MODEL))],
        out_specs=pl.BlockSpec((tm, D_MODEL), row),
        out_shape=jax.ShapeDtypeStruct((n, D_MODEL), F32),
        compiler_params=pltpu.CompilerParams(
            dimension_semantics=("parallel",), vmem_limit_bytes=VMEM_LIMIT),
        name="mix_mlp",
    )(x2d, gri, gri, ys, bdr, bdi, fg, wg, bg, sg, wo, pmg, plg, w1, w2, pog)


def _fourier_channel_maps(w_f, seq):
    c = np.arange(FOURIER_HEAD_DIM)
    ang = 2.0 * np.pi * ((c[:, None] * c[None, :]) % FOURIER_HEAD_DIM) / FOURIER_HEAD_DIM
    scale = 1.0 / math.sqrt(seq * FOURIER_HEAD_DIM)
    hp = jax.lax.Precision.HIGHEST
    cw = jnp.einsum("cd,hde->hce", jnp.asarray(np.cos(ang) * scale, F32), w_f.astype(F32), precision=hp)
    sw = jnp.einsum("cd,hde->hce", jnp.asarray(np.sin(ang) * scale, F32), w_f.astype(F32), precision=hp)
    eye = jnp.eye(FOURIER_HEADS, dtype=F32)
    bd = lambda m: jnp.einsum("hce,hk->hcke", m, eye).reshape(D_FOURIER, D_FOURIER)
    return bd(cw).astype(BF16), bd(sw).astype(BF16)


def kernel(x, w_in, w_out, pre_mix_g, post_mix_g, pre_mlp_g, post_mlp_g, fourier_out_g, ssm_out_g,
           w_fourier, lam_re, lam_im, log_dt, b_re, b_im, c_re, c_im, d_skip, w_glu, b_glu,
           w_ff1, w_ff2):
    bsz, seq, _ = x.shape
    depth = w_in.shape[0]
    n = bsz * seq
    n1, n2 = FFT_N1, seq // FFT_N1
    nc = seq // CHUNK
    assert seq % (FFT_N1 * SUBLANES) == 0 and nc % SUBLANES == 0
    tm = min(512, n)
    w1_tab, m2_tab = _fft_tables(seq)
    row = lambda v: v.reshape(1, -1).astype(F32)

    x2d = x.reshape(n, D_MODEL).astype(F32)
    for i in range(depth):
        zf, zs = _in_proj(x2d, row(pre_mix_g[i]), w_in[i].astype(BF16), tm)

        u = zf.reshape(bsz, n1, n2 * D_FOURIER)
        y = _fft_stage1(u, w1_tab, min(8192, n2 * D_FOURIER))
        y = y.reshape(bsz, n1, 2 * n2, D_FOURIER)
        g = _fft_stage2(y, m2_tab, kb=4)
        gri = g.reshape(bsz, n1, 2, n2, D_FOURIER).transpose(2, 0, 3, 1, 4).reshape(2, n, D_FOURIER)

        m_intra, b_pair, c_pair, tab = _ssm_params(lam_re[i], lam_im[i], log_dt[i], b_re[i],
                                                   b_im[i], c_re[i], c_im[i], d_skip[i])
        ysr = _ssm_chunked(zs, m_intra, b_pair, c_pair, tab, bsz, seq)

        bdr, bdi = _fourier_channel_maps(w_fourier[i], seq)
        x2d = _mix_mlp(x2d, gri, ysr, bdr, bdi, row(fourier_out_g[i]), w_glu[i].astype(BF16),
                       row(b_glu[i]), row(ssm_out_g[i]), w_out[i].astype(BF16),
                       row(post_mix_g[i]), row(pre_mlp_g[i]), w_ff1[i].astype(BF16),
                       w_ff2[i].astype(BF16), row(post_mlp_g[i]), tm, ff_blk=1024)
    return x2d.reshape(bsz, seq, D_MODEL).astype(x.dtype)
```

```python
import functools
import math

import jax
import jax.numpy as jnp
import numpy as np
from jax.experimental import pallas as pl
from jax.experimental.pallas import tpu as pltpu

EPS = 1e-6
D_MODEL = 1024
D_FOURIER = 512
D_SSM = 512
FOURIER_HEADS = 8
FOURIER_HEAD_DIM = 64
SSM_GROUP = 16
SSM_GROUPS = 32
SSM_STATE = 64
D_FF = 4096

FFT_N1 = 64
CHUNK = 16
LANES = 128
SUBLANES = 8
VMEM_LIMIT = 56 * 1024 * 1024

BF16 = jnp.bfloat16
F32 = jnp.float32


def _rms(x, g):
    return x * jax.lax.rsqrt(jnp.mean(x * x, axis=-1, keepdims=True) + EPS) * g


def _dot(a, b):
    return jnp.dot(a, b, preferred_element_type=F32)


def _const_spec(shape):
    nd = len(shape)
    return pl.BlockSpec(shape, lambda *_: (0,) * nd, pipeline_mode=pl.Buffered(1))


def _in_proj_kernel(x_ref, g_ref, w_ref, zf_ref, zs_ref):
    h = _rms(x_ref[...], g_ref[...]).astype(BF16)
    z = _dot(h, w_ref[...])
    zf_ref[...] = z[:, :D_FOURIER].astype(BF16)
    zs_ref[...] = _rotate_rows(z[:, D_FOURIER:])


def _in_proj(x2d, g, w_bf16, tm):
    n = x2d.shape[0]
    return pl.pallas_call(
        _in_proj_kernel,
        grid=(n // tm,),
        in_specs=[pl.BlockSpec((tm, D_MODEL), lambda i: (i, 0)),
                  _const_spec((1, D_MODEL)),
                  _const_spec((D_MODEL, D_MODEL))],
        out_specs=[pl.BlockSpec((tm, D_FOURIER), lambda i: (i, 0)),
                   pl.BlockSpec((tm, D_SSM), lambda i: (i, 0))],
        out_shape=[jax.ShapeDtypeStruct((n, D_FOURIER), BF16),
                   jax.ShapeDtypeStruct((n, D_SSM), F32)],
        compiler_params=pltpu.CompilerParams(
            dimension_semantics=("parallel",), vmem_limit_bytes=VMEM_LIMIT),
        name="in_proj",
    )(x2d, g, w_bf16)


def _fft1_kernel(w_ref, u_ref, y_ref):
    y_ref[...] = _dot(w_ref[...], u_ref[...]).astype(BF16)


def _fft_stage1(u, w1, cols_blk):
    b, n1, cols = u.shape
    return pl.pallas_call(
        _fft1_kernel,
        grid=(b, cols // cols_blk),
        in_specs=[_const_spec((2 * n1, n1)),
                  pl.BlockSpec((None, n1, cols_blk), lambda i, j: (i, 0, j))],
        out_specs=pl.BlockSpec((None, 2 * n1, cols_blk), lambda i, j: (i, 0, j)),
        out_shape=jax.ShapeDtypeStruct((b, 2 * n1, cols), BF16),
        compiler_params=pltpu.CompilerParams(
            dimension_semantics=("parallel", "parallel"), vmem_limit_bytes=VMEM_LIMIT),
        name="fft_stage1",
    )(w1, u)


def _fft2_kernel(m_ref, y_ref, g_ref, *, kb):
    for j in range(kb):
        g_ref[j] = _dot(m_ref[j], y_ref[j]).astype(BF16)


def _fft_stage2(y, m2, kb):
    b, n1, rows, c = y.shape
    return pl.pallas_call(
        functools.partial(_fft2_kernel, kb=kb),
        grid=(b, n1 // kb),
        in_specs=[pl.BlockSpec((kb, rows, rows), lambda i, j: (j, 0, 0)),
                  pl.BlockSpec((None, kb, rows, c), lambda i, j: (i, j, 0, 0))],
        out_specs=pl.BlockSpec((None, kb, rows, c), lambda i, j: (i, j, 0, 0)),
        out_shape=jax.ShapeDtypeStruct((b, n1, rows, c), BF16),
        compiler_params=pltpu.CompilerParams(
            dimension_semantics=("parallel", "parallel"), vmem_limit_bytes=VMEM_LIMIT),
        name="fft_stage2",
    )(m2, y)


def _fft_tables(seq):
    n1, n2 = FFT_N1, seq // FFT_N1
    k1 = np.arange(n1)
    ang1 = 2.0 * np.pi * ((k1[:, None] * k1[None, :]) % n1) / n1
    w1 = np.stack([np.cos(ang1), -np.sin(ang1)], axis=1).reshape(2 * n1, n1)
    k2 = np.arange(n2)
    freq = k1[:, None, None] + n1 * k2[None, :, None]
    ang2 = 2.0 * np.pi * ((freq * k2[None, None, :]) % seq) / seq
    mr, mi = np.cos(ang2), -np.sin(ang2)
    m2 = np.concatenate([np.concatenate([mr, -mi], axis=2),
                         np.concatenate([mi, mr], axis=2)], axis=1)
    return jnp.asarray(w1, F32).astype(BF16), jnp.asarray(m2, F32).astype(BF16)


GROUPS_PER_BLOCK = LANES // SSM_GROUP
PAIRS_PER_BLOCK = GROUPS_PER_BLOCK // 2
RELAYOUT_CHUNKS = 4 * SUBLANES


def _rotate_rows(v):
    return jnp.concatenate(
        [pltpu.roll(v[:, c:c + LANES], 0, axis=1, stride=SSM_GROUP, stride_axis=0)
         for c in range(0, v.shape[1], LANES)], axis=1)


def _unrotate_rows(v):
    row = jax.lax.broadcasted_iota(jnp.int32, v.shape, 0)
    cols = []
    for c in range(0, v.shape[1], LANES):
        w = v[:, c:c + LANES]
        for bit in (1, 2, 4):
            w = jnp.where((row[:, :LANES] & bit) != 0,
                          pltpu.roll(w, LANES - SSM_GROUP * bit, axis=1), w)
        cols.append(w)
    return jnp.concatenate(cols, axis=1)


def _ssm_kernel(z_ref, mi_ref, bs_ref, cs_ref, tab_ref, y_ref,
                u2_ref, s_ref, car_ref, yg_ref, *, nc):
    gpb, npair = GROUPS_PER_BLOCK, PAIRS_PER_BLOCK
    t16 = RELAYOUT_CHUNKS
    rows_it = t16 * CHUNK
    lane_blk = jax.lax.broadcasted_iota(jnp.int32, (t16, LANES), 1) // SSM_GROUP
    masks = [lane_blk == m for m in range(gpb)]

    def relayout_in(i, _):
        r0 = pl.multiple_of(i * rows_it, rows_it)
        ut = [z_ref[pl.ds(r0 + t, t16, stride=CHUNK), :] for t in range(CHUNK)]
        c0 = pl.multiple_of(i * t16, t16)
        for g in range(gpb):
            for j in range(CHUNK // gpb):
                acc = ut[gpb * j]
                for tt in range(1, gpb):
                    acc = jnp.where(masks[(g + tt) % gpb], ut[gpb * j + tt], acc)
                if g:
                    acc = pltpu.roll(acc, LANES - SSM_GROUP * g, axis=1)
                u2_ref[g, pl.ds(c0, t16), j * LANES:(j + 1) * LANES] = acc.astype(BF16)
        return 0

    jax.lax.fori_loop(0, nc // t16, relayout_in, 0)

    for q in range(npair):
        up = jnp.concatenate([u2_ref[2 * q], u2_ref[2 * q + 1]], axis=1)
        s_ref[q] = _dot(up, bs_ref[q])

    row = jax.lax.broadcasted_iota(jnp.int32, (SUBLANES, LANES), 0)
    nblk = nc // SUBLANES

    def scan_block(tab, sr, si, cr, ci, down):
        keep = (lambda sh: row >= sh) if down else (lambda sh: row <= SUBLANES - 1 - sh)
        amt = (lambda sh: sh) if down else (lambda sh: SUBLANES - sh)
        for k, sh in enumerate((1, 2, 4)):
            tr = jnp.where(keep(sh), pltpu.roll(sr, amt(sh), axis=0), 0.0)
            ti = jnp.where(keep(sh), pltpu.roll(si, amt(sh), axis=0), 0.0)
            ar, ai = tab[2 * k], tab[2 * k + 1]
            sr, si = sr + (ar * tr - ai * ti), si + (ar * ti + ai * tr)
        fr = sr + (tab[6] * cr - tab[7] * ci)
        fi = si + (tab[6] * ci + tab[7] * cr)
        outr = jnp.where(keep(1), pltpu.roll(fr, amt(1), axis=0), cr)
        outi = jnp.where(keep(1), pltpu.roll(fi, amt(1), axis=0), ci)
        edge = SUBLANES - 1 if down else 0
        ncr = jnp.broadcast_to(fr[edge:edge + 1, :], (SUBLANES, LANES))
        nci = jnp.broadcast_to(fi[edge:edge + 1, :], (SUBLANES, LANES))
        return outr, outi, ncr, nci

    def scan_body(i, carry):
        new = []
        for q in range(npair):
            for d in range(2):
                blk = i if d == 0 else nblk - 1 - i
                r0 = pl.multiple_of(blk * SUBLANES, SUBLANES)
                lo = 2 * d * LANES
                tab = [tab_ref[q, d, k] for k in range(8)]
                cr, ci = carry[4 * q + 2 * d], carry[4 * q + 2 * d + 1]
                outr, outi, cr, ci = scan_block(
                    tab, s_ref[q, pl.ds(r0, SUBLANES), lo:lo + LANES],
                    s_ref[q, pl.ds(r0, SUBLANES), lo + LANES:lo + 2 * LANES], cr, ci, d == 0)
                car_ref[q, pl.ds(r0, SUBLANES), lo:lo + LANES] = outr
                car_ref[q, pl.ds(r0, SUBLANES), lo + LANES:lo + 2 * LANES] = outi
                new += [cr, ci]
        return tuple(new)

    zero = jnp.zeros((SUBLANES, LANES), F32)
    jax.lax.fori_loop(0, nblk, scan_body, (zero,) * (4 * npair))

    kk = CHUNK * SSM_GROUP
    for q in range(npair):
        yi = _dot(car_ref[q].astype(BF16), cs_ref[q])
        for e in range(2):
            g = 2 * q + e
            yg_ref[g] = _dot(u2_ref[g], mi_ref[g]) + yi[:, e * kk:(e + 1) * kk]

    def relayout_out(i, _):
        c0 = pl.multiple_of(i * t16, t16)
        r0 = pl.multiple_of(i * rows_it, rows_it)
        yrot = []
        for g in range(gpb):
            halves = []
            for j in range(CHUNK // gpb):
                v = yg_ref[g, pl.ds(c0, t16), j * LANES:(j + 1) * LANES]
                halves.append(pltpu.roll(v, SSM_GROUP * g, axis=1) if g else v)
            yrot.append(halves)
        for t in range(CHUNK):
            j, tt = divmod(t, gpb)
            acc = yrot[0][j]
            for g in range(1, gpb):
                acc = jnp.where(masks[(g + tt) % gpb], yrot[g][j], acc)
            y_ref[pl.ds(r0 + t, t16, stride=CHUNK), :] = acc
        return 0

    jax.lax.fori_loop(0, nc // t16, relayout_out, 0)


def _ssm_chunked(zs, m_intra, b_pair, c_pair, tab, bsz, seq):
    nc = seq // CHUNK
    gpb, npair = GROUPS_PER_BLOCK, PAIRS_PER_BLOCK
    k = CHUNK * SSM_GROUP
    return pl.pallas_call(
        functools.partial(_ssm_kernel, nc=nc),
        grid=(D_SSM // LANES, bsz),
        in_specs=[pl.BlockSpec((seq, LANES), lambda j, b: (b, j)),
                  pl.BlockSpec((gpb, k, k), lambda j, b: (j, 0, 0)),
                  pl.BlockSpec((npair, 2 * k, 2 * k), lambda j, b: (j, 0, 0)),
                  pl.BlockSpec((npair, 2 * k, 2 * k), lambda j, b: (j, 0, 0)),
                  pl.BlockSpec((npair, 2, 8, SUBLANES, LANES), lambda j, b: (j, 0, 0, 0, 0))],
        out_specs=pl.BlockSpec((seq, LANES), lambda j, b: (b, j)),
        out_shape=jax.ShapeDtypeStruct((bsz * seq, D_SSM), F32),
        scratch_shapes=[                        pltpu.VMEM((gpb, nc, k), BF16),
                        pltpu.VMEM((npair, nc, 2 * k), F32),
                        pltpu.VMEM((npair, nc, 2 * k), F32),
                        pltpu.VMEM((gpb, nc, k), F32)],
        compiler_params=pltpu.CompilerParams(
            dimension_semantics=("parallel", "parallel"), vmem_limit_bytes=VMEM_LIMIT),
        name="ssm_chunk",
    )(zs, m_intra, b_pair, c_pair, tab)


def _ssm_params(lam_re, lam_im, log_dt, b_re, b_im, c_re, c_im, d_skip):
    t = CHUNK
    hp = jax.lax.Precision.HIGHEST
    dt = jnp.exp(log_dt.astype(F32))[..., None]
    are = lam_re.astype(F32) * dt
    aim = lam_im.astype(F32) * dt

    def power(k):
        kk = jnp.asarray(k, F32).reshape((-1, 1, 1, 1))
        mag = jnp.exp(are[None] * kk)
        return mag * jnp.cos(aim[None] * kk), mag * jnp.sin(aim[None] * kk)

    lbr, lbi = power([1.0])
    lbr, lbi = lbr[0], lbi[0]
    den = lam_re * lam_re + lam_im * lam_im
    cr = ((lbr - 1.0) * lam_re + lbi * lam_im) / den
    ci = (lbi * lam_re - (lbr - 1.0) * lam_im) / den
    bbr = cr[..., None] * b_re - ci[..., None] * b_im
    bbi = cr[..., None] * b_im + ci[..., None] * b_re

    pr, pi = power(np.arange(t + 1))
    cpr = c_re[None] * pr[:, :, :, None, :] - c_im[None] * pi[:, :, :, None, :]
    cpi = c_re[None] * pi[:, :, :, None, :] + c_im[None] * pr[:, :, :, None, :]
    kern = (jnp.einsum("ldgop,dgpi->ldgoi", cpr, bbr, precision=hp)
            - jnp.einsum("ldgop,dgpi->ldgoi", cpi, bbi, precision=hp))
    eye = jnp.eye(SSM_GROUP, dtype=F32)
    k0 = kern[0, 0] + kern[0, 1] + d_skip.astype(F32)[:, :, None] * eye[None]
    lags = jnp.concatenate([kern[1:t, 1][::-1], k0[None], kern[1:t, 0]], axis=0)
    tt = np.arange(t)
    idx = tt[None, :] - tt[:, None] + (t - 1)
    m_intra = lags[idx]
    m_intra = m_intra.transpose(2, 0, 4, 1, 3).reshape(SSM_GROUPS, t * SSM_GROUP, t * SSM_GROUP)

    pw_r = jnp.stack([pr[:t, 0][::-1], pr[:t, 1]], axis=1)
    pw_i = jnp.stack([pi[:t, 0][::-1], pi[:t, 1]], axis=1)
    sbr = pw_r[..., None] * bbr[None] - pw_i[..., None] * bbi[None]
    sbi = pw_r[..., None] * bbi[None] + pw_i[..., None] * bbr[None]
    b_st = jnp.stack([sbr, sbi], axis=2)
    b_st = b_st.transpose(3, 0, 5, 1, 2, 4).reshape(SSM_GROUPS, t * SSM_GROUP, 4 * SSM_STATE)

    qw_r = jnp.stack([pr[1:, 0], pr[1:, 1][::-1]], axis=1)
    qw_i = jnp.stack([pi[1:, 0], pi[1:, 1][::-1]], axis=1)
    cqr = c_re[None] * qw_r[:, :, :, None, :] - c_im[None] * qw_i[:, :, :, None, :]
    cqi = c_re[None] * qw_i[:, :, :, None, :] + c_im[None] * qw_r[:, :, :, None, :]
    c_st = jnp.stack([cqr, -cqi], axis=2)
    c_st = c_st.transpose(3, 1, 2, 5, 0, 4).reshape(SSM_GROUPS, 4 * SSM_STATE, t * SSM_GROUP)

    ar, ai = power(t * np.arange(1, SUBLANES + 1))

    nq = SSM_GROUPS // 2
    ar = ar.reshape(SUBLANES, 2, nq, LANES)
    ai = ai.reshape(SUBLANES, 2, nq, LANES)

    def rows8(x):
        return jnp.broadcast_to(x[:, :, None, :], x.shape[:2] + (SUBLANES, x.shape[-1]))

    def edge_rows(x):
        x = x.transpose(1, 2, 0, 3)
        return jnp.stack([x[0], x[1][:, ::-1]], axis=0)

    slots = []
    for k in (0, 1, 3):
        slots += [rows8(ar[k]), rows8(ai[k])]
    slots += [edge_rows(ar), edge_rows(ai)]
    tab = jnp.stack(slots, axis=2).transpose(1, 0, 2, 3, 4)

    eye2 = jnp.eye(2, dtype=F32)
    kk = t * SSM_GROUP
    b_pair = jnp.einsum("qgrbp,gh->qgrbhp", b_st.reshape(nq, 2, kk, 4, SSM_STATE), eye2)
    b_pair = b_pair.reshape(nq, 2 * kk, 8 * SSM_STATE)
    c_pair = jnp.einsum("qgbpc,gh->qbhpgc", c_st.reshape(nq, 2, 4, SSM_STATE, kk), eye2)
    c_pair = c_pair.reshape(nq, 8 * SSM_STATE, 2 * kk)
    return m_intra.astype(BF16), b_pair.astype(BF16), c_pair.astype(BF16), tab.astype(F32)


def _mix_mlp_kernel(x_ref, gr_ref, gi_ref, ys_ref, bdr_ref, bdi_ref, fg_ref, wg_ref, bg_ref,
                    sg_ref, wo_ref, pmg_ref, plg_ref, w1_ref, w2_ref, pog_ref, o_ref, *, ff_blk):
    yf = _dot(gr_ref[...], bdr_ref[...]) + _dot(gi_ref[...], bdi_ref[...])
    yf = _rms(yf, fg_ref[...])
    y = _unrotate_rows(ys_ref[...])
    gl = jax.nn.gelu(y, approximate=True)
    gate = _dot(gl.astype(BF16), wg_ref[...]) + bg_ref[...]
    ys = _rms(gl * jax.nn.sigmoid(gate), sg_ref[...])
    cat = jnp.concatenate([yf.astype(BF16), ys.astype(BF16)], axis=-1)
    x1 = x_ref[...] + _rms(_dot(cat, wo_ref[...]), pmg_ref[...])
    h = _rms(x1, plg_ref[...]).astype(BF16)
    acc = jnp.zeros(x1.shape, F32)
    for j in range(D_FF // ff_blk):
        a = _dot(h, w1_ref[:, j * ff_blk:(j + 1) * ff_blk])
        a = jnp.square(jnp.maximum(a, 0.0)).astype(BF16)
        acc = acc + _dot(a, w2_ref[j * ff_blk:(j + 1) * ff_blk, :])
    o_ref[...] = x1 + _rms(acc, pog_ref[...])


def _mix_mlp(x2d, gri, ys, bdr, bdi, fg, wg, bg, sg, wo, pmg, plg, w1, w2, pog, tm, ff_blk):
    n = x2d.shape[0]
    row = lambda i: (i, 0)
    return pl.pallas_call(
        functools.partial(_mix_mlp_kernel, ff_blk=ff_blk),
        grid=(n // tm,),
        in_specs=[pl.BlockSpec((tm, D_MODEL), row),
                  pl.BlockSpec((None, tm, D_FOURIER), lambda i: (0, i, 0)),
                  pl.BlockSpec((None, tm, D_FOURIER), lambda i: (1, i, 0)),
                  pl.BlockSpec((tm, D_SSM), row),
                  _const_spec((D_FOURIER, D_FOURIER)), _const_spec((D_FOURIER, D_FOURIER)),
                  _const_spec((1, D_FOURIER)),
                  _const_spec((D_SSM, D_SSM)), _const_spec((1, D_SSM)), _const_spec((1, D_SSM)),
                  _const_spec((D_MODEL, D_MODEL)), _const_spec((1, D_MODEL)),
                  _const_spec((1, D_MODEL)),
                  _const_spec((D_MODEL, D_FF)), _const_spec((D_FF, D_MODEL)),
                  _const_spec((1, D_MODEL))],
        out_specs=pl.BlockSpec((tm, D_MODEL), row),
        out_shape=jax.ShapeDtypeStruct((n, D_MODEL), F32),
        compiler_params=pltpu.CompilerParams(
            dimension_semantics=("parallel",), vmem_limit_bytes=VMEM_LIMIT),
        name="mix_mlp",
    )(x2d, gri, gri, ys, bdr, bdi, fg, wg, bg, sg, wo, pmg, plg, w1, w2, pog)


def _fourier_channel_maps(w_f, seq):
    c = np.arange(FOURIER_HEAD_DIM)
    ang = 2.0 * np.pi * ((c[:, None] * c[None, :]) % FOURIER_HEAD_DIM) / FOURIER_HEAD_DIM
    scale = 1.0 / math.sqrt(seq * FOURIER_HEAD_DIM)
    hp = jax.lax.Precision.HIGHEST
    cw = jnp.einsum("cd,hde->hce", jnp.asarray(np.cos(ang) * scale, F32), w_f.astype(F32), precision=hp)
    sw = jnp.einsum("cd,hde->hce", jnp.asarray(np.sin(ang) * scale, F32), w_f.astype(F32), precision=hp)
    eye = jnp.eye(FOURIER_HEADS, dtype=F32)
    bd = lambda m: jnp.einsum("hce,hk->hcke", m, eye).reshape(D_FOURIER, D_FOURIER)
    return bd(cw).astype(BF16), bd(sw).astype(BF16)


def kernel(x, w_in, w_out, pre_mix_g, post_mix_g, pre_mlp_g, post_mlp_g, fourier_out_g, ssm_out_g,
           w_fourier, lam_re, lam_im, log_dt, b_re, b_im, c_re, c_im, d_skip, w_glu, b_glu,
           w_ff1, w_ff2):
    bsz, seq, _ = x.shape
    depth = w_in.shape[0]
    n = bsz * seq
    n1, n2 = FFT_N1, seq // FFT_N1
    nc = seq // CHUNK
    assert seq % (FFT_N1 * SUBLANES) == 0 and nc % SUBLANES == 0
    tm = min(512, n)
    w1_tab, m2_tab = _fft_tables(seq)
    row = lambda v: v.reshape(1, -1).astype(F32)

    x2d = x.reshape(n, D_MODEL).astype(F32)
    for i in range(depth):
        zf, zs = _in_proj(x2d, row(pre_mix_g[i]), w_in[i].astype(BF16), tm)

        u = zf.reshape(bsz, n1, n2 * D_FOURIER)
        y = _fft_stage1(u, w1_tab, min(8192, n2 * D_FOURIER))
        y = y.reshape(bsz, n1, 2 * n2, D_FOURIER)
        g = _fft_stage2(y, m2_tab, kb=4)
        gri = g.reshape(bsz, n1, 2, n2, D_FOURIER).transpose(2, 0, 3, 1, 4).reshape(2, n, D_FOURIER)

        m_intra, b_pair, c_pair, tab = _ssm_params(lam_re[i], lam_im[i], log_dt[i], b_re[i],
                                                   b_im[i], c_re[i], c_im[i], d_skip[i])
        ysr = _ssm_chunked(zs, m_intra, b_pair, c_pair, tab, bsz, seq)

        bdr, bdi = _fourier_channel_maps(w_fourier[i], seq)
        x2d = _mix_mlp(x2d, gri, ysr, bdr, bdi, row(fourier_out_g[i]), w_glu[i].astype(BF16),
                       row(b_glu[i]), row(ssm_out_g[i]), w_out[i].astype(BF16),
                       row(post_mix_g[i]), row(pre_mlp_g[i]), w_ff1[i].astype(BF16),
                       w_ff2[i].astype(BF16), row(post_mlp_g[i]), tm, ff_blk=1024)
    return x2d.reshape(bsz, seq, D_MODEL).astype(x.dtype)
```

```python
import functools
import math

import jax
import jax.numpy as jnp
import numpy as np
from jax.experimental import pallas as pl
from jax.experimental.pallas import tpu as pltpu

EPS = 1e-6
D_MODEL = 1024
D_FOURIER = 512
D_SSM = 512
FOURIER_HEADS = 8
FOURIER_HEAD_DIM = 64
SSM_GROUP = 16
SSM_GROUPS = 32
SSM_STATE = 64
D_FF = 4096

FFT_N1 = 64
CHUNK = 16
LANES = 128
SUBLANES = 8
VMEM_LIMIT = 56 * 1024 * 1024

BF16 = jnp.bfloat16
F32 = jnp.float32


def _rms(x, g):
    return x * jax.lax.rsqrt(jnp.mean(x * x, axis=-1, keepdims=True) + EPS) * g


def _dot(a, b):
    return jnp.dot(a, b, preferred_element_type=F32)


def _const_spec(shape):
    nd = len(shape)
    return pl.BlockSpec(shape, lambda *_: (0,) * nd, pipeline_mode=pl.Buffered(1))


N2_BLK = SUBLANES


def _in_proj_kernel(x_ref, g_ref, w_ref, wk_ref, y_ref, zs_ref):
    n1 = x_ref.shape[0]
    x = x_ref[...].reshape(n1 * N2_BLK, D_MODEL)
    h = _rms(x, g_ref[...]).astype(BF16)
    z = _dot(h, w_ref[...])
    y = _dot(wk_ref[...], z[:, :D_FOURIER].astype(BF16))
    y_ref[...] = y.reshape(n1, 2, N2_BLK, D_FOURIER)
    zs_ref[...] = _rotate_rows(z[:, D_FOURIER:]).reshape(n1, N2_BLK, D_SSM)


def _in_proj(x4, g, w_bf16, wk):
    bsz, n1, n2, _ = x4.shape
    rows = n1 * N2_BLK
    return pl.pallas_call(
        _in_proj_kernel,
        grid=(bsz, n2 // N2_BLK),
        in_specs=[pl.BlockSpec((None, n1, N2_BLK, D_MODEL), lambda b, j: (b, 0, j, 0)),
                  _const_spec((1, D_MODEL)),
                  _const_spec((D_MODEL, D_MODEL)),
                  _const_spec((2 * rows, rows))],
        out_specs=[pl.BlockSpec((None, n1, 2, N2_BLK, D_FOURIER), lambda b, j: (b, 0, 0, j, 0)),
                   pl.BlockSpec((None, n1, N2_BLK, D_SSM), lambda b, j: (b, 0, j, 0))],
        out_shape=[jax.ShapeDtypeStruct((bsz, n1, 2, n2, D_FOURIER), F32),
                   jax.ShapeDtypeStruct((bsz, n1, n2, D_SSM), F32)],
        compiler_params=pltpu.CompilerParams(
            dimension_semantics=("parallel", "parallel"), vmem_limit_bytes=VMEM_LIMIT),
        name="in_proj",
    )(x4, g, w_bf16, wk)


def _fft2_kernel(m_ref, y_ref, g_ref, z_ref, *, kb, n2, pitch):
    nslab = D_FOURIER // LANES
    for j in range(kb):
        z = _dot(m_ref[j], y_ref[j].astype(BF16))
        for ri in range(2):
            for s in range(nslab):
                z_ref[ri, s, j * pitch:j * pitch + n2, :] = (
                    z[ri * n2:(ri + 1) * n2, s * LANES:(s + 1) * LANES])

    def regroup(k2, _):
        for ri in range(2):
            for s in range(nslab):
                g_ref[ri, k2, :, s * LANES:(s + 1) * LANES] = (
                    z_ref[ri, s, pl.ds(k2, kb, stride=pitch), :])
        return 0

    jax.lax.fori_loop(0, n2, regroup, 0)


def _fft_stage2(y, m2, kb):
    b, n1, rows, c = y.shape
    n2 = rows // 2
    pitch = n2 + SUBLANES
    return pl.pallas_call(
        functools.partial(_fft2_kernel, kb=kb, n2=n2, pitch=pitch),
        grid=(b, n1 // kb),
        in_specs=[pl.BlockSpec((kb, rows, rows), lambda i, j: (j, 0, 0)),
                  pl.BlockSpec((None, kb, rows, c), lambda i, j: (i, j, 0, 0))],
        out_specs=pl.BlockSpec((2, None, n2, kb, c), lambda i, j: (0, i, 0, j, 0)),
        out_shape=jax.ShapeDtypeStruct((2, b, n2, n1, c), F32),
        scratch_shapes=[pltpu.VMEM((2, c // LANES, kb * pitch, LANES), F32)],
        compiler_params=pltpu.CompilerParams(
            dimension_semantics=("parallel", "parallel"), vmem_limit_bytes=VMEM_LIMIT),
        name="fft_stage2",
    )(m2, y)


def _fft_tables(seq):
    n1, n2 = FFT_N1, seq // FFT_N1
    k1 = np.arange(n1)
    ang1 = 2.0 * np.pi * ((k1[:, None] * k1[None, :]) % n1) / n1
    w1 = np.stack([np.cos(ang1), -np.sin(ang1)], axis=1).reshape(2 * n1, n1)
    wk = np.einsum("kn,ab->kanb", w1, np.eye(N2_BLK)).reshape(2 * n1 * N2_BLK, n1 * N2_BLK)
    k2 = np.arange(n2)
    freq = k1[:, None, None] + n1 * k2[None, :, None]
    ang2 = 2.0 * np.pi * ((freq * k2[None, None, :]) % seq) / seq
    mr, mi = np.cos(ang2), -np.sin(ang2)
    m2 = np.concatenate([np.concatenate([mr, -mi], axis=2),
                         np.concatenate([mi, mr], axis=2)], axis=1)
    return jnp.asarray(wk, F32).astype(BF16), jnp.asarray(m2, F32).astype(BF16)


GROUPS_PER_BLOCK = LANES // SSM_GROUP
PAIRS_PER_BLOCK = GROUPS_PER_BLOCK // 2
RELAYOUT_CHUNKS = 4 * SUBLANES


def _rotate_rows(v):
    return jnp.concatenate(
        [pltpu.roll(v[:, c:c + LANES], 0, axis=1, stride=SSM_GROUP, stride_axis=0)
         for c in range(0, v.shape[1], LANES)], axis=1)


def _unrotate_rows(v):
    row = jax.lax.broadcasted_iota(jnp.int32, v.shape, 0)
    cols = []
    for c in range(0, v.shape[1], LANES):
        w = v[:, c:c + LANES]
        for bit in (1, 2, 4):
            w = jnp.where((row[:, :LANES] & bit) != 0,
                          pltpu.roll(w, LANES - SSM_GROUP * bit, axis=1), w)
        cols.append(w)
    return jnp.concatenate(cols, axis=1)


def _ssm_kernel(z_ref, mi_ref, bs_ref, cs_ref, tab_ref, y_ref,
                u2_ref, s_ref, car_ref, yg_ref, *, nc):
    gpb, npair = GROUPS_PER_BLOCK, PAIRS_PER_BLOCK
    t16 = RELAYOUT_CHUNKS
    rows_it = t16 * CHUNK
    lane_blk = jax.lax.broadcasted_iota(jnp.int32, (t16, LANES), 1) // SSM_GROUP
    masks = [lane_blk == m for m in range(gpb)]

    def relayout_in(i, _):
        r0 = pl.multiple_of(i * rows_it, rows_it)
        ut = [z_ref[pl.ds(r0 + t, t16, stride=CHUNK), :] for t in range(CHUNK)]
        c0 = pl.multiple_of(i * t16, t16)
        for g in range(gpb):
            for j in range(CHUNK // gpb):
                acc = ut[gpb * j]
                for tt in range(1, gpb):
                    acc = jnp.where(masks[(g + tt) % gpb], ut[gpb * j + tt], acc)
                if g:
                    acc = pltpu.roll(acc, LANES - SSM_GROUP * g, axis=1)
                u2_ref[g, pl.ds(c0, t16), j * LANES:(j + 1) * LANES] = acc.astype(BF16)
        return 0

    jax.lax.fori_loop(0, nc // t16, relayout_in, 0)

    for q in range(npair):
        up = jnp.concatenate([u2_ref[2 * q], u2_ref[2 * q + 1]], axis=1)
        s_ref[q] = _dot(up, bs_ref[q])

    row = jax.lax.broadcasted_iota(jnp.int32, (SUBLANES, LANES), 0)
    nblk = nc // SUBLANES

    def scan_block(tab, sr, si, cr, ci, down):
        keep = (lambda sh: row >= sh) if down else (lambda sh: row <= SUBLANES - 1 - sh)
        amt = (lambda sh: sh) if down else (lambda sh: SUBLANES - sh)
        for k, sh in enumerate((1, 2, 4)):
            tr = jnp.where(keep(sh), pltpu.roll(sr, amt(sh), axis=0), 0.0)
            ti = jnp.where(keep(sh), pltpu.roll(si, amt(sh), axis=0), 0.0)
            ar, ai = tab[2 * k], tab[2 * k + 1]
            sr, si = sr + (ar * tr - ai * ti), si + (ar * ti + ai * tr)
        fr = sr + (tab[6] * cr - tab[7] * ci)
        fi = si + (tab[6] * ci + tab[7] * cr)
        outr = jnp.where(keep(1), pltpu.roll(fr, amt(1), axis=0), cr)
        outi = jnp.where(keep(1), pltpu.roll(fi, amt(1), axis=0), ci)
        edge = SUBLANES - 1 if down else 0
        ncr = jnp.broadcast_to(fr[edge:edge + 1, :], (SUBLANES, LANES))
        nci = jnp.broadcast_to(fi[edge:edge + 1, :], (SUBLANES, LANES))
        return outr, outi, ncr, nci

    def scan_body(i, carry):
        new = []
        for q in range(npair):
            for d in range(2):
                blk = i if d == 0 else nblk - 1 - i
                r0 = pl.multiple_of(blk * SUBLANES, SUBLANES)
                lo = 2 * d * LANES
                tab = [tab_ref[q, d, k] for k in range(8)]
                cr, ci = carry[4 * q + 2 * d], carry[4 * q + 2 * d + 1]
                outr, outi, cr, ci = scan_block(
                    tab, s_ref[q, pl.ds(r0, SUBLANES), lo:lo + LANES],
                    s_ref[q, pl.ds(r0, SUBLANES), lo + LANES:lo + 2 * LANES], cr, ci, d == 0)
                car_ref[q, pl.ds(r0, SUBLANES), lo:lo + LANES] = outr
                car_ref[q, pl.ds(r0, SUBLANES), lo + LANES:lo + 2 * LANES] = outi
                new += [cr, ci]
        return tuple(new)

    zero = jnp.zeros((SUBLANES, LANES), F32)
    jax.lax.fori_loop(0, nblk, scan_body, (zero,) * (4 * npair))

    kk = CHUNK * SSM_GROUP
    for q in range(npair):
        yi = _dot(car_ref[q].astype(BF16), cs_ref[q])
        for e in range(2):
            g = 2 * q + e
            yg_ref[g] = _dot(u2_ref[g], mi_ref[g]) + yi[:, e * kk:(e + 1) * kk]

    def relayout_out(i, _):
        c0 = pl.multiple_of(i * t16, t16)
        r0 = pl.multiple_of(i * rows_it, rows_it)
        yrot = []
        for g in range(gpb):
            halves = []
            for j in range(CHUNK // gpb):
                v = yg_ref[g, pl.ds(c0, t16), j * LANES:(j + 1) * LANES]
                halves.append(pltpu.roll(v, SSM_GROUP * g, axis=1) if g else v)
            yrot.append(halves)
        for t in range(CHUNK):
            j, tt = divmod(t, gpb)
            acc = yrot[0][j]
            for g in range(1, gpb):
                acc = jnp.where(masks[(g + tt) % gpb], yrot[g][j], acc)
            y_ref[pl.ds(r0 + t, t16, stride=CHUNK), :] = acc
        return 0

    jax.lax.fori_loop(0, nc // t16, relayout_out, 0)


def _ssm_chunked(zs, m_intra, b_pair, c_pair, tab, bsz, seq):
    nc = seq // CHUNK
    gpb, npair = GROUPS_PER_BLOCK, PAIRS_PER_BLOCK
    k = CHUNK * SSM_GROUP
    return pl.pallas_call(
        functools.partial(_ssm_kernel, nc=nc),
        grid=(D_SSM // LANES, bsz),
        in_specs=[pl.BlockSpec((seq, LANES), lambda j, b: (b, j)),
                  pl.BlockSpec((gpb, k, k), lambda j, b: (j, 0, 0)),
                  pl.BlockSpec((npair, 2 * k, 2 * k), lambda j, b: (j, 0, 0)),
                  pl.BlockSpec((npair, 2 * k, 2 * k), lambda j, b: (j, 0, 0)),
                  pl.BlockSpec((npair, 2, 8, SUBLANES, LANES), lambda j, b: (j, 0, 0, 0, 0))],
        out_specs=pl.BlockSpec((seq, LANES), lambda j, b: (b, j)),
        out_shape=jax.ShapeDtypeStruct((bsz * seq, D_SSM), F32),
        scratch_shapes=[                        pltpu.VMEM((gpb, nc, k), BF16),
                        pltpu.VMEM((npair, nc, 2 * k), F32),
                        pltpu.VMEM((npair, nc, 2 * k), F32),
                        pltpu.VMEM((gpb, nc, k), F32)],
        compiler_params=pltpu.CompilerParams(
            dimension_semantics=("parallel", "parallel"), vmem_limit_bytes=VMEM_LIMIT),
        name="ssm_chunk",
    )(zs, m_intra, b_pair, c_pair, tab)


def _ssm_operators(lam_re, lam_im, log_dt, b_re, b_im, c_re, c_im, d_skip):
    t, h, p, g = CHUNK, SSM_GROUP, SSM_STATE, SSM_GROUPS
    kk, nq = t * h, g // 2
    hp = jax.lax.Precision.HIGHEST
    dt = jnp.exp(log_dt.astype(F32))[..., None]
    are = lam_re.astype(F32) * dt
    aim = lam_im.astype(F32) * dt

    def power(expo, a_re, a_im):
        mag = jnp.exp(a_re * expo)
        return mag * jnp.cos(a_im * expo), mag * jnp.sin(a_im * expo)

    lbr, lbi = power(1.0, are, aim)
    den = lam_re * lam_re + lam_im * lam_im
    cr = ((lbr - 1.0) * lam_re + lbi * lam_im) / den
    ci = (lbi * lam_re - (lbr - 1.0) * lam_im) / den
    bbr = cr[..., None] * b_re - ci[..., None] * b_im
    bbi = cr[..., None] * b_im + ci[..., None] * b_re
    btr, bti = jnp.swapaxes(bbr, -1, -2), jnp.swapaxes(bbi, -1, -2)
    ctr = jnp.tile(jnp.swapaxes(c_re.astype(F32), -1, -2), (1, 1, 1, t))
    cti = jnp.tile(jnp.swapaxes(c_im.astype(F32), -1, -2), (1, 1, 1, t))
    blk = np.arange(kk) // h
    dirs = lambda f, b: jnp.asarray(np.stack([f, b]), F32)

    def c_times_power(expo):
        pr, pi = power(expo[:, None, None, :], are[..., None], aim[..., None])
        return ctr * pr - cti * pi, ctr * pi + cti * pr

    clr, cli = c_times_power(dirs(blk, (t - blk) % t))
    gen = (jnp.einsum("dghp,dgpc->dghc", btr, clr, precision=hp)
           - jnp.einsum("dghp,dgpc->dghc", bti, cli, precision=hp))
    diag = gen[1, :, :, :h] + d_skip.astype(F32)[:, None, :] * jnp.eye(h, dtype=F32)[None]
    kf = gen[0] + jnp.pad(diag, ((0, 0), (0, 0), (0, kk - h)))
    kb = gen[1]
    lane_t = jnp.asarray(blk)
    rows = [jnp.where(lane_t >= s, jnp.roll(kf, h * s, axis=-1), jnp.roll(kb, h * s, axis=-1))
            for s in range(t)]
    m_intra = jnp.stack(rows, axis=1).reshape(g, kk, kk)

    pr, pi = power(dirs(t - 1 - blk, blk)[:, None, :, None], are[:, :, None, :], aim[:, :, None, :])
    btr_t, bti_t = jnp.tile(btr, (1, 1, t, 1)), jnp.tile(bti, (1, 1, t, 1))
    sb = jnp.stack([btr_t * pr - bti_t * pi, btr_t * pi + bti_t * pr], axis=1)
    sb = sb.reshape(2, 2, nq, 2, kk, p)
    zero = jnp.zeros_like(sb[:, :, :, 0])
    sb = jnp.stack([jnp.concatenate([sb[:, :, :, 0], zero], -1),
                    jnp.concatenate([zero, sb[:, :, :, 1]], -1)], axis=3)
    b_pair = sb.transpose(2, 3, 4, 0, 1, 5).reshape(nq, 2 * kk, 8 * p)

    cqr, cqi = c_times_power(dirs(blk + 1, t - blk))
    cq = jnp.stack([cqr, -cqi], axis=1).reshape(2, 2, nq, 2, p, kk)
    zero = jnp.zeros_like(cq[:, :, :, 0])
    cq = jnp.stack([jnp.concatenate([cq[:, :, :, 0], zero], -1),
                    jnp.concatenate([zero, cq[:, :, :, 1]], -1)], axis=3)
    c_pair = cq.transpose(2, 0, 1, 3, 4, 5).reshape(nq, 8 * p, 2 * kk)

    r8 = np.arange(SUBLANES)
    expo = np.stack([np.stack([np.full(SUBLANES, k), np.full(SUBLANES, k)]) for k in (1, 2, 4)]
                    + [np.stack([r8 + 1, SUBLANES - r8])])
    expo = jnp.asarray(t * expo, F32)[:, :, None, :, None]
    ar, ai = power(expo, are.reshape(2, nq, 1, 2 * p)[None], aim.reshape(2, nq, 1, 2 * p)[None])
    tab = jnp.stack([ar, ai], axis=1).reshape(8, 2, nq, SUBLANES, 2 * p)
    tab = tab.transpose(2, 1, 0, 3, 4)
    return m_intra.astype(BF16), b_pair.astype(BF16), c_pair.astype(BF16), tab


def _mix_mlp_kernel(x_ref, gr_ref, gi_ref, ys_ref, bdr_ref, bdi_ref, fg_ref, wg_ref, bg_ref,
                    sg_ref, wo_ref, pmg_ref, plg_ref, w1_ref, w2_ref, pog_ref, o_ref, *, ff_blk):
    yf = (_dot(gr_ref[...].astype(BF16), bdr_ref[...])
          + _dot(gi_ref[...].astype(BF16), bdi_ref[...]))
    yf = _rms(yf, fg_ref[...])
    y = _unrotate_rows(ys_ref[...])
    gl = jax.nn.gelu(y, approximate=True)
    gate = _dot(gl.astype(BF16), wg_ref[...]) + bg_ref[...]
    ys = _rms(gl * jax.nn.sigmoid(gate), sg_ref[...])
    cat = jnp.concatenate([yf.astype(BF16), ys.astype(BF16)], axis=-1)
    x1 = x_ref[...] + _rms(_dot(cat, wo_ref[...]), pmg_ref[...])
    h = _rms(x1, plg_ref[...]).astype(BF16)
    acc = jnp.zeros(x1.shape, F32)
    for j in range(D_FF // ff_blk):
        a = _dot(h, w1_ref[:, j * ff_blk:(j + 1) * ff_blk])
        a = jnp.square(jnp.maximum(a, 0.0)).astype(BF16)
        acc = acc + _dot(a, w2_ref[j * ff_blk:(j + 1) * ff_blk, :])
    o_ref[...] = x1 + _rms(acc, pog_ref[...])


def _mix_mlp(x2d, gri, ys, bdr, bdi, fg, wg, bg, sg, wo, pmg, plg, w1, w2, pog, tm, ff_blk):
    n = x2d.shape[0]
    row = lambda i: (i, 0)
    return pl.pallas_call(
        functools.partial(_mix_mlp_kernel, ff_blk=ff_blk),
        grid=(n // tm,),
        in_specs=[pl.BlockSpec((tm, D_MODEL), row),
                  pl.BlockSpec((None, tm, D_FOURIER), lambda i: (0, i, 0)),
                  pl.BlockSpec((None, tm, D_FOURIER), lambda i: (1, i, 0)),
                  pl.BlockSpec((tm, D_SSM), row),
                  _const_spec((D_FOURIER, D_FOURIER)), _const_spec((D_FOURIER, D_FOURIER)),
                  _const_spec((1, D_FOURIER)),
                  _const_spec((D_SSM, D_SSM)), _const_spec((1, D_SSM)), _const_spec((1, D_SSM)),
                  _const_spec((D_MODEL, D_MODEL)), _const_spec((1, D_MODEL)),
                  _const_spec((1, D_MODEL)),
                  _const_spec((D_MODEL, D_FF)), _const_spec((D_FF, D_MODEL)),
                  _const_spec((1, D_MODEL))],
        out_specs=pl.BlockSpec((tm, D_MODEL), row),
        out_shape=jax.ShapeDtypeStruct((n, D_MODEL), F32),
        compiler_params=pltpu.CompilerParams(
            dimension_semantics=("parallel",), vmem_limit_bytes=VMEM_LIMIT),
        name="mix_mlp",
    )(x2d, gri, gri, ys, bdr, bdi, fg, wg, bg, sg, wo, pmg, plg, w1, w2, pog)


def _fourier_channel_maps(w_f, seq):
    c = np.arange(FOURIER_HEAD_DIM)
    ang = 2.0 * np.pi * ((c[:, None] * c[None, :]) % FOURIER_HEAD_DIM) / FOURIER_HEAD_DIM
    scale = 1.0 / math.sqrt(seq * FOURIER_HEAD_DIM)
    hp = jax.lax.Precision.HIGHEST
    cw = jnp.einsum("cd,hde->hce", jnp.asarray(np.cos(ang) * scale, F32), w_f.astype(F32), precision=hp)
    sw = jnp.einsum("cd,hde->hce", jnp.asarray(np.sin(ang) * scale, F32), w_f.astype(F32), precision=hp)
    eye = jnp.eye(FOURIER_HEADS, dtype=F32)
    bd = lambda m: jnp.einsum("hce,hk->hcke", m, eye).reshape(D_FOURIER, D_FOURIER)
    return bd(cw).astype(BF16), bd(sw).astype(BF16)


def kernel(x, w_in, w_out, pre_mix_g, post_mix_g, pre_mlp_g, post_mlp_g, fourier_out_g, ssm_out_g,
           w_fourier, lam_re, lam_im, log_dt, b_re, b_im, c_re, c_im, d_skip, w_glu, b_glu,
           w_ff1, w_ff2):
    bsz, seq, _ = x.shape
    depth = w_in.shape[0]
    n = bsz * seq
    n1, n2 = FFT_N1, seq // FFT_N1
    nc = seq // CHUNK
    assert seq % (FFT_N1 * SUBLANES) == 0 and nc % SUBLANES == 0
    tm = min(512, n)
    wk_tab, m2_tab = _fft_tables(seq)
    row = lambda v: v.reshape(1, -1).astype(F32)

    x2d = x.reshape(n, D_MODEL).astype(F32)
    for i in range(depth):
        y, zs = _in_proj(x2d.reshape(bsz, n1, n2, D_MODEL), row(pre_mix_g[i]),
                         w_in[i].astype(BF16), wk_tab)
        zs = zs.reshape(n, D_SSM)
        g = _fft_stage2(y.reshape(bsz, n1, 2 * n2, D_FOURIER), m2_tab, kb=SUBLANES)
        gri = g.reshape(2, n, D_FOURIER)

        m_intra, b_pair, c_pair, tab = _ssm_operators(lam_re[i], lam_im[i], log_dt[i], b_re[i],
                                                      b_im[i], c_re[i], c_im[i], d_skip[i])
        ysr = _ssm_chunked(zs, m_intra, b_pair, c_pair, tab, bsz, seq)

        bdr, bdi = _fourier_channel_maps(w_fourier[i], seq)
        x2d = _mix_mlp(x2d, gri, ysr, bdr, bdi, row(fourier_out_g[i]), w_glu[i].astype(BF16),
                       row(b_glu[i]), row(ssm_out_g[i]), w_out[i].astype(BF16),
                       row(post_mix_g[i]), row(pre_mlp_g[i]), w_ff1[i].astype(BF16),
                       w_ff2[i].astype(BF16), row(post_mlp_g[i]), tm, ff_blk=1024)
    return x2d.reshape(bsz, seq, D_MODEL).astype(x.dtype)
```

```python
import functools
import math

import jax
import jax.numpy as jnp
import numpy as np
from jax.experimental import pallas as pl
from jax.experimental.pallas import tpu as pltpu

EPS = 1e-6
D_MODEL = 1024
D_FOURIER = 512
D_SSM = 512
FOURIER_HEADS = 8
FOURIER_HEAD_DIM = 64
SSM_GROUP = 16
SSM_GROUPS = 32
SSM_STATE = 64
D_FF = 4096

FFT_N1 = 64
CHUNK = 16
LANES = 128
SUBLANES = 8
MXU_DIM = 256
VMEM_LIMIT = 56 * 1024 * 1024

BF16 = jnp.bfloat16
F32 = jnp.float32


def _rms(x, g):
    return x * jax.lax.rsqrt(jnp.mean(x * x, axis=-1, keepdims=True) + EPS) * g


def _dot(a, b):
    return jnp.dot(a, b, preferred_element_type=F32)


def _const_spec(shape):
    nd = len(shape)
    return pl.BlockSpec(shape, lambda *_: (0,) * nd, pipeline_mode=pl.Buffered(1))


N2_BLK = SUBLANES


def _in_proj_kernel(x_ref, g_ref, w_ref, wk_ref, y_ref, zs_ref):
    n1 = x_ref.shape[0]
    x = x_ref[...].reshape(n1 * N2_BLK, D_MODEL)
    h = _rms(x, g_ref[...]).astype(BF16)
    z = _dot(h, w_ref[...])
    y = _dot(wk_ref[...], z[:, :D_FOURIER].astype(BF16))
    y_ref[...] = y.reshape(n1, 2, N2_BLK, D_FOURIER)
    zs_ref[...] = _rotate_rows(z[:, D_FOURIER:]).reshape(n1, N2_BLK, D_SSM)


def _in_proj(x4, g, w_bf16, wk):
    bsz, n1, n2, _ = x4.shape
    rows = n1 * N2_BLK
    return pl.pallas_call(
        _in_proj_kernel,
        grid=(bsz, n2 // N2_BLK),
        in_specs=[pl.BlockSpec((None, n1, N2_BLK, D_MODEL), lambda b, j: (b, 0, j, 0)),
                  _const_spec((1, D_MODEL)),
                  _const_spec((D_MODEL, D_MODEL)),
                  _const_spec((2 * rows, rows))],
        out_specs=[pl.BlockSpec((None, n1, 2, N2_BLK, D_FOURIER), lambda b, j: (b, 0, 0, j, 0)),
                   pl.BlockSpec((None, n1, N2_BLK, D_SSM), lambda b, j: (b, 0, j, 0))],
        out_shape=[jax.ShapeDtypeStruct((bsz, n1, 2, n2, D_FOURIER), F32),
                   jax.ShapeDtypeStruct((bsz, n1, n2, D_SSM), F32)],
        compiler_params=pltpu.CompilerParams(
            dimension_semantics=("parallel", "parallel"), vmem_limit_bytes=VMEM_LIMIT),
        name="in_proj",
    )(x4, g, w_bf16, wk)


def _fft2_kernel(m_ref, y_ref, g_ref, z_ref, *, kb, n2, pitch):
    nslab = D_FOURIER // LANES
    for j in range(kb):
        z = _dot(m_ref[j], y_ref[j].astype(BF16))
        for ri in range(2):
            for s in range(nslab):
                z_ref[ri, s, j * pitch:j * pitch + n2, :] = (
                    z[ri * n2:(ri + 1) * n2, s * LANES:(s + 1) * LANES])

    def regroup(k2, _):
        for ri in range(2):
            for s in range(nslab):
                g_ref[ri, k2, :, s * LANES:(s + 1) * LANES] = (
                    z_ref[ri, s, pl.ds(k2, kb, stride=pitch), :])
        return 0

    jax.lax.fori_loop(0, n2, regroup, 0)


def _fft_stage2(y, m2, kb):
    b, n1, rows, c = y.shape
    n2 = rows // 2
    pitch = n2 + SUBLANES
    return pl.pallas_call(
        functools.partial(_fft2_kernel, kb=kb, n2=n2, pitch=pitch),
        grid=(n1 // kb, b),
        in_specs=[pl.BlockSpec((kb, rows, rows), lambda j, i: (j, 0, 0)),
                  pl.BlockSpec((None, kb, rows, c), lambda j, i: (i, j, 0, 0))],
        out_specs=pl.BlockSpec((2, None, n2, kb, c), lambda j, i: (0, i, 0, j, 0)),
        out_shape=jax.ShapeDtypeStruct((2, b, n2, n1, c), F32),
        scratch_shapes=[pltpu.VMEM((2, c // LANES, kb * pitch, LANES), F32)],
        compiler_params=pltpu.CompilerParams(
            dimension_semantics=("parallel", "parallel"), vmem_limit_bytes=VMEM_LIMIT),
        name="fft_stage2",
    )(m2, y)


def _fft_tables(seq):
    n1, n2 = FFT_N1, seq // FFT_N1
    k1 = np.arange(n1)
    ang1 = 2.0 * np.pi * ((k1[:, None] * k1[None, :]) % n1) / n1
    w1 = np.stack([np.cos(ang1), -np.sin(ang1)], axis=1).reshape(2 * n1, n1)
    wk = np.einsum("kn,ab->kanb", w1, np.eye(N2_BLK)).reshape(2 * n1 * N2_BLK, n1 * N2_BLK)
    k2 = np.arange(n2)
    freq = k1[:, None, None] + n1 * k2[None, :, None]
    ang2 = 2.0 * np.pi * ((freq * k2[None, None, :]) % seq) / seq
    mr, mi = np.cos(ang2), -np.sin(ang2)
    m2 = np.concatenate([np.concatenate([mr, -mi], axis=2),
                         np.concatenate([mi, mr], axis=2)], axis=1)
    return jnp.asarray(wk, F32).astype(BF16), jnp.asarray(m2, F32).astype(BF16)


GROUPS_PER_BLOCK = LANES // SSM_GROUP
PAIRS_PER_BLOCK = GROUPS_PER_BLOCK // 2
RELAYOUT_CHUNKS = 4 * SUBLANES


def _rotate_rows(v):
    return jnp.concatenate(
        [pltpu.roll(v[:, c:c + LANES], 0, axis=1, stride=SSM_GROUP, stride_axis=0)
         for c in range(0, v.shape[1], LANES)], axis=1)


def _unrotate_rows(v):
    row = jax.lax.broadcasted_iota(jnp.int32, v.shape, 0)
    cols = []
    for c in range(0, v.shape[1], LANES):
        w = v[:, c:c + LANES]
        for bit in (1, 2, 4):
            w = jnp.where((row[:, :LANES] & bit) != 0,
                          pltpu.roll(w, LANES - SSM_GROUP * bit, axis=1), w)
        cols.append(w)
    return jnp.concatenate(cols, axis=1)


def _ssm_kernel(z_ref, mi_ref, bs_ref, cs_ref, tab_ref, y_ref,
                u2_ref, s_ref, car_ref, yg_ref, *, nc):
    gpb, npair = GROUPS_PER_BLOCK, PAIRS_PER_BLOCK
    t16 = RELAYOUT_CHUNKS
    rows_it = t16 * CHUNK
    lane_blk = jax.lax.broadcasted_iota(jnp.int32, (t16, LANES), 1) // SSM_GROUP
    masks = [lane_blk == m for m in range(gpb)]

    def relayout_in(i, _):
        r0 = pl.multiple_of(i * rows_it, rows_it)
        ut = [z_ref[pl.ds(r0 + t, t16, stride=CHUNK), :] for t in range(CHUNK)]
        c0 = pl.multiple_of(i * t16, t16)
        for g in range(gpb):
            for j in range(CHUNK // gpb):
                acc = ut[gpb * j]
                for tt in range(1, gpb):
                    acc = jnp.where(masks[(g + tt) % gpb], ut[gpb * j + tt], acc)
                if g:
                    acc = pltpu.roll(acc, LANES - SSM_GROUP * g, axis=1)
                u2_ref[g, pl.ds(c0, t16), j * LANES:(j + 1) * LANES] = acc.astype(BF16)
        return 0

    jax.lax.fori_loop(0, nc // t16, relayout_in, 0)

    for q in range(npair):
        up = jnp.concatenate([u2_ref[2 * q], u2_ref[2 * q + 1]], axis=1)
        s_ref[q] = _dot(up, bs_ref[q])

    row = jax.lax.broadcasted_iota(jnp.int32, (SUBLANES, LANES), 0)
    nblk = nc // SUBLANES

    def scan_block(tab, sr, si, cr, ci, down):
        keep = (lambda sh: row >= sh) if down else (lambda sh: row <= SUBLANES - 1 - sh)
        amt = (lambda sh: sh) if down else (lambda sh: SUBLANES - sh)
        for k, sh in enumerate((1, 2, 4)):
            tr = pltpu.roll(sr, amt(sh), axis=0)
            ti = pltpu.roll(si, amt(sh), axis=0)
            ar, ai = tab[2 * k], tab[2 * k + 1]
            sr, si = sr + (ar * tr - ai * ti), si + (ar * ti + ai * tr)
        fr = sr + (tab[6] * cr - tab[7] * ci)
        fi = si + (tab[6] * ci + tab[7] * cr)
        outr = jnp.where(keep(1), pltpu.roll(fr, amt(1), axis=0), cr)
        outi = jnp.where(keep(1), pltpu.roll(fi, amt(1), axis=0), ci)
        edge = SUBLANES - 1 if down else 0
        ncr = jnp.broadcast_to(fr[edge:edge + 1, :], (SUBLANES, LANES))
        nci = jnp.broadcast_to(fi[edge:edge + 1, :], (SUBLANES, LANES))
        return outr, outi, ncr, nci

    def scan_body(i, carry):
        new = []
        for q in range(npair):
            for d in range(2):
                blk = i if d == 0 else nblk - 1 - i
                r0 = pl.multiple_of(blk * SUBLANES, SUBLANES)
                lo = 2 * d * LANES
                tab = [tab_ref[q, d, k] for k in range(8)]
                cr, ci = carry[4 * q + 2 * d], carry[4 * q + 2 * d + 1]
                outr, outi, cr, ci = scan_block(
                    tab, s_ref[q, pl.ds(r0, SUBLANES), lo:lo + LANES],
                    s_ref[q, pl.ds(r0, SUBLANES), lo + LANES:lo + 2 * LANES], cr, ci, d == 0)
                car_ref[q, pl.ds(r0, SUBLANES), lo:lo + LANES] = outr
                car_ref[q, pl.ds(r0, SUBLANES), lo + LANES:lo + 2 * LANES] = outi
                new += [cr, ci]
        return tuple(new)

    zero = jnp.zeros((SUBLANES, LANES), F32)
    jax.lax.fori_loop(0, nblk, scan_body, (zero,) * (4 * npair))

    kk = CHUNK * SSM_GROUP
    for q in range(npair):
        yi = _dot(car_ref[q].astype(BF16), cs_ref[q])
        for e in range(2):
            g = 2 * q + e
            yg_ref[g] = _dot(u2_ref[g], mi_ref[g]) + yi[:, e * kk:(e + 1) * kk]

    def relayout_out(i, _):
        c0 = pl.multiple_of(i * t16, t16)
        r0 = pl.multiple_of(i * rows_it, rows_it)
        yrot = []
        for g in range(gpb):
            halves = []
            for j in range(CHUNK // gpb):
                v = yg_ref[g, pl.ds(c0, t16), j * LANES:(j + 1) * LANES]
                halves.append(pltpu.roll(v, SSM_GROUP * g, axis=1) if g else v)
            yrot.append(halves)
        for t in range(CHUNK):
            j, tt = divmod(t, gpb)
            acc = yrot[0][j]
            for g in range(1, gpb):
                acc = jnp.where(masks[(g + tt) % gpb], yrot[g][j], acc)
            y_ref[pl.ds(r0 + t, t16, stride=CHUNK), :] = acc
        return 0

    jax.lax.fori_loop(0, nc // t16, relayout_out, 0)


def _ssm_chunked(zs, m_intra, b_pair, c_pair, tab, bsz, seq):
    nc = seq // CHUNK
    gpb, npair = GROUPS_PER_BLOCK, PAIRS_PER_BLOCK
    k = CHUNK * SSM_GROUP
    return pl.pallas_call(
        functools.partial(_ssm_kernel, nc=nc),
        grid=(D_SSM // LANES, bsz),
        in_specs=[pl.BlockSpec((seq, LANES), lambda j, b: (b, j)),
                  pl.BlockSpec((gpb, k, k), lambda j, b: (j, 0, 0)),
                  pl.BlockSpec((npair, 2 * k, 2 * k), lambda j, b: (j, 0, 0)),
                  pl.BlockSpec((npair, 2 * k, 2 * k), lambda j, b: (j, 0, 0)),
                  pl.BlockSpec((npair, 2, 8, SUBLANES, LANES), lambda j, b: (j, 0, 0, 0, 0))],
        out_specs=pl.BlockSpec((seq, LANES), lambda j, b: (b, j)),
        out_shape=jax.ShapeDtypeStruct((bsz * seq, D_SSM), F32),
        scratch_shapes=[                        pltpu.VMEM((gpb, nc, k), BF16),
                        pltpu.VMEM((npair, nc, 2 * k), F32),
                        pltpu.VMEM((npair, nc, 2 * k), F32),
                        pltpu.VMEM((gpb, nc, k), F32)],
        compiler_params=pltpu.CompilerParams(
            dimension_semantics=("parallel", "parallel"), vmem_limit_bytes=VMEM_LIMIT),
        name="ssm_chunk",
    )(zs, m_intra, b_pair, c_pair, tab)


def _ssm_operators(lam_re, lam_im, log_dt, b_re, b_im, c_re, c_im, d_skip):
    t, h, p, g = CHUNK, SSM_GROUP, SSM_STATE, SSM_GROUPS
    kk, nq = t * h, g // 2
    hp = jax.lax.Precision.HIGHEST
    dt = jnp.exp(log_dt.astype(F32))[..., None]
    are = lam_re.astype(F32) * dt
    aim = lam_im.astype(F32) * dt

    def power(expo, a_re, a_im):
        mag = jnp.exp(a_re * expo)
        return mag * jnp.cos(a_im * expo), mag * jnp.sin(a_im * expo)

    lbr, lbi = power(1.0, are, aim)
    den = lam_re * lam_re + lam_im * lam_im
    cr = ((lbr - 1.0) * lam_re + lbi * lam_im) / den
    ci = (lbi * lam_re - (lbr - 1.0) * lam_im) / den
    bbr = cr[..., None] * b_re - ci[..., None] * b_im
    bbi = cr[..., None] * b_im + ci[..., None] * b_re
    btr, bti = jnp.swapaxes(bbr, -1, -2), jnp.swapaxes(bbi, -1, -2)
    ctr = jnp.tile(jnp.swapaxes(c_re.astype(F32), -1, -2), (1, 1, 1, t))
    cti = jnp.tile(jnp.swapaxes(c_im.astype(F32), -1, -2), (1, 1, 1, t))
    blk = np.arange(kk) // h
    tt = np.arange(t)
    steps = jnp.arange(t + 1, dtype=F32)
    plr, pli = power(steps, are[..., None], aim[..., None])
    ptr, pti = power(steps[:, None], are[:, :, None, :], aim[:, :, None, :])

    def expand(tab, idx_f, idx_b, axis):
        return jnp.stack([jnp.repeat(jnp.take(tab[0], idx_f, axis=axis), h, axis=axis),
                          jnp.repeat(jnp.take(tab[1], idx_b, axis=axis), h, axis=axis)])

    def c_times_power(idx_f, idx_b):
        pr, pi = expand(plr, idx_f, idx_b, -1), expand(pli, idx_f, idx_b, -1)
        return ctr * pr - cti * pi, ctr * pi + cti * pr

    clr, cli = c_times_power(tt, (t - tt) % t)
    gen = (jnp.einsum("dghp,dgpc->dghc", btr, clr, precision=hp)
           - jnp.einsum("dghp,dgpc->dghc", bti, cli, precision=hp))
    diag = gen[1, :, :, :h] + d_skip.astype(F32)[:, None, :] * jnp.eye(h, dtype=F32)[None]
    kf = gen[0] + jnp.pad(diag, ((0, 0), (0, 0), (0, kk - h)))
    kb = gen[1]
    lane_t = jnp.asarray(blk)
    rows = [jnp.where(lane_t >= s, jnp.roll(kf, h * s, axis=-1), jnp.roll(kb, h * s, axis=-1))
            for s in range(t)]
    m_intra = jnp.stack(rows, axis=1).reshape(g, kk, kk)

    pr, pi = expand(ptr, t - 1 - tt, tt, -2), expand(pti, t - 1 - tt, tt, -2)
    btr_t, bti_t = jnp.tile(btr, (1, 1, t, 1)), jnp.tile(bti, (1, 1, t, 1))
    sb = jnp.stack([btr_t * pr - bti_t * pi, btr_t * pi + bti_t * pr], axis=1)
    sb = sb.reshape(2, 2, nq, 2, kk, p)
    zero = jnp.zeros_like(sb[:, :, :, 0])
    sb = jnp.stack([jnp.concatenate([sb[:, :, :, 0], zero], -1),
                    jnp.concatenate([zero, sb[:, :, :, 1]], -1)], axis=3)
    b_pair = sb.transpose(2, 3, 4, 0, 1, 5).reshape(nq, 2 * kk, 8 * p)

    cqr, cqi = c_times_power(tt + 1, t - tt)
    cq = jnp.stack([cqr, -cqi], axis=1).reshape(2, 2, nq, 2, p, kk)
    zero = jnp.zeros_like(cq[:, :, :, 0])
    cq = jnp.stack([jnp.concatenate([cq[:, :, :, 0], zero], -1),
                    jnp.concatenate([zero, cq[:, :, :, 1]], -1)], axis=3)
    c_pair = cq.transpose(2, 0, 1, 3, 4, 5).reshape(nq, 8 * p, 2 * kk)

    r8 = np.arange(SUBLANES)
    expo = np.stack([np.stack([np.full(SUBLANES, k), np.full(SUBLANES, k)]) for k in (1, 2, 4)]
                    + [np.stack([r8 + 1, SUBLANES - r8])])
    expo = jnp.asarray(t * expo, F32)[:, :, None, :, None]
    ar, ai = power(expo, are.reshape(2, nq, 1, 2 * p)[None], aim.reshape(2, nq, 1, 2 * p)[None])
    live = np.stack([np.stack([r8 >= k, r8 <= SUBLANES - 1 - k]) for k in (1, 2, 4)]
                    + [np.ones((2, SUBLANES), bool)])
    live = jnp.asarray(live, F32)[:, :, None, :, None]
    ar, ai = ar * live, ai * live
    tab = jnp.stack([ar, ai], axis=1).reshape(8, 2, nq, SUBLANES, 2 * p)
    tab = tab.transpose(2, 1, 0, 3, 4)
    return m_intra.astype(BF16), b_pair.astype(BF16), c_pair.astype(BF16), tab


def _mix_mlp_kernel(x_ref, gr_ref, gi_ref, ys_ref, bdr_ref, bdi_ref, fg_ref, wg_ref, bg_ref,
                    sg_ref, wo_ref, pmg_ref, plg_ref, w1_ref, w2_ref, pog_ref, o_ref, *, ff_blk):
    gr, gi = gr_ref[...].astype(BF16), gi_ref[...].astype(BF16)
    yf = jnp.concatenate(
        [_dot(gr[:, c:c + MXU_DIM], bdr_ref[c:c + MXU_DIM, c:c + MXU_DIM])
         + _dot(gi[:, c:c + MXU_DIM], bdi_ref[c:c + MXU_DIM, c:c + MXU_DIM])
         for c in range(0, D_FOURIER, MXU_DIM)], axis=1)
    yf = _rms(yf, fg_ref[...])
    y = _unrotate_rows(ys_ref[...])
    gl = jax.nn.gelu(y, approximate=True)
    gate = _dot(gl.astype(BF16), wg_ref[...]) + bg_ref[...]
    ys = _rms(gl * jax.nn.sigmoid(gate), sg_ref[...])
    cat = jnp.concatenate([yf.astype(BF16), ys.astype(BF16)], axis=-1)
    x1 = x_ref[...] + _rms(_dot(cat, wo_ref[...]), pmg_ref[...])
    h = _rms(x1, plg_ref[...]).astype(BF16)
    acc = jnp.zeros(x1.shape, F32)
    for j in range(D_FF // ff_blk):
        a = _dot(h, w1_ref[:, j * ff_blk:(j + 1) * ff_blk])
        a = jnp.square(jnp.maximum(a, 0.0)).astype(BF16)
        acc = acc + _dot(a, w2_ref[j * ff_blk:(j + 1) * ff_blk, :])
    o_ref[...] = x1 + _rms(acc, pog_ref[...])


def _mix_mlp(x2d, gri, ys, bdr, bdi, fg, wg, bg, sg, wo, pmg, plg, w1, w2, pog, tm, ff_blk):
    n = x2d.shape[0]
    row = lambda i: (i, 0)
    return pl.pallas_call(
        functools.partial(_mix_mlp_kernel, ff_blk=ff_blk),
        grid=(n // tm,),
        in_specs=[pl.BlockSpec((tm, D_MODEL), row),
                  pl.BlockSpec((None, tm, D_FOURIER), lambda i: (0, i, 0)),
                  pl.BlockSpec((None, tm, D_FOURIER), lambda i: (1, i, 0)),
                  pl.BlockSpec((tm, D_SSM), row),
                  _const_spec((D_FOURIER, D_FOURIER)), _const_spec((D_FOURIER, D_FOURIER)),
                  _const_spec((1, D_FOURIER)),
                  _const_spec((D_SSM, D_SSM)), _const_spec((1, D_SSM)), _const_spec((1, D_SSM)),
                  _const_spec((D_MODEL, D_MODEL)), _const_spec((1, D_MODEL)),
                  _const_spec((1, D_MODEL)),
                  _const_spec((D_MODEL, D_FF)), _const_spec((D_FF, D_MODEL)),
                  _const_spec((1, D_MODEL))],
        out_specs=pl.BlockSpec((tm, D_MODEL), row),
        out_shape=jax.ShapeDtypeStruct((n, D_MODEL), F32),
        compiler_params=pltpu.CompilerParams(
            dimension_semantics=("parallel",), vmem_limit_bytes=VMEM_LIMIT),
        name="mix_mlp",
    )(x2d, gri, gri, ys, bdr, bdi, fg, wg, bg, sg, wo, pmg, plg, w1, w2, pog)


def _fourier_channel_maps(w_f, seq):
    c = np.arange(FOURIER_HEAD_DIM)
    ang = 2.0 * np.pi * ((c[:, None] * c[None, :]) % FOURIER_HEAD_DIM) / FOURIER_HEAD_DIM
    scale = 1.0 / math.sqrt(seq * FOURIER_HEAD_DIM)
    hp = jax.lax.Precision.HIGHEST
    cw = jnp.einsum("cd,hde->hce", jnp.asarray(np.cos(ang) * scale, F32), w_f.astype(F32), precision=hp)
    sw = jnp.einsum("cd,hde->hce", jnp.asarray(np.sin(ang) * scale, F32), w_f.astype(F32), precision=hp)
    eye = jnp.eye(FOURIER_HEADS, dtype=F32)
    bd = lambda m: jnp.einsum("hce,hk->hcke", m, eye).reshape(D_FOURIER, D_FOURIER)
    return bd(cw).astype(BF16), bd(sw).astype(BF16)


def kernel(x, w_in, w_out, pre_mix_g, post_mix_g, pre_mlp_g, post_mlp_g, fourier_out_g, ssm_out_g,
           w_fourier, lam_re, lam_im, log_dt, b_re, b_im, c_re, c_im, d_skip, w_glu, b_glu,
           w_ff1, w_ff2):
    bsz, seq, _ = x.shape
    depth = w_in.shape[0]
    n = bsz * seq
    n1, n2 = FFT_N1, seq // FFT_N1
    nc = seq // CHUNK
    assert seq % (FFT_N1 * SUBLANES) == 0 and nc % SUBLANES == 0
    tm = min(512, n)
    wk_tab, m2_tab = _fft_tables(seq)
    row = lambda v: v.reshape(1, -1).astype(F32)

    x2d = x.reshape(n, D_MODEL).astype(F32)
    for i in range(depth):
        y, zs = _in_proj(x2d.reshape(bsz, n1, n2, D_MODEL), row(pre_mix_g[i]),
                         w_in[i].astype(BF16), wk_tab)
        zs = zs.reshape(n, D_SSM)
        g = _fft_stage2(y.reshape(bsz, n1, 2 * n2, D_FOURIER), m2_tab, kb=SUBLANES)
        gri = g.reshape(2, n, D_FOURIER)

        m_intra, b_pair, c_pair, tab = _ssm_operators(lam_re[i], lam_im[i], log_dt[i], b_re[i],
                                                      b_im[i], c_re[i], c_im[i], d_skip[i])
        ysr = _ssm_chunked(zs, m_intra, b_pair, c_pair, tab, bsz, seq)

        bdr, bdi = _fourier_channel_maps(w_fourier[i], seq)
        x2d = _mix_mlp(x2d, gri, ysr, bdr, bdi, row(fourier_out_g[i]), w_glu[i].astype(BF16),
                       row(b_glu[i]), row(ssm_out_g[i]), w_out[i].astype(BF16),
                       row(post_mix_g[i]), row(pre_mlp_g[i]), w_ff1[i].astype(BF16),
                       w_ff2[i].astype(BF16), row(post_mlp_g[i]), tm, ff_blk=1024)
    return x2d.reshape(bsz, seq, D_MODEL).astype(x.dtype)
```

```python
import functools
import math

import jax
import jax.numpy as jnp
import numpy as np
from jax.experimental import pallas as pl
from jax.experimental.pallas import tpu as pltpu

EPS = 1e-6
D_MODEL = 1024
D_FOURIER = 512
D_SSM = 512
FOURIER_HEADS = 8
FOURIER_HEAD_DIM = 64
SSM_GROUP = 16
SSM_GROUPS = 32
SSM_STATE = 64
D_FF = 4096

FFT_N1 = 64
CHUNK = 16
LANES = 128
SUBLANES = 8
MXU_DIM = 256
VMEM_LIMIT = 56 * 1024 * 1024

BF16 = jnp.bfloat16
F32 = jnp.float32


def _rms(x, g):
    return x * jax.lax.rsqrt(jnp.mean(x * x, axis=-1, keepdims=True) + EPS) * g


def _dot(a, b):
    return jnp.dot(a, b, preferred_element_type=F32)


def _const_spec(shape):
    nd = len(shape)
    return pl.BlockSpec(shape, lambda *_: (0,) * nd, pipeline_mode=pl.Buffered(1))


def _layer_spec(shape, layer):
    nd = len(shape)
    return pl.BlockSpec((None,) + tuple(shape), lambda *_: (layer,) + (0,) * nd,
                        pipeline_mode=pl.Buffered(1))


N2_BLK = SUBLANES


def _in_proj_kernel(x_ref, g_ref, w_ref, wk_ref, y_ref, zs_ref):
    n1 = x_ref.shape[0]
    x = x_ref[...].reshape(n1 * N2_BLK, D_MODEL)
    h = _rms(x, g_ref[...]).astype(BF16)
    z = _dot(h, w_ref[...])
    y = _dot(wk_ref[...], z[:, :D_FOURIER].astype(BF16))
    y_ref[...] = y.reshape(n1, 2, N2_BLK, D_FOURIER)
    zs_ref[...] = _rotate_rows(z[:, D_FOURIER:]).reshape(n1, N2_BLK, D_SSM)


def _in_proj(x4, g, w_bf16, wk, layer):
    bsz, n1, n2, _ = x4.shape
    rows = n1 * N2_BLK
    return pl.pallas_call(
        _in_proj_kernel,
        grid=(bsz, n2 // N2_BLK),
        in_specs=[pl.BlockSpec((None, n1, N2_BLK, D_MODEL), lambda b, j: (b, 0, j, 0)),
                  _layer_spec((1, D_MODEL), layer),
                  _layer_spec((D_MODEL, D_MODEL), layer),
                  _const_spec((2 * rows, rows))],
        out_specs=[pl.BlockSpec((None, n1, 2, N2_BLK, D_FOURIER), lambda b, j: (b, 0, 0, j, 0)),
                   pl.BlockSpec((None, n1, N2_BLK, D_SSM), lambda b, j: (b, 0, j, 0))],
        out_shape=[jax.ShapeDtypeStruct((bsz, n1, 2, n2, D_FOURIER), F32),
                   jax.ShapeDtypeStruct((bsz, n1, n2, D_SSM), F32)],
        compiler_params=pltpu.CompilerParams(
            dimension_semantics=("parallel", "parallel"), vmem_limit_bytes=VMEM_LIMIT),
        name="in_proj",
    )(x4, g, w_bf16, wk)


def _fft2_kernel(m_ref, y_ref, g_ref, z_ref, *, kb, n2, pitch):
    nslab = D_FOURIER // LANES
    for j in range(kb):
        z = _dot(m_ref[j], y_ref[j].astype(BF16))
        for ri in range(2):
            for s in range(nslab):
                z_ref[ri, s, j * pitch:j * pitch + n2, :] = (
                    z[ri * n2:(ri + 1) * n2, s * LANES:(s + 1) * LANES])

    def regroup(k2, _):
        for ri in range(2):
            for s in range(nslab):
                g_ref[ri, k2, :, s * LANES:(s + 1) * LANES] = (
                    z_ref[ri, s, pl.ds(k2, kb, stride=pitch), :])
        return 0

    jax.lax.fori_loop(0, n2, regroup, 0)


def _fft_stage2(y, m2, kb):
    b, n1, rows, c = y.shape
    n2 = rows // 2
    pitch = n2 + SUBLANES
    return pl.pallas_call(
        functools.partial(_fft2_kernel, kb=kb, n2=n2, pitch=pitch),
        grid=(n1 // kb, b),
        in_specs=[pl.BlockSpec((kb, rows, rows), lambda j, i: (j, 0, 0)),
                  pl.BlockSpec((None, kb, rows, c), lambda j, i: (i, j, 0, 0))],
        out_specs=pl.BlockSpec((2, None, n2, kb, c), lambda j, i: (0, i, 0, j, 0)),
        out_shape=jax.ShapeDtypeStruct((2, b, n2, n1, c), F32),
        scratch_shapes=[pltpu.VMEM((2, c // LANES, kb * pitch, LANES), F32)],
        compiler_params=pltpu.CompilerParams(
            dimension_semantics=("parallel", "parallel"), vmem_limit_bytes=VMEM_LIMIT),
        name="fft_stage2",
    )(m2, y)


def _fft_tables(seq):
    n1, n2 = FFT_N1, seq // FFT_N1
    k1 = np.arange(n1)
    ang1 = 2.0 * np.pi * ((k1[:, None] * k1[None, :]) % n1) / n1
    w1 = np.stack([np.cos(ang1), -np.sin(ang1)], axis=1).reshape(2 * n1, n1)
    wk = np.einsum("kn,ab->kanb", w1, np.eye(N2_BLK)).reshape(2 * n1 * N2_BLK, n1 * N2_BLK)
    k2 = np.arange(n2)
    freq = k1[:, None, None] + n1 * k2[None, :, None]
    ang2 = 2.0 * np.pi * ((freq * k2[None, None, :]) % seq) / seq
    mr, mi = np.cos(ang2), -np.sin(ang2)
    m2 = np.concatenate([np.concatenate([mr, -mi], axis=2),
                         np.concatenate([mi, mr], axis=2)], axis=1)
    return jnp.asarray(wk, F32).astype(BF16), jnp.asarray(m2, F32).astype(BF16)


GROUPS_PER_BLOCK = LANES // SSM_GROUP
PAIRS_PER_BLOCK = GROUPS_PER_BLOCK // 2
RELAYOUT_CHUNKS = 4 * SUBLANES


def _rotate_rows(v):
    return jnp.concatenate(
        [pltpu.roll(v[:, c:c + LANES], 0, axis=1, stride=SSM_GROUP, stride_axis=0)
         for c in range(0, v.shape[1], LANES)], axis=1)


def _unrotate_rows(v):
    row = jax.lax.broadcasted_iota(jnp.int32, v.shape, 0)
    cols = []
    for c in range(0, v.shape[1], LANES):
        w = v[:, c:c + LANES]
        for bit in (1, 2, 4):
            w = jnp.where((row[:, :LANES] & bit) != 0,
                          pltpu.roll(w, LANES - SSM_GROUP * bit, axis=1), w)
        cols.append(w)
    return jnp.concatenate(cols, axis=1)


def _ssm_kernel(z_ref, mi_ref, bs_ref, cs_ref, tab_ref, y_ref,
                u2_ref, s_ref, car_ref, yg_ref, *, nc):
    gpb, npair = GROUPS_PER_BLOCK, PAIRS_PER_BLOCK
    t16 = RELAYOUT_CHUNKS
    rows_it = t16 * CHUNK
    lane_blk = jax.lax.broadcasted_iota(jnp.int32, (t16, LANES), 1) // SSM_GROUP
    masks = [lane_blk == m for m in range(gpb)]

    def relayout_in(i, _):
        r0 = pl.multiple_of(i * rows_it, rows_it)
        ut = [z_ref[pl.ds(r0 + t, t16, stride=CHUNK), :] for t in range(CHUNK)]
        c0 = pl.multiple_of(i * t16, t16)
        for g in range(gpb):
            for j in range(CHUNK // gpb):
                acc = ut[gpb * j]
                for tt in range(1, gpb):
                    acc = jnp.where(masks[(g + tt) % gpb], ut[gpb * j + tt], acc)
                if g:
                    acc = pltpu.roll(acc, LANES - SSM_GROUP * g, axis=1)
                u2_ref[g, pl.ds(c0, t16), j * LANES:(j + 1) * LANES] = acc.astype(BF16)
        return 0

    jax.lax.fori_loop(0, nc // t16, relayout_in, 0)

    for q in range(npair):
        up = jnp.concatenate([u2_ref[2 * q], u2_ref[2 * q + 1]], axis=1)
        s_ref[q] = _dot(up, bs_ref[q])

    row = jax.lax.broadcasted_iota(jnp.int32, (SUBLANES, LANES), 0)
    nblk = nc // SUBLANES

    def scan_block(tab, sr, si, cr, ci, down):
        keep = (lambda sh: row >= sh) if down else (lambda sh: row <= SUBLANES - 1 - sh)
        amt = (lambda sh: sh) if down else (lambda sh: SUBLANES - sh)
        for k, sh in enumerate((1, 2, 4)):
            tr = pltpu.roll(sr, amt(sh), axis=0)
            ti = pltpu.roll(si, amt(sh), axis=0)
            ar, ai = tab[2 * k], tab[2 * k + 1]
            sr, si = sr + (ar * tr - ai * ti), si + (ar * ti + ai * tr)
        fr = sr + (tab[6] * cr - tab[7] * ci)
        fi = si + (tab[6] * ci + tab[7] * cr)
        outr = jnp.where(keep(1), pltpu.roll(fr, amt(1), axis=0), cr)
        outi = jnp.where(keep(1), pltpu.roll(fi, amt(1), axis=0), ci)
        edge = SUBLANES - 1 if down else 0
        ncr = jnp.broadcast_to(fr[edge:edge + 1, :], (SUBLANES, LANES))
        nci = jnp.broadcast_to(fi[edge:edge + 1, :], (SUBLANES, LANES))
        return outr, outi, ncr, nci

    def scan_body(i, carry):
        new = []
        for q in range(npair):
            for d in range(2):
                blk = i if d == 0 else nblk - 1 - i
                r0 = pl.multiple_of(blk * SUBLANES, SUBLANES)
                lo = 2 * d * LANES
                tab = [tab_ref[q, d, k] for k in range(8)]
                cr, ci = carry[4 * q + 2 * d], carry[4 * q + 2 * d + 1]
                outr, outi, cr, ci = scan_block(
                    tab, s_ref[q, pl.ds(r0, SUBLANES), lo:lo + LANES],
                    s_ref[q, pl.ds(r0, SUBLANES), lo + LANES:lo + 2 * LANES], cr, ci, d == 0)
                car_ref[q, pl.ds(r0, SUBLANES), lo:lo + LANES] = outr
                car_ref[q, pl.ds(r0, SUBLANES), lo + LANES:lo + 2 * LANES] = outi
                new += [cr, ci]
        return tuple(new)

    zero = jnp.zeros((SUBLANES, LANES), F32)
    jax.lax.fori_loop(0, nblk, scan_body, (zero,) * (4 * npair))

    kk = CHUNK * SSM_GROUP
    for q in range(npair):
        yi = _dot(car_ref[q].astype(BF16), cs_ref[q])
        for e in range(2):
            g = 2 * q + e
            yg_ref[g] = _dot(u2_ref[g], mi_ref[g]) + yi[:, e * kk:(e + 1) * kk]

    def relayout_out(i, _):
        c0 = pl.multiple_of(i * t16, t16)
        r0 = pl.multiple_of(i * rows_it, rows_it)
        yrot = []
        for g in range(gpb):
            halves = []
            for j in range(CHUNK // gpb):
                v = yg_ref[g, pl.ds(c0, t16), j * LANES:(j + 1) * LANES]
                halves.append(pltpu.roll(v, SSM_GROUP * g, axis=1) if g else v)
            yrot.append(halves)
        for t in range(CHUNK):
            j, tt = divmod(t, gpb)
            acc = yrot[0][j]
            for g in range(1, gpb):
                acc = jnp.where(masks[(g + tt) % gpb], yrot[g][j], acc)
            y_ref[pl.ds(r0 + t, t16, stride=CHUNK), :] = acc
        return 0

    jax.lax.fori_loop(0, nc // t16, relayout_out, 0)


def _ssm_chunked(zs, m_intra, b_pair, c_pair, tab, bsz, seq, layer):
    nc = seq // CHUNK
    gpb, npair = GROUPS_PER_BLOCK, PAIRS_PER_BLOCK
    k = CHUNK * SSM_GROUP
    return pl.pallas_call(
        functools.partial(_ssm_kernel, nc=nc),
        grid=(D_SSM // LANES, bsz),
        in_specs=[pl.BlockSpec((seq, LANES), lambda j, b: (b, j)),
                  pl.BlockSpec((None, gpb, k, k), lambda j, b: (layer, j, 0, 0)),
                  pl.BlockSpec((None, npair, 2 * k, 2 * k), lambda j, b: (layer, j, 0, 0)),
                  pl.BlockSpec((None, npair, 2 * k, 2 * k), lambda j, b: (layer, j, 0, 0)),
                  pl.BlockSpec((None, npair, 2, 8, SUBLANES, LANES),
                               lambda j, b: (layer, j, 0, 0, 0, 0))],
        out_specs=pl.BlockSpec((seq, LANES), lambda j, b: (b, j)),
        out_shape=jax.ShapeDtypeStruct((bsz * seq, D_SSM), F32),
        scratch_shapes=[pltpu.VMEM((gpb, nc, k), BF16),
                        pltpu.VMEM((npair, nc, 2 * k), F32),
                        pltpu.VMEM((npair, nc, 2 * k), F32),
                        pltpu.VMEM((gpb, nc, k), F32)],
        compiler_params=pltpu.CompilerParams(
            dimension_semantics=("parallel", "parallel"), vmem_limit_bytes=VMEM_LIMIT),
        name="ssm_chunk",
    )(zs, m_intra, b_pair, c_pair, tab)


def _ssm_operators(lam_re, lam_im, log_dt, b_re, b_im, c_re, c_im, d_skip):
    t, h, p, g = CHUNK, SSM_GROUP, SSM_STATE, SSM_GROUPS
    kk, nq = t * h, g // 2
    dt = jnp.exp(log_dt.astype(F32))[..., None]
    are = lam_re.astype(F32) * dt
    aim = lam_im.astype(F32) * dt

    def power(expo, a_re, a_im):
        mag = jnp.exp(a_re * expo)
        return mag * jnp.cos(a_im * expo), mag * jnp.sin(a_im * expo)

    lbr, lbi = power(1.0, are, aim)
    den = lam_re * lam_re + lam_im * lam_im
    cr = ((lbr - 1.0) * lam_re + lbi * lam_im) / den
    ci = (lbi * lam_re - (lbr - 1.0) * lam_im) / den
    bbr = cr[..., None] * b_re - ci[..., None] * b_im
    bbi = cr[..., None] * b_im + ci[..., None] * b_re
    btr, bti = jnp.swapaxes(bbr, -1, -2), jnp.swapaxes(bbi, -1, -2)
    ctr = jnp.tile(jnp.swapaxes(c_re.astype(F32), -1, -2), (1, 1, 1, t))
    cti = jnp.tile(jnp.swapaxes(c_im.astype(F32), -1, -2), (1, 1, 1, t))
    blk = np.arange(kk) // h
    tt = np.arange(t)
    steps = jnp.arange(t + 1, dtype=F32)
    plr, pli = power(steps, are[..., None], aim[..., None])
    ptr, pti = power(steps[:, None], are[:, :, None, :], aim[:, :, None, :])

    def expand(tab, idx_f, idx_b, axis):
        return jnp.stack([jnp.repeat(jnp.take(tab[0], idx_f, axis=axis), h, axis=axis),
                          jnp.repeat(jnp.take(tab[1], idx_b, axis=axis), h, axis=axis)])

    def c_times_power(idx_f, idx_b):
        pr, pi = expand(plr, idx_f, idx_b, -1), expand(pli, idx_f, idx_b, -1)
        return ctr * pr - cti * pi, ctr * pi + cti * pr

    clr, cli = c_times_power(tt, (t - tt) % t)
    gen = (jnp.einsum("dghp,dgpc->dghc", btr, clr)
           - jnp.einsum("dghp,dgpc->dghc", bti, cli))
    diag = gen[1, :, :, :h] + d_skip.astype(F32)[:, None, :] * jnp.eye(h, dtype=F32)[None]
    kf = gen[0] + jnp.pad(diag, ((0, 0), (0, 0), (0, kk - h)))
    kb = gen[1]
    lane_t = jnp.asarray(blk)
    rows = [jnp.where(lane_t >= s, jnp.roll(kf, h * s, axis=-1), jnp.roll(kb, h * s, axis=-1))
            for s in range(t)]
    m_intra = jnp.stack(rows, axis=1).reshape(g, kk, kk)

    pr, pi = expand(ptr, t - 1 - tt, tt, -2), expand(pti, t - 1 - tt, tt, -2)
    btr_t, bti_t = jnp.tile(btr, (1, 1, t, 1)), jnp.tile(bti, (1, 1, t, 1))
    sb = jnp.stack([btr_t * pr - bti_t * pi, btr_t * pi + bti_t * pr], axis=1)
    sb = sb.reshape(2, 2, nq, 2, kk, p)
    zero = jnp.zeros_like(sb[:, :, :, 0])
    sb = jnp.stack([jnp.concatenate([sb[:, :, :, 0], zero], -1),
                    jnp.concatenate([zero, sb[:, :, :, 1]], -1)], axis=3)
    b_pair = sb.transpose(2, 3, 4, 0, 1, 5).reshape(nq, 2 * kk, 8 * p)

    cqr, cqi = c_times_power(tt + 1, t - tt)
    cq = jnp.stack([cqr, -cqi], axis=1).reshape(2, 2, nq, 2, p, kk)
    zero = jnp.zeros_like(cq[:, :, :, 0])
    cq = jnp.stack([jnp.concatenate([cq[:, :, :, 0], zero], -1),
                    jnp.concatenate([zero, cq[:, :, :, 1]], -1)], axis=3)
    c_pair = cq.transpose(2, 0, 1, 3, 4, 5).reshape(nq, 8 * p, 2 * kk)

    r8 = np.arange(SUBLANES)
    expo = np.stack([np.stack([np.full(SUBLANES, k), np.full(SUBLANES, k)]) for k in (1, 2, 4)]
                    + [np.stack([r8 + 1, SUBLANES - r8])])
    expo = jnp.asarray(t * expo, F32)[:, :, None, :, None]
    ar, ai = power(expo, are.reshape(2, nq, 1, 2 * p)[None], aim.reshape(2, nq, 1, 2 * p)[None])
    live = np.stack([np.stack([r8 >= k, r8 <= SUBLANES - 1 - k]) for k in (1, 2, 4)]
                    + [np.ones((2, SUBLANES), bool)])
    live = jnp.asarray(live, F32)[:, :, None, :, None]
    ar, ai = ar * live, ai * live
    tab = jnp.stack([ar, ai], axis=1).reshape(8, 2, nq, SUBLANES, 2 * p)
    tab = tab.transpose(2, 1, 0, 3, 4)
    return m_intra.astype(BF16), b_pair.astype(BF16), c_pair.astype(BF16), tab


def _mix_mlp_kernel(x_ref, gr_ref, gi_ref, ys_ref, bdr_ref, bdi_ref, fg_ref, wg_ref, bg_ref,
                    sg_ref, wo_ref, pmg_ref, plg_ref, w1_ref, w2_ref, pog_ref, o_ref, *, ff_blk):
    gr, gi = gr_ref[...].astype(BF16), gi_ref[...].astype(BF16)
    yf = jnp.concatenate(
        [_dot(gr[:, c:c + MXU_DIM], bdr_ref[c:c + MXU_DIM, c:c + MXU_DIM])
         + _dot(gi[:, c:c + MXU_DIM], bdi_ref[c:c + MXU_DIM, c:c + MXU_DIM])
         for c in range(0, D_FOURIER, MXU_DIM)], axis=1)
    yf = _rms(yf, fg_ref[...])
    y = _unrotate_rows(ys_ref[...])
    gl = jax.nn.gelu(y, approximate=True)
    gate = _dot(gl.astype(BF16), wg_ref[...]) + bg_ref[...]
    ys = _rms(gl * jax.nn.sigmoid(gate), sg_ref[...])
    cat = jnp.concatenate([yf.astype(BF16), ys.astype(BF16)], axis=-1)
    x1 = x_ref[...] + _rms(_dot(cat, wo_ref[...]), pmg_ref[...])
    h = _rms(x1, plg_ref[...]).astype(BF16)
    acc = jnp.zeros(x1.shape, F32)
    for j in range(D_FF // ff_blk):
        a = _dot(h, w1_ref[:, j * ff_blk:(j + 1) * ff_blk])
        a = jnp.square(jnp.maximum(a, 0.0)).astype(BF16)
        acc = acc + _dot(a, w2_ref[j * ff_blk:(j + 1) * ff_blk, :])
    o_ref[...] = x1 + _rms(acc, pog_ref[...])


def _mix_mlp(x2d, gri, ys, bdr, bdi, fg, wg, bg, sg, wo, pmg, plg, w1, w2, pog, tm, ff_blk, layer):
    n = x2d.shape[0]
    row = lambda i: (i, 0)
    lspec = functools.partial(_layer_spec, layer=layer)
    return pl.pallas_call(
        functools.partial(_mix_mlp_kernel, ff_blk=ff_blk),
        grid=(n // tm,),
        in_specs=[pl.BlockSpec((tm, D_MODEL), row),
                  pl.BlockSpec((None, tm, D_FOURIER), lambda i: (0, i, 0)),
                  pl.BlockSpec((None, tm, D_FOURIER), lambda i: (1, i, 0)),
                  pl.BlockSpec((tm, D_SSM), row),
                  lspec((D_FOURIER, D_FOURIER)), lspec((D_FOURIER, D_FOURIER)),
                  lspec((1, D_FOURIER)),
                  lspec((D_SSM, D_SSM)), lspec((1, D_SSM)), lspec((1, D_SSM)),
                  lspec((D_MODEL, D_MODEL)), lspec((1, D_MODEL)),
                  lspec((1, D_MODEL)),
                  lspec((D_MODEL, D_FF)), lspec((D_FF, D_MODEL)),
                  lspec((1, D_MODEL))],
        out_specs=pl.BlockSpec((tm, D_MODEL), row),
        out_shape=jax.ShapeDtypeStruct((n, D_MODEL), F32),
        compiler_params=pltpu.CompilerParams(
            dimension_semantics=("parallel",), vmem_limit_bytes=VMEM_LIMIT),
        name="mix_mlp",
    )(x2d, gri, gri, ys, bdr, bdi, fg, wg, bg, sg, wo, pmg, plg, w1, w2, pog)


def _fourier_channel_maps(w_f, seq):
    c = np.arange(FOURIER_HEAD_DIM)
    ang = 2.0 * np.pi * ((c[:, None] * c[None, :]) % FOURIER_HEAD_DIM) / FOURIER_HEAD_DIM
    scale = 1.0 / math.sqrt(seq * FOURIER_HEAD_DIM)
    hp = jax.lax.Precision.HIGHEST
    cw = jnp.einsum("cd,hde->hce", jnp.asarray(np.cos(ang) * scale, F32), w_f.astype(F32), precision=hp)
    sw = jnp.einsum("cd,hde->hce", jnp.asarray(np.sin(ang) * scale, F32), w_f.astype(F32), precision=hp)
    eye = jnp.eye(FOURIER_HEADS, dtype=F32)
    bd = lambda m: jnp.einsum("hce,hk->hcke", m, eye).reshape(D_FOURIER, D_FOURIER)
    return bd(cw).astype(BF16), bd(sw).astype(BF16)


def kernel(x, w_in, w_out, pre_mix_g, post_mix_g, pre_mlp_g, post_mlp_g, fourier_out_g, ssm_out_g,
           w_fourier, lam_re, lam_im, log_dt, b_re, b_im, c_re, c_im, d_skip, w_glu, b_glu,
           w_ff1, w_ff2):
    bsz, seq, _ = x.shape
    depth = w_in.shape[0]
    n = bsz * seq
    n1, n2 = FFT_N1, seq // FFT_N1
    nc = seq // CHUNK
    assert seq % (FFT_N1 * SUBLANES) == 0 and nc % SUBLANES == 0
    tm = min(512, n)
    wk_tab, m2_tab = _fft_tables(seq)

    rows = lambda v: v.reshape(depth, 1, -1).astype(F32)
    w_in_b, w_out_b, w_glu_b = w_in.astype(BF16), w_out.astype(BF16), w_glu.astype(BF16)
    w_ff1_b, w_ff2_b = w_ff1.astype(BF16), w_ff2.astype(BF16)
    m_intra, b_pair, c_pair, tab = jax.vmap(_ssm_operators)(
        lam_re, lam_im, log_dt, b_re, b_im, c_re, c_im, d_skip)
    bdr, bdi = jax.vmap(functools.partial(_fourier_channel_maps, seq=seq))(w_fourier)
    gains = [rows(v) for v in (pre_mix_g, fourier_out_g, b_glu, ssm_out_g, post_mix_g,
                               pre_mlp_g, post_mlp_g)]
    pre_mix, four_g, glu_b, ssm_g, post_mix, pre_mlp, post_mlp = gains

    x2d = x.reshape(n, D_MODEL).astype(F32)
    for i in range(depth):
        y, zs = _in_proj(x2d.reshape(bsz, n1, n2, D_MODEL), pre_mix, w_in_b, wk_tab, i)
        zs = zs.reshape(n, D_SSM)
        g = _fft_stage2(y.reshape(bsz, n1, 2 * n2, D_FOURIER), m2_tab, kb=SUBLANES)
        gri = g.reshape(2, n, D_FOURIER)
        ysr = _ssm_chunked(zs, m_intra, b_pair, c_pair, tab, bsz, seq, i)
        x2d = _mix_mlp(x2d, gri, ysr, bdr, bdi, four_g, w_glu_b, glu_b, ssm_g, w_out_b,
                       post_mix, pre_mlp, w_ff1_b, w_ff2_b, post_mlp, tm, ff_blk=1024, layer=i)
    return x2d.reshape(bsz, seq, D_MODEL).astype(x.dtype)
```

```python
import functools
import math

import jax
import jax.numpy as jnp
import numpy as np
from jax.experimental import pallas as pl
from jax.experimental.pallas import tpu as pltpu

EPS = 1e-6
D_MODEL = 1024
D_FOURIER = 512
D_SSM = 512
FOURIER_HEADS = 8
FOURIER_HEAD_DIM = 64
SSM_GROUP = 16
SSM_GROUPS = 32
SSM_STATE = 64
D_FF = 4096

FFT_N1 = 64
CHUNK = 16
LANES = 128
SUBLANES = 8
MXU_DIM = 256
VMEM_LIMIT = 56 * 1024 * 1024

BF16 = jnp.bfloat16
F32 = jnp.float32


def _rms(x, g):
    return x * jax.lax.rsqrt(jnp.mean(x * x, axis=-1, keepdims=True) + EPS) * g


def _dot(a, b):
    return jnp.dot(a, b, preferred_element_type=F32)


def _const_spec(shape):
    nd = len(shape)
    return pl.BlockSpec(shape, lambda *_: (0,) * nd, pipeline_mode=pl.Buffered(1))


def _layer_spec(shape, layer):
    nd = len(shape)
    return pl.BlockSpec((None,) + tuple(shape), lambda *_: (layer,) + (0,) * nd,
                        pipeline_mode=pl.Buffered(1))


N2_BLK = SUBLANES


def _in_proj_kernel(x_ref, g_ref, w_ref, wk_ref, y_ref, zs_ref):
    n1 = x_ref.shape[0]
    x = x_ref[...].reshape(n1 * N2_BLK, D_MODEL)
    h = _rms(x, g_ref[...]).astype(BF16)
    z = _dot(h, w_ref[...])
    y = _dot(wk_ref[...], z[:, :D_FOURIER].astype(BF16))
    y_ref[...] = y.reshape(n1, 2, N2_BLK, D_FOURIER)
    zs_ref[...] = _rotate_rows(z[:, D_FOURIER:]).reshape(n1, N2_BLK, D_SSM)


def _in_proj(x4, g, w_bf16, wk, layer):
    bsz, n1, n2, _ = x4.shape
    rows = n1 * N2_BLK
    return pl.pallas_call(
        _in_proj_kernel,
        grid=(bsz, n2 // N2_BLK),
        in_specs=[pl.BlockSpec((None, n1, N2_BLK, D_MODEL), lambda b, j: (b, 0, j, 0)),
                  _layer_spec((1, D_MODEL), layer),
                  _layer_spec((D_MODEL, D_MODEL), layer),
                  _const_spec((2 * rows, rows))],
        out_specs=[pl.BlockSpec((None, n1, 2, N2_BLK, D_FOURIER), lambda b, j: (b, 0, 0, j, 0)),
                   pl.BlockSpec((None, n1, N2_BLK, D_SSM), lambda b, j: (b, 0, j, 0))],
        out_shape=[jax.ShapeDtypeStruct((bsz, n1, 2, n2, D_FOURIER), F32),
                   jax.ShapeDtypeStruct((bsz, n1, n2, D_SSM), F32)],
        compiler_params=pltpu.CompilerParams(
            dimension_semantics=("parallel", "parallel"), vmem_limit_bytes=VMEM_LIMIT),
        name="in_proj",
    )(x4, g, w_bf16, wk)


def _fft2_kernel(m_ref, y_ref, g_ref, z_ref, *, kb, n2, pitch):
    nslab = D_FOURIER // LANES
    for j in range(kb):
        z = _dot(m_ref[j], y_ref[j].astype(BF16))
        for ri in range(2):
            for s in range(nslab):
                z_ref[ri, s, j * pitch:j * pitch + n2, :] = (
                    z[ri * n2:(ri + 1) * n2, s * LANES:(s + 1) * LANES])

    def regroup(k2, _):
        for ri in range(2):
            for s in range(nslab):
                g_ref[ri, k2, :, s * LANES:(s + 1) * LANES] = (
                    z_ref[ri, s, pl.ds(k2, kb, stride=pitch), :])
        return 0

    jax.lax.fori_loop(0, n2, regroup, 0)


def _fft_stage2(y, m2, kb):
    b, n1, rows, c = y.shape
    n2 = rows // 2
    pitch = n2 + SUBLANES
    return pl.pallas_call(
        functools.partial(_fft2_kernel, kb=kb, n2=n2, pitch=pitch),
        grid=(n1 // kb, b),
        in_specs=[pl.BlockSpec((kb, rows, rows), lambda j, i: (j, 0, 0)),
                  pl.BlockSpec((None, kb, rows, c), lambda j, i: (i, j, 0, 0))],
        out_specs=pl.BlockSpec((2, None, n2, kb, c), lambda j, i: (0, i, 0, j, 0)),
        out_shape=jax.ShapeDtypeStruct((2, b, n2, n1, c), F32),
        scratch_shapes=[pltpu.VMEM((2, c // LANES, kb * pitch, LANES), F32)],
        compiler_params=pltpu.CompilerParams(
            dimension_semantics=("parallel", "parallel"), vmem_limit_bytes=VMEM_LIMIT),
        name="fft_stage2",
    )(m2, y)


def _fft_tables(seq):
    n1, n2 = FFT_N1, seq // FFT_N1
    k1 = np.arange(n1)
    ang1 = 2.0 * np.pi * ((k1[:, None] * k1[None, :]) % n1) / n1
    w1 = np.stack([np.cos(ang1), -np.sin(ang1)], axis=1).reshape(2 * n1, n1)
    wk = np.einsum("kn,ab->kanb", w1, np.eye(N2_BLK)).reshape(2 * n1 * N2_BLK, n1 * N2_BLK)
    k2 = np.arange(n2)
    freq = k1[:, None, None] + n1 * k2[None, :, None]
    ang2 = 2.0 * np.pi * ((freq * k2[None, None, :]) % seq) / seq
    mr, mi = np.cos(ang2), -np.sin(ang2)
    m2 = np.concatenate([np.concatenate([mr, -mi], axis=2),
                         np.concatenate([mi, mr], axis=2)], axis=1)
    return jnp.asarray(wk, F32).astype(BF16), jnp.asarray(m2, F32).astype(BF16)


GROUPS_PER_BLOCK = LANES // SSM_GROUP
PAIRS_PER_BLOCK = GROUPS_PER_BLOCK // 2
RELAYOUT_CHUNKS = 4 * SUBLANES


def _rotate_rows(v):
    return jnp.concatenate(
        [pltpu.roll(v[:, c:c + LANES], 0, axis=1, stride=SSM_GROUP, stride_axis=0)
         for c in range(0, v.shape[1], LANES)], axis=1)


def _unrotate_rows(v):
    row = jax.lax.broadcasted_iota(jnp.int32, v.shape, 0)
    cols = []
    for c in range(0, v.shape[1], LANES):
        w = v[:, c:c + LANES]
        for bit in (1, 2, 4):
            w = jnp.where((row[:, :LANES] & bit) != 0,
                          pltpu.roll(w, LANES - SSM_GROUP * bit, axis=1), w)
        cols.append(w)
    return jnp.concatenate(cols, axis=1)


def _ssm_kernel(z_ref, mi_ref, bs_ref, cs_ref, tab_ref, y_ref,
                u2_ref, s_ref, car_ref, yg_ref, *, nc):
    gpb, npair = GROUPS_PER_BLOCK, PAIRS_PER_BLOCK
    t16 = RELAYOUT_CHUNKS
    rows_it = t16 * CHUNK
    lane_blk = jax.lax.broadcasted_iota(jnp.int32, (t16, LANES), 1) // SSM_GROUP
    masks = [lane_blk == m for m in range(gpb)]

    def relayout_in(i, _):
        r0 = pl.multiple_of(i * rows_it, rows_it)
        ut = [z_ref[pl.ds(r0 + t, t16, stride=CHUNK), :] for t in range(CHUNK)]
        c0 = pl.multiple_of(i * t16, t16)
        for g in range(gpb):
            for j in range(CHUNK // gpb):
                acc = ut[gpb * j]
                for tt in range(1, gpb):
                    acc = jnp.where(masks[(g + tt) % gpb], ut[gpb * j + tt], acc)
                if g:
                    acc = pltpu.roll(acc, LANES - SSM_GROUP * g, axis=1)
                u2_ref[g, pl.ds(c0, t16), j * LANES:(j + 1) * LANES] = acc.astype(BF16)
        return 0

    jax.lax.fori_loop(0, nc // t16, relayout_in, 0)

    for q in range(npair):
        up = jnp.concatenate([u2_ref[2 * q], u2_ref[2 * q + 1]], axis=1)
        s_ref[q] = _dot(up, bs_ref[q])

    row = jax.lax.broadcasted_iota(jnp.int32, (SUBLANES, LANES), 0)
    nblk = nc // SUBLANES

    def scan_block(tab, sr, si, cr, ci, down):
        keep = (lambda sh: row >= sh) if down else (lambda sh: row <= SUBLANES - 1 - sh)
        amt = (lambda sh: sh) if down else (lambda sh: SUBLANES - sh)
        for k, sh in enumerate((1, 2, 4)):
            tr = pltpu.roll(sr, amt(sh), axis=0)
            ti = pltpu.roll(si, amt(sh), axis=0)
            ar, ai = tab[2 * k], tab[2 * k + 1]
            sr, si = sr + (ar * tr - ai * ti), si + (ar * ti + ai * tr)
        fr = sr + (tab[6] * cr - tab[7] * ci)
        fi = si + (tab[6] * ci + tab[7] * cr)
        outr = jnp.where(keep(1), pltpu.roll(fr, amt(1), axis=0), cr)
        outi = jnp.where(keep(1), pltpu.roll(fi, amt(1), axis=0), ci)
        edge = SUBLANES - 1 if down else 0
        ncr = jnp.broadcast_to(fr[edge:edge + 1, :], (SUBLANES, LANES))
        nci = jnp.broadcast_to(fi[edge:edge + 1, :], (SUBLANES, LANES))
        return outr, outi, ncr, nci

    def scan_body(i, carry):
        new = []
        for q in range(npair):
            for d in range(2):
                blk = i if d == 0 else nblk - 1 - i
                r0 = pl.multiple_of(blk * SUBLANES, SUBLANES)
                lo = 2 * d * LANES
                tab = [tab_ref[q, d, k] for k in range(8)]
                cr, ci = carry[4 * q + 2 * d], carry[4 * q + 2 * d + 1]
                outr, outi, cr, ci = scan_block(
                    tab, s_ref[q, pl.ds(r0, SUBLANES), lo:lo + LANES],
                    s_ref[q, pl.ds(r0, SUBLANES), lo + LANES:lo + 2 * LANES], cr, ci, d == 0)
                car_ref[q, pl.ds(r0, SUBLANES), lo:lo + LANES] = outr
                car_ref[q, pl.ds(r0, SUBLANES), lo + LANES:lo + 2 * LANES] = outi
                new += [cr, ci]
        return tuple(new)

    zero = jnp.zeros((SUBLANES, LANES), F32)
    jax.lax.fori_loop(0, nblk, scan_body, (zero,) * (4 * npair))

    kk = CHUNK * SSM_GROUP
    for q in range(npair):
        yi = _dot(car_ref[q].astype(BF16), cs_ref[q])
        for e in range(2):
            g = 2 * q + e
            yg_ref[g] = _dot(u2_ref[g], mi_ref[g]) + yi[:, e * kk:(e + 1) * kk]

    def relayout_out(i, _):
        c0 = pl.multiple_of(i * t16, t16)
        r0 = pl.multiple_of(i * rows_it, rows_it)
        yrot = []
        for g in range(gpb):
            halves = []
            for j in range(CHUNK // gpb):
                v = yg_ref[g, pl.ds(c0, t16), j * LANES:(j + 1) * LANES]
                halves.append(pltpu.roll(v, SSM_GROUP * g, axis=1) if g else v)
            yrot.append(halves)
        for t in range(CHUNK):
            j, tt = divmod(t, gpb)
            acc = yrot[0][j]
            for g in range(1, gpb):
                acc = jnp.where(masks[(g + tt) % gpb], yrot[g][j], acc)
            y_ref[pl.ds(r0 + t, t16, stride=CHUNK), :] = acc
        return 0

    jax.lax.fori_loop(0, nc // t16, relayout_out, 0)


def _ssm_chunked(zs, m_intra, b_pair, c_pair, tab, bsz, seq, layer):
    nc = seq // CHUNK
    gpb, npair = GROUPS_PER_BLOCK, PAIRS_PER_BLOCK
    k = CHUNK * SSM_GROUP
    return pl.pallas_call(
        functools.partial(_ssm_kernel, nc=nc),
        grid=(D_SSM // LANES, bsz),
        in_specs=[pl.BlockSpec((seq, LANES), lambda j, b: (b, j)),
                  pl.BlockSpec((None, gpb, k, k), lambda j, b: (layer, j, 0, 0)),
                  pl.BlockSpec((None, npair, 2 * k, 2 * k), lambda j, b: (layer, j, 0, 0)),
                  pl.BlockSpec((None, npair, 2 * k, 2 * k), lambda j, b: (layer, j, 0, 0)),
                  pl.BlockSpec((None, npair, 2, 8, SUBLANES, LANES),
                               lambda j, b: (layer, j, 0, 0, 0, 0))],
        out_specs=pl.BlockSpec((seq, LANES), lambda j, b: (b, j)),
        out_shape=jax.ShapeDtypeStruct((bsz * seq, D_SSM), F32),
        scratch_shapes=[pltpu.VMEM((gpb, nc, k), BF16),
                        pltpu.VMEM((npair, nc, 2 * k), F32),
                        pltpu.VMEM((npair, nc, 2 * k), F32),
                        pltpu.VMEM((gpb, nc, k), F32)],
        compiler_params=pltpu.CompilerParams(
            dimension_semantics=("parallel", "parallel"), vmem_limit_bytes=VMEM_LIMIT),
        name="ssm_chunk",
    )(zs, m_intra, b_pair, c_pair, tab)


def _ssm_tables(lam_re, lam_im, log_dt, b_re, b_im, c_re, c_im, d_skip):
    t, h, p, g = CHUNK, SSM_GROUP, SSM_STATE, SSM_GROUPS
    nq = g // 2
    dt = jnp.exp(log_dt.astype(F32))[..., None]
    are = lam_re.astype(F32) * dt
    aim = lam_im.astype(F32) * dt

    def power(expo, a_re, a_im):
        mag = jnp.exp(a_re * expo)
        return mag * jnp.cos(a_im * expo), mag * jnp.sin(a_im * expo)

    lbr, lbi = power(1.0, are, aim)
    den = lam_re * lam_re + lam_im * lam_im
    cr = ((lbr - 1.0) * lam_re + lbi * lam_im) / den
    ci = (lbi * lam_re - (lbr - 1.0) * lam_im) / den
    bbr = cr[..., None] * b_re - ci[..., None] * b_im
    bbi = cr[..., None] * b_im + ci[..., None] * b_re

    def pair_rows(x):
        return x.reshape(2, 2, nq, 2 * p, x.shape[-1]).transpose(2, 0, 1, 3, 4)

    lane_pad = lambda x: jnp.pad(x, [(0, 0)] * (x.ndim - 1) + [(0, LANES - x.shape[-1])])
    steps = jnp.arange(t + 1, dtype=F32)
    pw = jnp.stack(power(steps, are[..., None], aim[..., None]))
    pl_tab = lane_pad(pair_rows(pw))
    pt_tab = jnp.pad(jnp.swapaxes(pair_rows(pw), -1, -2),
                     [(0, 0)] * 3 + [(0, LANES - t - 1), (0, 0)])
    ct_tab = lane_pad(pair_rows(jnp.stack([jnp.swapaxes(c_re.astype(F32), -1, -2),
                                           jnp.swapaxes(c_im.astype(F32), -1, -2)])))
    bt = jnp.stack([jnp.swapaxes(bbr, -1, -2), jnp.swapaxes(bbi, -1, -2)])
    bt = bt.reshape(2, 2, nq, 2, h, p)
    zero = jnp.zeros_like(bt[:, :, :, 0])
    bt_tab = jnp.stack([jnp.concatenate([bt[:, :, :, 0], zero], -1),
                        jnp.concatenate([zero, bt[:, :, :, 1]], -1)], axis=3)
    bt_tab = bt_tab.transpose(2, 0, 1, 3, 4, 5)
    dsk = jnp.pad(d_skip.astype(F32), ((0, 0), (0, t * h - h))).reshape(nq, 2, 1, t * h)

    r8 = np.arange(SUBLANES)
    expo = np.stack([np.stack([np.full(SUBLANES, k), np.full(SUBLANES, k)]) for k in (1, 2, 4)]
                    + [np.stack([r8 + 1, SUBLANES - r8])])
    expo = jnp.asarray(t * expo, F32)[:, :, None, :, None]
    ar, ai = power(expo, are.reshape(2, nq, 1, 2 * p)[None], aim.reshape(2, nq, 1, 2 * p)[None])
    live = np.stack([np.stack([r8 >= k, r8 <= SUBLANES - 1 - k]) for k in (1, 2, 4)]
                    + [np.ones((2, SUBLANES), bool)])
    live = jnp.asarray(live, F32)[:, :, None, :, None]
    ar, ai = ar * live, ai * live
    tab = jnp.stack([ar, ai], axis=1).reshape(8, 2, nq, SUBLANES, 2 * p)
    tab = tab.transpose(2, 1, 0, 3, 4)
    return pl_tab, pt_tab, ct_tab, bt_tab, dsk, tab


def _select_lanes(x, sel):
    hi = x.astype(BF16)
    lo = (x - hi.astype(F32)).astype(BF16)
    return _dot(hi, sel) + _dot(lo, sel)


def _select_rows(sel, x):
    hi = x.astype(BF16)
    lo = (x - hi.astype(F32)).astype(BF16)
    return _dot(sel, hi) + _dot(sel, lo)


def _roll_two_vregs(x, r):
    x0, x1 = x[:, :LANES], x[:, LANES:]
    if r >= LANES:
        x0, x1, r = x1, x0, r - LANES
    if r == 0:
        return jnp.concatenate([x0, x1], axis=1)
    lane = jax.lax.broadcasted_iota(jnp.int32, x0.shape, 1)
    a0, a1 = pltpu.roll(x0, r, axis=1), pltpu.roll(x1, r, axis=1)
    return jnp.concatenate([jnp.where(lane < r, a1, a0), jnp.where(lane < r, a0, a1)], axis=1)


def _ssm_prep_kernel(pl_ref, pt_ref, ct_ref, bt_ref, dsk_ref, sl_ref, sr_ref, e_ref,
                     m_ref, b_ref, c_ref):
    t, h = CHUNK, SSM_GROUP
    kk = t * h
    lane = jax.lax.broadcasted_iota(jnp.int32, (h, kk), 1)
    rowi = jax.lax.broadcasted_iota(jnp.int32, (h, kk), 0)
    half = jax.lax.broadcasted_iota(jnp.int32, (LANES, kk), 0) // SSM_STATE

    cl, cq = [], []
    for d in range(2):
        ctr, cti = _select_lanes(ct_ref[0, d], e_ref[...]), _select_lanes(ct_ref[1, d], e_ref[...])
        for out, pat in ((cl, d), (cq, 2 + d)):
            pr, pi = _select_lanes(pl_ref[0, d], sl_ref[pat]), _select_lanes(pl_ref[1, d], sl_ref[pat])
            out.append((ctr * pr - cti * pi, ctr * pi + cti * pr))

    for d in range(2):
        for ri, val in enumerate((cq[d][0], -cq[d][1])):
            r0 = (2 * d + ri) * LANES
            for gi in range(2):
                c_ref[r0:r0 + LANES, gi * kk:(gi + 1) * kk] = (
                    jnp.where(half == gi, val, 0.0).astype(BF16))

    for d in range(2):
        pwr, pwi = _select_rows(sr_ref[d], pt_ref[0, d]), _select_rows(sr_ref[d], pt_ref[1, d])
        for gi in range(2):
            btr = jnp.concatenate([bt_ref[0, d, gi]] * t, axis=0)
            bti = jnp.concatenate([bt_ref[1, d, gi]] * t, axis=0)
            r0, c0 = gi * kk, 2 * d * LANES
            b_ref[r0:r0 + kk, c0:c0 + LANES] = (btr * pwr - bti * pwi).astype(BF16)
            b_ref[r0:r0 + kk, c0 + LANES:c0 + 2 * LANES] = (btr * pwi + bti * pwr).astype(BF16)

    for gi in range(2):
        gen = [_dot(bt_ref[0, d, gi].astype(BF16), cl[d][0].astype(BF16))
               - _dot(bt_ref[1, d, gi].astype(BF16), cl[d][1].astype(BF16)) for d in range(2)]
        diag = jnp.where(lane < h, gen[1], 0.0) + jnp.where(rowi == lane, dsk_ref[gi], 0.0)
        kf, kb = gen[0] + diag, gen[1]
        for s in range(t):
            blk = jnp.where(lane >= h * s, _roll_two_vregs(kf, h * s), _roll_two_vregs(kb, h * s))
            m_ref[gi, s * h:(s + 1) * h, :] = blk.astype(BF16)


def _ssm_prep_constants():
    t, h = CHUNK, SSM_GROUP
    kk = t * h
    step = np.arange(kk) // h
    k = np.arange(LANES)
    pats = [step, (t - step) % t, step + 1, t - step]
    sel_lane = np.stack([(k[:, None] == pat[None, :]) for pat in pats]).astype(np.float32)
    sel_row = np.stack([(pat[:, None] == k[None, :]) for pat in (t - 1 - step, step)])
    tile_ho = (k[:, None] == (np.arange(kk) % h)[None, :])
    as_bf16 = lambda a: jnp.asarray(a, F32).astype(BF16)
    return as_bf16(sel_lane), as_bf16(sel_row), as_bf16(tile_ho)


def _ssm_prep(pl_tab, pt_tab, ct_tab, bt_tab, dsk):
    depth, nq = pl_tab.shape[:2]
    kk = CHUNK * SSM_GROUP
    sel_lane, sel_row, tile_ho = _ssm_prep_constants()
    tab5 = pl.BlockSpec((None, None, 2, 2, LANES, LANES), lambda l, q: (l, q, 0, 0, 0, 0))
    return pl.pallas_call(
        _ssm_prep_kernel,
        grid=(depth, nq),
        in_specs=[tab5, tab5, tab5,
                  pl.BlockSpec((None, None, 2, 2, 2, SSM_GROUP, LANES),
                               lambda l, q: (l, q, 0, 0, 0, 0, 0)),
                  pl.BlockSpec((None, None, 2, 1, kk), lambda l, q: (l, q, 0, 0, 0)),
                  _const_spec(sel_lane.shape), _const_spec(sel_row.shape),
                  _const_spec(tile_ho.shape)],
        out_specs=[pl.BlockSpec((None, 2, kk, kk), lambda l, q: (l, q, 0, 0)),
                   pl.BlockSpec((None, None, 2 * kk, 2 * kk), lambda l, q: (l, q, 0, 0)),
                   pl.BlockSpec((None, None, 2 * kk, 2 * kk), lambda l, q: (l, q, 0, 0))],
        out_shape=[jax.ShapeDtypeStruct((depth, 2 * nq, kk, kk), BF16),
                   jax.ShapeDtypeStruct((depth, nq, 2 * kk, 2 * kk), BF16),
                   jax.ShapeDtypeStruct((depth, nq, 2 * kk, 2 * kk), BF16)],
        compiler_params=pltpu.CompilerParams(
            dimension_semantics=("parallel", "parallel"), vmem_limit_bytes=VMEM_LIMIT),
        name="ssm_prep",
    )(pl_tab, pt_tab, ct_tab, bt_tab, dsk, sel_lane, sel_row, tile_ho)


def _mix_mlp_kernel(x_ref, gr_ref, gi_ref, ys_ref, bdr_ref, bdi_ref, fg_ref, wg_ref, bg_ref,
                    sg_ref, wo_ref, pmg_ref, plg_ref, w1_ref, w2_ref, pog_ref, o_ref, *, ff_blk):
    gr, gi = gr_ref[...].astype(BF16), gi_ref[...].astype(BF16)
    yf = jnp.concatenate(
        [_dot(gr[:, c:c + MXU_DIM], bdr_ref[c:c + MXU_DIM, c:c + MXU_DIM])
         + _dot(gi[:, c:c + MXU_DIM], bdi_ref[c:c + MXU_DIM, c:c + MXU_DIM])
         for c in range(0, D_FOURIER, MXU_DIM)], axis=1)
    yf = _rms(yf, fg_ref[...])
    y = _unrotate_rows(ys_ref[...])
    gl = jax.nn.gelu(y, approximate=True)
    gate = _dot(gl.astype(BF16), wg_ref[...]) + bg_ref[...]
    ys = _rms(gl * jax.nn.sigmoid(gate), sg_ref[...])
    cat = jnp.concatenate([yf.astype(BF16), ys.astype(BF16)], axis=-1)
    x1 = x_ref[...] + _rms(_dot(cat, wo_ref[...]), pmg_ref[...])
    h = _rms(x1, plg_ref[...]).astype(BF16)
    acc = jnp.zeros(x1.shape, F32)
    for j in range(D_FF // ff_blk):
        a = _dot(h, w1_ref[:, j * ff_blk:(j + 1) * ff_blk])
        a = jnp.square(jnp.maximum(a, 0.0)).astype(BF16)
        acc = acc + _dot(a, w2_ref[j * ff_blk:(j + 1) * ff_blk, :])
    o_ref[...] = x1 + _rms(acc, pog_ref[...])


def _mix_mlp(x2d, gri, ys, bdr, bdi, fg, wg, bg, sg, wo, pmg, plg, w1, w2, pog, tm, ff_blk, layer):
    n = x2d.shape[0]
    row = lambda i: (i, 0)
    lspec = functools.partial(_layer_spec, layer=layer)
    return pl.pallas_call(
        functools.partial(_mix_mlp_kernel, ff_blk=ff_blk),
        grid=(n // tm,),
        in_specs=[pl.BlockSpec((tm, D_MODEL), row),
                  pl.BlockSpec((None, tm, D_FOURIER), lambda i: (0, i, 0)),
                  pl.BlockSpec((None, tm, D_FOURIER), lambda i: (1, i, 0)),
                  pl.BlockSpec((tm, D_SSM), row),
                  lspec((D_FOURIER, D_FOURIER)), lspec((D_FOURIER, D_FOURIER)),
                  lspec((1, D_FOURIER)),
                  lspec((D_SSM, D_SSM)), lspec((1, D_SSM)), lspec((1, D_SSM)),
                  lspec((D_MODEL, D_MODEL)), lspec((1, D_MODEL)),
                  lspec((1, D_MODEL)),
                  lspec((D_MODEL, D_FF)), lspec((D_FF, D_MODEL)),
                  lspec((1, D_MODEL))],
        out_specs=pl.BlockSpec((tm, D_MODEL), row),
        out_shape=jax.ShapeDtypeStruct((n, D_MODEL), F32),
        compiler_params=pltpu.CompilerParams(
            dimension_semantics=("parallel",), vmem_limit_bytes=VMEM_LIMIT),
        name="mix_mlp",
    )(x2d, gri, gri, ys, bdr, bdi, fg, wg, bg, sg, wo, pmg, plg, w1, w2, pog)


def _fourier_channel_maps(w_f, seq):
    c = np.arange(FOURIER_HEAD_DIM)
    ang = 2.0 * np.pi * ((c[:, None] * c[None, :]) % FOURIER_HEAD_DIM) / FOURIER_HEAD_DIM
    scale = 1.0 / math.sqrt(seq * FOURIER_HEAD_DIM)
    hp = jax.lax.Precision.HIGHEST
    cw = jnp.einsum("cd,hde->hce", jnp.asarray(np.cos(ang) * scale, F32), w_f.astype(F32), precision=hp)
    sw = jnp.einsum("cd,hde->hce", jnp.asarray(np.sin(ang) * scale, F32), w_f.astype(F32), precision=hp)
    eye = jnp.eye(FOURIER_HEADS, dtype=F32)
    bd = lambda m: jnp.einsum("hce,hk->hcke", m, eye).reshape(D_FOURIER, D_FOURIER)
    return bd(cw).astype(BF16), bd(sw).astype(BF16)


def kernel(x, w_in, w_out, pre_mix_g, post_mix_g, pre_mlp_g, post_mlp_g, fourier_out_g, ssm_out_g,
           w_fourier, lam_re, lam_im, log_dt, b_re, b_im, c_re, c_im, d_skip, w_glu, b_glu,
           w_ff1, w_ff2):
    bsz, seq, _ = x.shape
    depth = w_in.shape[0]
    n = bsz * seq
    n1, n2 = FFT_N1, seq // FFT_N1
    nc = seq // CHUNK
    assert seq % (FFT_N1 * SUBLANES) == 0 and nc % SUBLANES == 0
    tm = min(512, n)
    wk_tab, m2_tab = _fft_tables(seq)

    rows = lambda v: v.reshape(depth, 1, -1).astype(F32)
    w_in_b, w_out_b, w_glu_b = w_in.astype(BF16), w_out.astype(BF16), w_glu.astype(BF16)
    w_ff1_b, w_ff2_b = w_ff1.astype(BF16), w_ff2.astype(BF16)
    pl_tab, pt_tab, ct_tab, bt_tab, dsk, tab = jax.vmap(_ssm_tables)(
        lam_re, lam_im, log_dt, b_re, b_im, c_re, c_im, d_skip)
    m_intra, b_pair, c_pair = _ssm_prep(pl_tab, pt_tab, ct_tab, bt_tab, dsk)
    bdr, bdi = jax.vmap(functools.partial(_fourier_channel_maps, seq=seq))(w_fourier)
    gains = [rows(v) for v in (pre_mix_g, fourier_out_g, b_glu, ssm_out_g, post_mix_g,
                               pre_mlp_g, post_mlp_g)]
    pre_mix, four_g, glu_b, ssm_g, post_mix, pre_mlp, post_mlp = gains

    x2d = x.reshape(n, D_MODEL).astype(F32)
    for i in range(depth):
        y, zs = _in_proj(x2d.reshape(bsz, n1, n2, D_MODEL), pre_mix, w_in_b, wk_tab, i)
        zs = zs.reshape(n, D_SSM)
        g = _fft_stage2(y.reshape(bsz, n1, 2 * n2, D_FOURIER), m2_tab, kb=SUBLANES)
        gri = g.reshape(2, n, D_FOURIER)
        ysr = _ssm_chunked(zs, m_intra, b_pair, c_pair, tab, bsz, seq, i)
        x2d = _mix_mlp(x2d, gri, ysr, bdr, bdi, four_g, w_glu_b, glu_b, ssm_g, w_out_b,
                       post_mix, pre_mlp, w_ff1_b, w_ff2_b, post_mlp, tm, ff_blk=1024, layer=i)
    return x2d.reshape(bsz, seq, D_MODEL).astype(x.dtype)
```

```python
import functools
import math

import jax
import jax.numpy as jnp
import numpy as np
from jax.experimental import pallas as pl
from jax.experimental.pallas import tpu as pltpu

EPS = 1e-6
D_MODEL = 1024
D_FOURIER = 512
D_SSM = 512
FOURIER_HEADS = 8
FOURIER_HEAD_DIM = 64
SSM_GROUP = 16
SSM_GROUPS = 32
SSM_STATE = 64
D_FF = 4096

FFT_N1 = 64
CHUNK = 16
LANES = 128
SUBLANES = 8
MXU_DIM = 256
VMEM_LIMIT = 56 * 1024 * 1024

BF16 = jnp.bfloat16
F32 = jnp.float32


def _rms(x, g):
    return x * jax.lax.rsqrt(jnp.mean(x * x, axis=-1, keepdims=True) + EPS) * g


def _dot(a, b):
    return jnp.dot(a, b, preferred_element_type=F32)


def _const_spec(shape):
    nd = len(shape)
    return pl.BlockSpec(shape, lambda *_: (0,) * nd, pipeline_mode=pl.Buffered(1))


def _layer_spec(shape, layer):
    nd = len(shape)
    return pl.BlockSpec((None,) + tuple(shape), lambda *_: (layer,) + (0,) * nd,
                        pipeline_mode=pl.Buffered(1))


KRON_BLK = SUBLANES
N2_BLK = 2 * KRON_BLK


def _in_proj_kernel(x_ref, g_ref, w_ref, wk_ref, y_ref, zs_ref):
    n1 = x_ref.shape[0]
    x = x_ref[...].reshape(n1 * N2_BLK, D_MODEL)
    h = _rms(x, g_ref[...]).astype(BF16)
    z = _dot(h, w_ref[...])
    zf = z[:, :D_FOURIER].reshape(n1, N2_BLK // KRON_BLK, KRON_BLK, D_FOURIER)
    parts = []
    for a in range(N2_BLK // KRON_BLK):
        za = zf[:, a].reshape(n1 * KRON_BLK, D_FOURIER).astype(BF16)
        ya = _dot(wk_ref[...], za)
        parts.append(ya.reshape(2 * n1, KRON_BLK, D_FOURIER))
    y = jnp.concatenate(parts, axis=1)
    y_ref[...] = y.astype(BF16).reshape(n1, 2, N2_BLK, D_FOURIER)
    zs_ref[...] = _rotate_rows(z[:, D_FOURIER:]).reshape(n1, N2_BLK, D_SSM)


def _in_proj(x4, g, w_bf16, wk, layer):
    bsz, n1, n2, _ = x4.shape
    rows = n1 * KRON_BLK
    return pl.pallas_call(
        _in_proj_kernel,
        grid=(bsz, n2 // N2_BLK),
        in_specs=[pl.BlockSpec((None, n1, N2_BLK, D_MODEL), lambda b, j: (b, 0, j, 0)),
                  _layer_spec((1, D_MODEL), layer),
                  _layer_spec((D_MODEL, D_MODEL), layer),
                  _const_spec((2 * rows, rows))],
        out_specs=[pl.BlockSpec((None, n1, 2, N2_BLK, D_FOURIER), lambda b, j: (b, 0, 0, j, 0)),
                   pl.BlockSpec((None, n1, N2_BLK, D_SSM), lambda b, j: (b, 0, j, 0))],
        out_shape=[jax.ShapeDtypeStruct((bsz, n1, 2, n2, D_FOURIER), BF16),
                   jax.ShapeDtypeStruct((bsz, n1, n2, D_SSM), F32)],
        compiler_params=pltpu.CompilerParams(
            dimension_semantics=("parallel", "parallel"), vmem_limit_bytes=VMEM_LIMIT),
        name="in_proj",
    )(x4, g, w_bf16, wk)


def _fft2_kernel(m_ref, y_ref, g_ref, z_ref, *, kb, n2, pitch):
    nslab = D_FOURIER // LANES
    for j in range(kb):
        z = _dot(m_ref[j], y_ref[j])
        for ri in range(2):
            for s in range(nslab):
                z_ref[ri, s, j * pitch:j * pitch + n2, :] = (
                    z[ri * n2:(ri + 1) * n2, s * LANES:(s + 1) * LANES])

    def regroup(k2, _):
        for ri in range(2):
            for s in range(nslab):
                rows = [z_ref[ri, s, pl.ds(k2 + a * SUBLANES * pitch, SUBLANES, stride=pitch), :]
                        for a in range(kb // SUBLANES)]
                g_ref[ri, k2, :, s * LANES:(s + 1) * LANES] = (
                    jnp.concatenate(rows, axis=0).astype(BF16))
        return 0

    jax.lax.fori_loop(0, n2, regroup, 0)


def _fft_stage2(y, m2, kb):
    b, n1, rows, c = y.shape
    n2 = rows // 2
    pitch = n2 + SUBLANES
    return pl.pallas_call(
        functools.partial(_fft2_kernel, kb=kb, n2=n2, pitch=pitch),
        grid=(n1 // kb, b),
        in_specs=[pl.BlockSpec((kb, rows, rows), lambda j, i: (j, 0, 0)),
                  pl.BlockSpec((None, kb, rows, c), lambda j, i: (i, j, 0, 0))],
        out_specs=pl.BlockSpec((2, None, n2, kb, c), lambda j, i: (0, i, 0, j, 0)),
        out_shape=jax.ShapeDtypeStruct((2, b, n2, n1, c), BF16),
        scratch_shapes=[pltpu.VMEM((2, c // LANES, kb * pitch, LANES), F32)],
        compiler_params=pltpu.CompilerParams(
            dimension_semantics=("parallel", "parallel"), vmem_limit_bytes=VMEM_LIMIT),
        name="fft_stage2",
    )(m2, y)


def _fft_tables(seq):
    n1, n2 = FFT_N1, seq // FFT_N1
    k1 = np.arange(n1)
    ang1 = 2.0 * np.pi * ((k1[:, None] * k1[None, :]) % n1) / n1
    w1 = np.stack([np.cos(ang1), -np.sin(ang1)], axis=1).reshape(2 * n1, n1)
    wk = np.einsum("kn,ab->kanb", w1, np.eye(KRON_BLK)).reshape(2 * n1 * KRON_BLK, n1 * KRON_BLK)
    k2 = np.arange(n2)
    freq = k1[:, None, None] + n1 * k2[None, :, None]
    ang2 = 2.0 * np.pi * ((freq * k2[None, None, :]) % seq) / seq
    mr, mi = np.cos(ang2), -np.sin(ang2)
    m2 = np.concatenate([np.concatenate([mr, -mi], axis=2),
                         np.concatenate([mi, mr], axis=2)], axis=1)
    return jnp.asarray(wk, F32).astype(BF16), jnp.asarray(m2, F32).astype(BF16)


GROUPS_PER_BLOCK = LANES // SSM_GROUP
PAIRS_PER_BLOCK = GROUPS_PER_BLOCK // 2
RELAYOUT_CHUNKS = 4 * SUBLANES


def _rotate_rows(v):
    return jnp.concatenate(
        [pltpu.roll(v[:, c:c + LANES], 0, axis=1, stride=SSM_GROUP, stride_axis=0)
         for c in range(0, v.shape[1], LANES)], axis=1)


def _unrotate_rows(v):
    row = jax.lax.broadcasted_iota(jnp.int32, v.shape, 0)
    cols = []
    for c in range(0, v.shape[1], LANES):
        w = v[:, c:c + LANES]
        for bit in (1, 2, 4):
            w = jnp.where((row[:, :LANES] & bit) != 0,
                          pltpu.roll(w, LANES - SSM_GROUP * bit, axis=1), w)
        cols.append(w)
    return jnp.concatenate(cols, axis=1)


def _ssm_kernel(z_ref, mi_ref, bs_ref, cs_ref, tab_ref, y_ref,
                u2_ref, s_ref, car_ref, yg_ref, *, nc):
    gpb, npair = GROUPS_PER_BLOCK, PAIRS_PER_BLOCK
    t16 = RELAYOUT_CHUNKS
    rows_it = t16 * CHUNK
    lane_blk = jax.lax.broadcasted_iota(jnp.int32, (t16, LANES), 1) // SSM_GROUP
    masks = [lane_blk == m for m in range(gpb)]

    def relayout_in(i, _):
        r0 = pl.multiple_of(i * rows_it, rows_it)
        ut = [z_ref[pl.ds(r0 + t, t16, stride=CHUNK), :] for t in range(CHUNK)]
        c0 = pl.multiple_of(i * t16, t16)
        for g in range(gpb):
            for j in range(CHUNK // gpb):
                acc = ut[gpb * j]
                for tt in range(1, gpb):
                    acc = jnp.where(masks[(g + tt) % gpb], ut[gpb * j + tt], acc)
                if g:
                    acc = pltpu.roll(acc, LANES - SSM_GROUP * g, axis=1)
                u2_ref[g, pl.ds(c0, t16), j * LANES:(j + 1) * LANES] = acc.astype(BF16)
        return 0

    jax.lax.fori_loop(0, nc // t16, relayout_in, 0)

    for q in range(npair):
        up = jnp.concatenate([u2_ref[2 * q], u2_ref[2 * q + 1]], axis=1)
        s_ref[q] = _dot(up, bs_ref[q])

    row = jax.lax.broadcasted_iota(jnp.int32, (SUBLANES, LANES), 0)
    nblk = nc // SUBLANES

    def scan_block(tab, sr, si, cr, ci, down):
        keep = (lambda sh: row >= sh) if down else (lambda sh: row <= SUBLANES - 1 - sh)
        amt = (lambda sh: sh) if down else (lambda sh: SUBLANES - sh)
        for k, sh in enumerate((1, 2, 4)):
            tr = pltpu.roll(sr, amt(sh), axis=0)
            ti = pltpu.roll(si, amt(sh), axis=0)
            ar, ai = tab[2 * k], tab[2 * k + 1]
            sr, si = sr + (ar * tr - ai * ti), si + (ar * ti + ai * tr)
        fr = sr + (tab[6] * cr - tab[7] * ci)
        fi = si + (tab[6] * ci + tab[7] * cr)
        outr = jnp.where(keep(1), pltpu.roll(fr, amt(1), axis=0), cr)
        outi = jnp.where(keep(1), pltpu.roll(fi, amt(1), axis=0), ci)
        edge = SUBLANES - 1 if down else 0
        ncr = jnp.broadcast_to(fr[edge:edge + 1, :], (SUBLANES, LANES))
        nci = jnp.broadcast_to(fi[edge:edge + 1, :], (SUBLANES, LANES))
        return outr, outi, ncr, nci

    def scan_body(i, carry):
        new = []
        for q in range(npair):
            for d in range(2):
                blk = i if d == 0 else nblk - 1 - i
                r0 = pl.multiple_of(blk * SUBLANES, SUBLANES)
                lo = 2 * d * LANES
                tab = [tab_ref[q, d, k] for k in range(8)]
                cr, ci = carry[4 * q + 2 * d], carry[4 * q + 2 * d + 1]
                outr, outi, cr, ci = scan_block(
                    tab, s_ref[q, pl.ds(r0, SUBLANES), lo:lo + LANES],
                    s_ref[q, pl.ds(r0, SUBLANES), lo + LANES:lo + 2 * LANES], cr, ci, d == 0)
                car_ref[q, pl.ds(r0, SUBLANES), lo:lo + LANES] = outr
                car_ref[q, pl.ds(r0, SUBLANES), lo + LANES:lo + 2 * LANES] = outi
                new += [cr, ci]
        return tuple(new)

    zero = jnp.zeros((SUBLANES, LANES), F32)
    jax.lax.fori_loop(0, nblk, scan_body, (zero,) * (4 * npair))

    kk = CHUNK * SSM_GROUP
    for q in range(npair):
        yi = _dot(car_ref[q].astype(BF16), cs_ref[q])
        for e in range(2):
            g = 2 * q + e
            yg_ref[g] = _dot(u2_ref[g], mi_ref[g]) + yi[:, e * kk:(e + 1) * kk]

    def relayout_out(i, _):
        c0 = pl.multiple_of(i * t16, t16)
        r0 = pl.multiple_of(i * rows_it, rows_it)
        yrot = []
        for g in range(gpb):
            halves = []
            for j in range(CHUNK // gpb):
                v = yg_ref[g, pl.ds(c0, t16), j * LANES:(j + 1) * LANES]
                halves.append(pltpu.roll(v, SSM_GROUP * g, axis=1) if g else v)
            yrot.append(halves)
        for t in range(CHUNK):
            j, tt = divmod(t, gpb)
            acc = yrot[0][j]
            for g in range(1, gpb):
                acc = jnp.where(masks[(g + tt) % gpb], yrot[g][j], acc)
            y_ref[pl.ds(r0 + t, t16, stride=CHUNK), :] = acc
        return 0

    jax.lax.fori_loop(0, nc // t16, relayout_out, 0)


def _ssm_chunked(zs, m_intra, b_pair, c_pair, tab, bsz, seq, layer):
    nc = seq // CHUNK
    gpb, npair = GROUPS_PER_BLOCK, PAIRS_PER_BLOCK
    k = CHUNK * SSM_GROUP
    return pl.pallas_call(
        functools.partial(_ssm_kernel, nc=nc),
        grid=(D_SSM // LANES, bsz),
        in_specs=[pl.BlockSpec((seq, LANES), lambda j, b: (b, j)),
                  pl.BlockSpec((None, gpb, k, k), lambda j, b: (layer, j, 0, 0)),
                  pl.BlockSpec((None, npair, 2 * k, 2 * k), lambda j, b: (layer, j, 0, 0)),
                  pl.BlockSpec((None, npair, 2 * k, 2 * k), lambda j, b: (layer, j, 0, 0)),
                  pl.BlockSpec((None, npair, 2, 8, SUBLANES, LANES),
                               lambda j, b: (layer, j, 0, 0, 0, 0))],
        out_specs=pl.BlockSpec((seq, LANES), lambda j, b: (b, j)),
        out_shape=jax.ShapeDtypeStruct((bsz * seq, D_SSM), F32),
        scratch_shapes=[pltpu.VMEM((gpb, nc, k), BF16),
                        pltpu.VMEM((npair, nc, 2 * k), F32),
                        pltpu.VMEM((npair, nc, 2 * k), F32),
                        pltpu.VMEM((gpb, nc, k), F32)],
        compiler_params=pltpu.CompilerParams(
            dimension_semantics=("parallel", "parallel"), vmem_limit_bytes=VMEM_LIMIT),
        name="ssm_chunk",
    )(zs, m_intra, b_pair, c_pair, tab)


def _ssm_tables(lam_re, lam_im, log_dt, b_re, b_im, c_re, c_im, d_skip):
    t, h, p, g = CHUNK, SSM_GROUP, SSM_STATE, SSM_GROUPS
    nq = g // 2
    dt = jnp.exp(log_dt.astype(F32))[..., None]
    are = lam_re.astype(F32) * dt
    aim = lam_im.astype(F32) * dt

    def power(expo, a_re, a_im):
        mag = jnp.exp(a_re * expo)
        return mag * jnp.cos(a_im * expo), mag * jnp.sin(a_im * expo)

    lbr, lbi = power(1.0, are, aim)
    den = lam_re * lam_re + lam_im * lam_im
    cr = ((lbr - 1.0) * lam_re + lbi * lam_im) / den
    ci = (lbi * lam_re - (lbr - 1.0) * lam_im) / den
    bbr = cr[..., None] * b_re - ci[..., None] * b_im
    bbi = cr[..., None] * b_im + ci[..., None] * b_re

    def pair_rows(x):
        return x.reshape(2, 2, nq, 2 * p, x.shape[-1]).transpose(2, 0, 1, 3, 4)

    lane_pad = lambda x: jnp.pad(x, [(0, 0)] * (x.ndim - 1) + [(0, LANES - x.shape[-1])])
    steps = jnp.arange(t + 1, dtype=F32)
    pw = jnp.stack(power(steps, are[..., None], aim[..., None]))
    pl_tab = lane_pad(pair_rows(pw))
    pt_tab = jnp.pad(jnp.swapaxes(pair_rows(pw), -1, -2),
                     [(0, 0)] * 3 + [(0, LANES - t - 1), (0, 0)])
    ct_tab = lane_pad(pair_rows(jnp.stack([jnp.swapaxes(c_re.astype(F32), -1, -2),
                                           jnp.swapaxes(c_im.astype(F32), -1, -2)])))
    bt = jnp.stack([jnp.swapaxes(bbr, -1, -2), jnp.swapaxes(bbi, -1, -2)])
    bt = bt.reshape(2, 2, nq, 2, h, p)
    zero = jnp.zeros_like(bt[:, :, :, 0])
    bt_tab = jnp.stack([jnp.concatenate([bt[:, :, :, 0], zero], -1),
                        jnp.concatenate([zero, bt[:, :, :, 1]], -1)], axis=3)
    bt_tab = bt_tab.transpose(2, 0, 1, 3, 4, 5)
    dsk = jnp.pad(d_skip.astype(F32), ((0, 0), (0, t * h - h))).reshape(nq, 2, 1, t * h)

    r8 = np.arange(SUBLANES)
    expo = np.stack([np.stack([np.full(SUBLANES, k), np.full(SUBLANES, k)]) for k in (1, 2, 4)]
                    + [np.stack([r8 + 1, SUBLANES - r8])])
    expo = jnp.asarray(t * expo, F32)[:, :, None, :, None]
    ar, ai = power(expo, are.reshape(2, nq, 1, 2 * p)[None], aim.reshape(2, nq, 1, 2 * p)[None])
    live = np.stack([np.stack([r8 >= k, r8 <= SUBLANES - 1 - k]) for k in (1, 2, 4)]
                    + [np.ones((2, SUBLANES), bool)])
    live = jnp.asarray(live, F32)[:, :, None, :, None]
    ar, ai = ar * live, ai * live
    tab = jnp.stack([ar, ai], axis=1).reshape(8, 2, nq, SUBLANES, 2 * p)
    tab = tab.transpose(2, 1, 0, 3, 4)
    return pl_tab, pt_tab, ct_tab, bt_tab, dsk, tab


def _select_lanes(x, sel):
    hi = x.astype(BF16)
    lo = (x - hi.astype(F32)).astype(BF16)
    return _dot(hi, sel) + _dot(lo, sel)


def _select_rows(sel, x):
    hi = x.astype(BF16)
    lo = (x - hi.astype(F32)).astype(BF16)
    return _dot(sel, hi) + _dot(sel, lo)


def _roll_two_vregs(x, r):
    x0, x1 = x[:, :LANES], x[:, LANES:]
    if r >= LANES:
        x0, x1, r = x1, x0, r - LANES
    if r == 0:
        return jnp.concatenate([x0, x1], axis=1)
    lane = jax.lax.broadcasted_iota(jnp.int32, x0.shape, 1)
    a0, a1 = pltpu.roll(x0, r, axis=1), pltpu.roll(x1, r, axis=1)
    return jnp.concatenate([jnp.where(lane < r, a1, a0), jnp.where(lane < r, a0, a1)], axis=1)


def _ssm_prep_kernel(pl_ref, pt_ref, ct_ref, bt_ref, dsk_ref, sl_ref, sr_ref, e_ref,
                     m_ref, b_ref, c_ref):
    t, h = CHUNK, SSM_GROUP
    kk = t * h
    lane = jax.lax.broadcasted_iota(jnp.int32, (h, kk), 1)
    rowi = jax.lax.broadcasted_iota(jnp.int32, (h, kk), 0)
    half = jax.lax.broadcasted_iota(jnp.int32, (LANES, kk), 0) // SSM_STATE

    cl, cq = [], []
    for d in range(2):
        ctr, cti = _select_lanes(ct_ref[0, d], e_ref[...]), _select_lanes(ct_ref[1, d], e_ref[...])
        for out, pat in ((cl, d), (cq, 2 + d)):
            pr, pi = _select_lanes(pl_ref[0, d], sl_ref[pat]), _select_lanes(pl_ref[1, d], sl_ref[pat])
            out.append((ctr * pr - cti * pi, ctr * pi + cti * pr))

    for d in range(2):
        for ri, val in enumerate((cq[d][0], -cq[d][1])):
            r0 = (2 * d + ri) * LANES
            for gi in range(2):
                c_ref[r0:r0 + LANES, gi * kk:(gi + 1) * kk] = (
                    jnp.where(half == gi, val, 0.0).astype(BF16))

    for d in range(2):
        pwr, pwi = _select_rows(sr_ref[d], pt_ref[0, d]), _select_rows(sr_ref[d], pt_ref[1, d])
        for gi in range(2):
            btr = jnp.concatenate([bt_ref[0, d, gi]] * t, axis=0)
            bti = jnp.concatenate([bt_ref[1, d, gi]] * t, axis=0)
            r0, c0 = gi * kk, 2 * d * LANES
            b_ref[r0:r0 + kk, c0:c0 + LANES] = (btr * pwr - bti * pwi).astype(BF16)
            b_ref[r0:r0 + kk, c0 + LANES:c0 + 2 * LANES] = (btr * pwi + bti * pwr).astype(BF16)

    for gi in range(2):
        gen = [_dot(bt_ref[0, d, gi].astype(BF16), cl[d][0].astype(BF16))
               - _dot(bt_ref[1, d, gi].astype(BF16), cl[d][1].astype(BF16)) for d in range(2)]
        diag = jnp.where(lane < h, gen[1], 0.0) + jnp.where(rowi == lane, dsk_ref[gi], 0.0)
        kf, kb = gen[0] + diag, gen[1]
        for s in range(t):
            blk = jnp.where(lane >= h * s, _roll_two_vregs(kf, h * s), _roll_two_vregs(kb, h * s))
            m_ref[gi, s * h:(s + 1) * h, :] = blk.astype(BF16)


def _ssm_prep_constants():
    t, h = CHUNK, SSM_GROUP
    kk = t * h
    step = np.arange(kk) // h
    k = np.arange(LANES)
    pats = [step, (t - step) % t, step + 1, t - step]
    sel_lane = np.stack([(k[:, None] == pat[None, :]) for pat in pats]).astype(np.float32)
    sel_row = np.stack([(pat[:, None] == k[None, :]) for pat in (t - 1 - step, step)])
    tile_ho = (k[:, None] == (np.arange(kk) % h)[None, :])
    as_bf16 = lambda a: jnp.asarray(a, F32).astype(BF16)
    return as_bf16(sel_lane), as_bf16(sel_row), as_bf16(tile_ho)


def _ssm_prep(pl_tab, pt_tab, ct_tab, bt_tab, dsk):
    depth, nq = pl_tab.shape[:2]
    kk = CHUNK * SSM_GROUP
    sel_lane, sel_row, tile_ho = _ssm_prep_constants()
    tab5 = pl.BlockSpec((None, None, 2, 2, LANES, LANES), lambda l, q: (l, q, 0, 0, 0, 0))
    return pl.pallas_call(
        _ssm_prep_kernel,
        grid=(depth, nq),
        in_specs=[tab5, tab5, tab5,
                  pl.BlockSpec((None, None, 2, 2, 2, SSM_GROUP, LANES),
                               lambda l, q: (l, q, 0, 0, 0, 0, 0)),
                  pl.BlockSpec((None, None, 2, 1, kk), lambda l, q: (l, q, 0, 0, 0)),
                  _const_spec(sel_lane.shape), _const_spec(sel_row.shape),
                  _const_spec(tile_ho.shape)],
        out_specs=[pl.BlockSpec((None, 2, kk, kk), lambda l, q: (l, q, 0, 0)),
                   pl.BlockSpec((None, None, 2 * kk, 2 * kk), lambda l, q: (l, q, 0, 0)),
                   pl.BlockSpec((None, None, 2 * kk, 2 * kk), lambda l, q: (l, q, 0, 0))],
        out_shape=[jax.ShapeDtypeStruct((depth, 2 * nq, kk, kk), BF16),
                   jax.ShapeDtypeStruct((depth, nq, 2 * kk, 2 * kk), BF16),
                   jax.ShapeDtypeStruct((depth, nq, 2 * kk, 2 * kk), BF16)],
        compiler_params=pltpu.CompilerParams(
            dimension_semantics=("parallel", "parallel"), vmem_limit_bytes=VMEM_LIMIT),
        name="ssm_prep",
    )(pl_tab, pt_tab, ct_tab, bt_tab, dsk, sel_lane, sel_row, tile_ho)


def _mix_mlp_kernel(x_ref, gr_ref, gi_ref, ys_ref, bdr_ref, bdi_ref, fg_ref, wg_ref, bg_ref,
                    sg_ref, wo_ref, pmg_ref, plg_ref, w1_ref, w2_ref, pog_ref, o_ref, *, ff_blk):
    gr, gi = gr_ref[...], gi_ref[...]
    yf = jnp.concatenate(
        [_dot(gr[:, c:c + MXU_DIM], bdr_ref[c:c + MXU_DIM, c:c + MXU_DIM])
         + _dot(gi[:, c:c + MXU_DIM], bdi_ref[c:c + MXU_DIM, c:c + MXU_DIM])
         for c in range(0, D_FOURIER, MXU_DIM)], axis=1)
    yf = _rms(yf, fg_ref[...])
    y = _unrotate_rows(ys_ref[...])
    gl = jax.nn.gelu(y, approximate=True)
    gate = _dot(gl.astype(BF16), wg_ref[...]) + bg_ref[...]
    ys = _rms(gl * jax.nn.sigmoid(gate), sg_ref[...])
    cat = jnp.concatenate([yf.astype(BF16), ys.astype(BF16)], axis=-1)
    x1 = x_ref[...] + _rms(_dot(cat, wo_ref[...]), pmg_ref[...])
    h = _rms(x1, plg_ref[...]).astype(BF16)
    acc = jnp.zeros(x1.shape, F32)
    for j in range(D_FF // ff_blk):
        a = _dot(h, w1_ref[:, j * ff_blk:(j + 1) * ff_blk])
        a = jnp.square(jnp.maximum(a, 0.0)).astype(BF16)
        acc = acc + _dot(a, w2_ref[j * ff_blk:(j + 1) * ff_blk, :])
    o_ref[...] = x1 + _rms(acc, pog_ref[...])


def _mix_mlp(x2d, gri, ys, bdr, bdi, fg, wg, bg, sg, wo, pmg, plg, w1, w2, pog, tm, ff_blk, layer):
    n = x2d.shape[0]
    row = lambda i: (i, 0)
    lspec = functools.partial(_layer_spec, layer=layer)
    return pl.pallas_call(
        functools.partial(_mix_mlp_kernel, ff_blk=ff_blk),
        grid=(n // tm,),
        in_specs=[pl.BlockSpec((tm, D_MODEL), row),
                  pl.BlockSpec((None, tm, D_FOURIER), lambda i: (0, i, 0)),
                  pl.BlockSpec((None, tm, D_FOURIER), lambda i: (1, i, 0)),
                  pl.BlockSpec((tm, D_SSM), row),
                  lspec((D_FOURIER, D_FOURIER)), lspec((D_FOURIER, D_FOURIER)),
                  lspec((1, D_FOURIER)),
                  lspec((D_SSM, D_SSM)), lspec((1, D_SSM)), lspec((1, D_SSM)),
                  lspec((D_MODEL, D_MODEL)), lspec((1, D_MODEL)),
                  lspec((1, D_MODEL)),
                  lspec((D_MODEL, D_FF)), lspec((D_FF, D_MODEL)),
                  lspec((1, D_MODEL))],
        out_specs=pl.BlockSpec((tm, D_MODEL), row),
        out_shape=jax.ShapeDtypeStruct((n, D_MODEL), F32),
        compiler_params=pltpu.CompilerParams(
            dimension_semantics=("parallel",), vmem_limit_bytes=VMEM_LIMIT),
        name="mix_mlp",
    )(x2d, gri, gri, ys, bdr, bdi, fg, wg, bg, sg, wo, pmg, plg, w1, w2, pog)


def _fourier_channel_maps(w_f, seq):
    c = np.arange(FOURIER_HEAD_DIM)
    ang = 2.0 * np.pi * ((c[:, None] * c[None, :]) % FOURIER_HEAD_DIM) / FOURIER_HEAD_DIM
    scale = 1.0 / math.sqrt(seq * FOURIER_HEAD_DIM)
    hp = jax.lax.Precision.HIGHEST
    cw = jnp.einsum("cd,hde->hce", jnp.asarray(np.cos(ang) * scale, F32), w_f.astype(F32), precision=hp)
    sw = jnp.einsum("cd,hde->hce", jnp.asarray(np.sin(ang) * scale, F32), w_f.astype(F32), precision=hp)
    eye = jnp.eye(FOURIER_HEADS, dtype=F32)
    bd = lambda m: jnp.einsum("hce,hk->hcke", m, eye).reshape(D_FOURIER, D_FOURIER)
    return bd(cw).astype(BF16), bd(sw).astype(BF16)


def kernel(x, w_in, w_out, pre_mix_g, post_mix_g, pre_mlp_g, post_mlp_g, fourier_out_g, ssm_out_g,
           w_fourier, lam_re, lam_im, log_dt, b_re, b_im, c_re, c_im, d_skip, w_glu, b_glu,
           w_ff1, w_ff2):
    bsz, seq, _ = x.shape
    depth = w_in.shape[0]
    n = bsz * seq
    n1, n2 = FFT_N1, seq // FFT_N1
    nc = seq // CHUNK
    assert seq % (FFT_N1 * N2_BLK) == 0 and nc % RELAYOUT_CHUNKS == 0
    tm = min(512, n)
    wk_tab, m2_tab = _fft_tables(seq)

    rows = lambda v: v.reshape(depth, 1, -1).astype(F32)
    w_in_b, w_out_b, w_glu_b = w_in.astype(BF16), w_out.astype(BF16), w_glu.astype(BF16)
    w_ff1_b, w_ff2_b = w_ff1.astype(BF16), w_ff2.astype(BF16)
    pl_tab, pt_tab, ct_tab, bt_tab, dsk, tab = jax.vmap(_ssm_tables)(
        lam_re, lam_im, log_dt, b_re, b_im, c_re, c_im, d_skip)
    m_intra, b_pair, c_pair = _ssm_prep(pl_tab, pt_tab, ct_tab, bt_tab, dsk)
    bdr, bdi = jax.vmap(functools.partial(_fourier_channel_maps, seq=seq))(w_fourier)
    gains = [rows(v) for v in (pre_mix_g, fourier_out_g, b_glu, ssm_out_g, post_mix_g,
                               pre_mlp_g, post_mlp_g)]
    pre_mix, four_g, glu_b, ssm_g, post_mix, pre_mlp, post_mlp = gains

    x2d = x.reshape(n, D_MODEL).astype(F32)
    for i in range(depth):
        y, zs = _in_proj(x2d.reshape(bsz, n1, n2, D_MODEL), pre_mix, w_in_b, wk_tab, i)
        zs = zs.reshape(n, D_SSM)
        g = _fft_stage2(y.reshape(bsz, n1, 2 * n2, D_FOURIER), m2_tab, kb=2 * SUBLANES)
        gri = g.reshape(2, n, D_FOURIER)
        ysr = _ssm_chunked(zs, m_intra, b_pair, c_pair, tab, bsz, seq, i)
        x2d = _mix_mlp(x2d, gri, ysr, bdr, bdi, four_g, w_glu_b, glu_b, ssm_g, w_out_b,
                       post_mix, pre_mlp, w_ff1_b, w_ff2_b, post_mlp, tm, ff_blk=1024, layer=i)
    return x2d.reshape(bsz, seq, D_MODEL).astype(x.dtype)
```

```python
import functools
import math

import jax
import jax.numpy as jnp
import numpy as np
from jax.experimental import pallas as pl
from jax.experimental.pallas import tpu as pltpu

EPS = 1e-6
D_MODEL = 1024
D_FOURIER = 512
D_SSM = 512
FOURIER_HEADS = 8
FOURIER_HEAD_DIM = 64
SSM_GROUP = 16
SSM_GROUPS = 32
SSM_STATE = 64
D_FF = 4096

FFT_N1 = 64
CHUNK = 16
CHUNK_PITCH = CHUNK + 4
LANES = 128
SUBLANES = 8
MXU_DIM = 256
VMEM_LIMIT = 56 * 1024 * 1024

BF16 = jnp.bfloat16
F32 = jnp.float32


def _rms(x, g):
    return x * jax.lax.rsqrt(jnp.mean(x * x, axis=-1, keepdims=True) + EPS) * g


def _dot(a, b):
    return jnp.dot(a, b, preferred_element_type=F32)


def _const_spec(shape):
    nd = len(shape)
    return pl.BlockSpec(shape, lambda *_: (0,) * nd, pipeline_mode=pl.Buffered(1))


def _layer_spec(shape, layer):
    nd = len(shape)
    return pl.BlockSpec((None,) + tuple(shape), lambda *_: (layer,) + (0,) * nd,
                        pipeline_mode=pl.Buffered(1))


KRON_BLK = SUBLANES
N2_BLK = 2 * KRON_BLK


def _in_proj_kernel(x_ref, g_ref, w_ref, wk_ref, y_ref, zs_ref):
    n1 = x_ref.shape[0]
    x = x_ref[...].reshape(n1 * N2_BLK, D_MODEL)
    h = _rms(x, g_ref[...]).astype(BF16)
    z = _dot(h, w_ref[...])
    zf = z[:, :D_FOURIER].reshape(n1, N2_BLK // KRON_BLK, KRON_BLK, D_FOURIER)
    parts = []
    for a in range(N2_BLK // KRON_BLK):
        za = zf[:, a].reshape(n1 * KRON_BLK, D_FOURIER).astype(BF16)
        ya = _dot(wk_ref[...], za)
        parts.append(ya.reshape(2 * n1, KRON_BLK, D_FOURIER))
    y = jnp.concatenate(parts, axis=1)
    y_ref[...] = y.astype(BF16).reshape(n1, 2, N2_BLK, D_FOURIER)
    zs_ref[:, 0:CHUNK, :] = _rotate_rows(z[:, D_FOURIER:]).reshape(n1, CHUNK, D_SSM)
    zs_ref[:, CHUNK:, :] = jnp.zeros((n1, CHUNK_PITCH - CHUNK, D_SSM), F32)


def _in_proj(x4, g, w_bf16, wk, layer):
    bsz, n1, n2, _ = x4.shape
    rows = n1 * KRON_BLK
    return pl.pallas_call(
        _in_proj_kernel,
        grid=(bsz, n2 // N2_BLK),
        in_specs=[pl.BlockSpec((None, n1, N2_BLK, D_MODEL), lambda b, j: (b, 0, j, 0)),
                  _layer_spec((1, D_MODEL), layer),
                  _layer_spec((D_MODEL, D_MODEL), layer),
                  _const_spec((2 * rows, rows))],
        out_specs=[pl.BlockSpec((None, n1, 2, N2_BLK, D_FOURIER), lambda b, j: (b, 0, 0, j, 0)),
                   pl.BlockSpec((None, n1, None, CHUNK_PITCH, D_SSM),
                                lambda b, j: (b, 0, j, 0, 0))],
        out_shape=[jax.ShapeDtypeStruct((bsz, n1, 2, n2, D_FOURIER), BF16),
                   jax.ShapeDtypeStruct((bsz, n1, n2 // N2_BLK, CHUNK_PITCH, D_SSM), F32)],
        compiler_params=pltpu.CompilerParams(
            dimension_semantics=("parallel", "parallel"), vmem_limit_bytes=VMEM_LIMIT),
        name="in_proj",
    )(x4, g, w_bf16, wk)


def _fft2_kernel(m_ref, y_ref, g_ref, z_ref, *, kb, n2, pitch):
    nslab = D_FOURIER // LANES
    for j in range(kb):
        z = _dot(m_ref[j], y_ref[j])
        for ri in range(2):
            for s in range(nslab):
                z_ref[ri, s, j * pitch:j * pitch + n2, :] = (
                    z[ri * n2:(ri + 1) * n2, s * LANES:(s + 1) * LANES])

    def regroup(k2, _):
        for ri in range(2):
            for s in range(nslab):
                rows = [z_ref[ri, s, pl.ds(k2 + a * SUBLANES * pitch, SUBLANES, stride=pitch), :]
                        for a in range(kb // SUBLANES)]
                g_ref[ri, k2, :, s * LANES:(s + 1) * LANES] = (
                    jnp.concatenate(rows, axis=0).astype(BF16))
        return 0

    jax.lax.fori_loop(0, n2, regroup, 0)


def _fft_stage2(y, m2, kb):
    b, n1, rows, c = y.shape
    n2 = rows // 2
    pitch = n2 + SUBLANES
    return pl.pallas_call(
        functools.partial(_fft2_kernel, kb=kb, n2=n2, pitch=pitch),
        grid=(n1 // kb, b),
        in_specs=[pl.BlockSpec((kb, rows, rows), lambda j, i: (j, 0, 0)),
                  pl.BlockSpec((None, kb, rows, c), lambda j, i: (i, j, 0, 0))],
        out_specs=pl.BlockSpec((2, None, n2, kb, c), lambda j, i: (0, i, 0, j, 0)),
        out_shape=jax.ShapeDtypeStruct((2, b, n2, n1, c), BF16),
        scratch_shapes=[pltpu.VMEM((2, c // LANES, kb * pitch, LANES), F32)],
        compiler_params=pltpu.CompilerParams(
            dimension_semantics=("parallel", "parallel"), vmem_limit_bytes=VMEM_LIMIT),
        name="fft_stage2",
    )(m2, y)


def _fft_tables(seq):
    n1, n2 = FFT_N1, seq // FFT_N1
    k1 = np.arange(n1)
    ang1 = 2.0 * np.pi * ((k1[:, None] * k1[None, :]) % n1) / n1
    w1 = np.stack([np.cos(ang1), -np.sin(ang1)], axis=1).reshape(2 * n1, n1)
    wk = np.einsum("kn,ab->kanb", w1, np.eye(KRON_BLK)).reshape(2 * n1 * KRON_BLK, n1 * KRON_BLK)
    k2 = np.arange(n2)
    freq = k1[:, None, None] + n1 * k2[None, :, None]
    ang2 = 2.0 * np.pi * ((freq * k2[None, None, :]) % seq) / seq
    mr, mi = np.cos(ang2), -np.sin(ang2)
    m2 = np.concatenate([np.concatenate([mr, -mi], axis=2),
                         np.concatenate([mi, mr], axis=2)], axis=1)
    return jnp.asarray(wk, F32).astype(BF16), jnp.asarray(m2, F32).astype(BF16)


GROUPS_PER_BLOCK = LANES // SSM_GROUP
PAIRS_PER_BLOCK = GROUPS_PER_BLOCK // 2
RELAYOUT_CHUNKS = 8 * SUBLANES


def _rotate_rows(v):
    return jnp.concatenate(
        [pltpu.roll(v[:, c:c + LANES], 0, axis=1, stride=SSM_GROUP, stride_axis=0)
         for c in range(0, v.shape[1], LANES)], axis=1)


def _unrotate_rows(v):
    row = jax.lax.broadcasted_iota(jnp.int32, v.shape, 0)
    cols = []
    for c in range(0, v.shape[1], LANES):
        w = v[:, c:c + LANES]
        for bit in (1, 2, 4):
            w = jnp.where((row[:, :LANES] & bit) != 0,
                          pltpu.roll(w, LANES - SSM_GROUP * bit, axis=1), w)
        cols.append(w)
    return jnp.concatenate(cols, axis=1)


def _ssm_kernel(z_ref, mi_ref, bs_ref, cs_ref, tab_ref, y_ref,
                u2_ref, s_ref, car_ref, yg_ref, *, nc):
    gpb, npair = GROUPS_PER_BLOCK, PAIRS_PER_BLOCK
    t16 = RELAYOUT_CHUNKS
    rows_it = t16 * CHUNK_PITCH
    lane_blk = jax.lax.broadcasted_iota(jnp.int32, (t16, LANES), 1) // SSM_GROUP
    masks = [lane_blk == m for m in range(gpb)]

    def relayout_in(i, _):
        r0 = pl.multiple_of(i * rows_it, rows_it)
        ut = [z_ref[pl.ds(r0 + t, t16, stride=CHUNK_PITCH), :] for t in range(CHUNK)]
        c0 = pl.multiple_of(i * t16, t16)
        for g in range(gpb):
            for j in range(CHUNK // gpb):
                acc = ut[gpb * j]
                for tt in range(1, gpb):
                    acc = jnp.where(masks[(g + tt) % gpb], ut[gpb * j + tt], acc)
                if g:
                    acc = pltpu.roll(acc, LANES - SSM_GROUP * g, axis=1)
                u2_ref[g, pl.ds(c0, t16), j * LANES:(j + 1) * LANES] = acc.astype(BF16)
        return 0

    jax.lax.fori_loop(0, nc // t16, relayout_in, 0)

    for q in range(npair):
        up = jnp.concatenate([u2_ref[2 * q], u2_ref[2 * q + 1]], axis=1)
        s_ref[q] = _dot(up, bs_ref[q])

    row = jax.lax.broadcasted_iota(jnp.int32, (SUBLANES, LANES), 0)
    nblk = nc // SUBLANES

    def scan_block(tab, sr, si, cr, ci, down):
        keep = (lambda sh: row >= sh) if down else (lambda sh: row <= SUBLANES - 1 - sh)
        amt = (lambda sh: sh) if down else (lambda sh: SUBLANES - sh)
        for k, sh in enumerate((1, 2, 4)):
            tr = pltpu.roll(sr, amt(sh), axis=0)
            ti = pltpu.roll(si, amt(sh), axis=0)
            ar, ai = tab[2 * k], tab[2 * k + 1]
            sr, si = sr + (ar * tr - ai * ti), si + (ar * ti + ai * tr)
        fr = sr + (tab[6] * cr - tab[7] * ci)
        fi = si + (tab[6] * ci + tab[7] * cr)
        outr = jnp.where(keep(1), pltpu.roll(fr, amt(1), axis=0), cr)
        outi = jnp.where(keep(1), pltpu.roll(fi, amt(1), axis=0), ci)
        edge = SUBLANES - 1 if down else 0
        ncr = jnp.broadcast_to(fr[edge:edge + 1, :], (SUBLANES, LANES))
        nci = jnp.broadcast_to(fi[edge:edge + 1, :], (SUBLANES, LANES))
        return outr, outi, ncr, nci

    def scan_body(i, carry):
        new = []
        for q in range(npair):
            for d in range(2):
                blk = i if d == 0 else nblk - 1 - i
                r0 = pl.multiple_of(blk * SUBLANES, SUBLANES)
                lo = 2 * d * LANES
                tab = [tab_ref[q, d, k] for k in range(8)]
                cr, ci = carry[4 * q + 2 * d], carry[4 * q + 2 * d + 1]
                outr, outi, cr, ci = scan_block(
                    tab, s_ref[q, pl.ds(r0, SUBLANES), lo:lo + LANES],
                    s_ref[q, pl.ds(r0, SUBLANES), lo + LANES:lo + 2 * LANES], cr, ci, d == 0)
                car_ref[q, pl.ds(r0, SUBLANES), lo:lo + LANES] = outr
                car_ref[q, pl.ds(r0, SUBLANES), lo + LANES:lo + 2 * LANES] = outi
                new += [cr, ci]
        return tuple(new)

    zero = jnp.zeros((SUBLANES, LANES), F32)
    jax.lax.fori_loop(0, nblk, scan_body, (zero,) * (4 * npair))

    kk = CHUNK * SSM_GROUP
    for q in range(npair):
        yi = _dot(car_ref[q].astype(BF16), cs_ref[q])
        for e in range(2):
            g = 2 * q + e
            yg_ref[g] = _dot(u2_ref[g], mi_ref[g]) + yi[:, e * kk:(e + 1) * kk]

    def relayout_out(i, _):
        c0 = pl.multiple_of(i * t16, t16)
        r0 = pl.multiple_of(i * rows_it, rows_it)
        yrot = []
        for g in range(gpb):
            halves = []
            for j in range(CHUNK // gpb):
                v = yg_ref[g, pl.ds(c0, t16), j * LANES:(j + 1) * LANES]
                halves.append(pltpu.roll(v, SSM_GROUP * g, axis=1) if g else v)
            yrot.append(halves)
        for t in range(CHUNK):
            j, tt = divmod(t, gpb)
            acc = yrot[0][j]
            for g in range(1, gpb):
                acc = jnp.where(masks[(g + tt) % gpb], yrot[g][j], acc)
            y_ref[pl.ds(r0 + t, t16, stride=CHUNK_PITCH), :] = acc
        for t in range(CHUNK, CHUNK_PITCH):
            y_ref[pl.ds(r0 + t, t16, stride=CHUNK_PITCH), :] = jnp.zeros((t16, LANES), F32)
        return 0

    jax.lax.fori_loop(0, nc // t16, relayout_out, 0)


def _ssm_chunked(zs, m_intra, b_pair, c_pair, tab, bsz, seq, layer):
    nc = seq // CHUNK
    gpb, npair = GROUPS_PER_BLOCK, PAIRS_PER_BLOCK
    k = CHUNK * SSM_GROUP
    rows = nc * CHUNK_PITCH
    return pl.pallas_call(
        functools.partial(_ssm_kernel, nc=nc),
        grid=(D_SSM // LANES, bsz),
        in_specs=[pl.BlockSpec((rows, LANES), lambda j, b: (b, j)),
                  pl.BlockSpec((None, gpb, k, k), lambda j, b: (layer, j, 0, 0)),
                  pl.BlockSpec((None, npair, 2 * k, 2 * k), lambda j, b: (layer, j, 0, 0)),
                  pl.BlockSpec((None, npair, 2 * k, 2 * k), lambda j, b: (layer, j, 0, 0)),
                  pl.BlockSpec((None, npair, 2, 8, SUBLANES, LANES),
                               lambda j, b: (layer, j, 0, 0, 0, 0))],
        out_specs=pl.BlockSpec((rows, LANES), lambda j, b: (b, j)),
        out_shape=jax.ShapeDtypeStruct((bsz * rows, D_SSM), F32),
        scratch_shapes=[pltpu.VMEM((gpb, nc, k), BF16),
                        pltpu.VMEM((npair, nc, 2 * k), F32),
                        pltpu.VMEM((npair, nc, 2 * k), F32),
                        pltpu.VMEM((gpb, nc, k), F32)],
        compiler_params=pltpu.CompilerParams(
            dimension_semantics=("parallel", "parallel"), vmem_limit_bytes=VMEM_LIMIT),
        name="ssm_chunk",
    )(zs, m_intra, b_pair, c_pair, tab)


def _ssm_tables(lam_re, lam_im, log_dt, b_re, b_im, c_re, c_im, d_skip):
    t, h, p, g = CHUNK, SSM_GROUP, SSM_STATE, SSM_GROUPS
    nq = g // 2
    dt = jnp.exp(log_dt.astype(F32))[..., None]
    are = lam_re.astype(F32) * dt
    aim = lam_im.astype(F32) * dt

    def power(expo, a_re, a_im):
        mag = jnp.exp(a_re * expo)
        return mag * jnp.cos(a_im * expo), mag * jnp.sin(a_im * expo)

    lbr, lbi = power(1.0, are, aim)
    den = lam_re * lam_re + lam_im * lam_im
    cr = ((lbr - 1.0) * lam_re + lbi * lam_im) / den
    ci = (lbi * lam_re - (lbr - 1.0) * lam_im) / den
    bbr = cr[..., None] * b_re - ci[..., None] * b_im
    bbi = cr[..., None] * b_im + ci[..., None] * b_re

    def pair_rows(x):
        return x.reshape(2, 2, nq, 2 * p, x.shape[-1]).transpose(2, 0, 1, 3, 4)

    lane_pad = lambda x: jnp.pad(x, [(0, 0)] * (x.ndim - 1) + [(0, LANES - x.shape[-1])])
    steps = jnp.arange(t + 1, dtype=F32)
    pw = jnp.stack(power(steps, are[..., None], aim[..., None]))
    pl_tab = lane_pad(pair_rows(pw))
    pt_tab = jnp.pad(jnp.swapaxes(pair_rows(pw), -1, -2),
                     [(0, 0)] * 3 + [(0, LANES - t - 1), (0, 0)])
    ct_tab = lane_pad(pair_rows(jnp.stack([jnp.swapaxes(c_re.astype(F32), -1, -2),
                                           jnp.swapaxes(c_im.astype(F32), -1, -2)])))
    bt = jnp.stack([jnp.swapaxes(bbr, -1, -2), jnp.swapaxes(bbi, -1, -2)])
    bt = bt.reshape(2, 2, nq, 2, h, p)
    zero = jnp.zeros_like(bt[:, :, :, 0])
    bt_tab = jnp.stack([jnp.concatenate([bt[:, :, :, 0], zero], -1),
                        jnp.concatenate([zero, bt[:, :, :, 1]], -1)], axis=3)
    bt_tab = bt_tab.transpose(2, 0, 1, 3, 4, 5)
    dsk = jnp.pad(d_skip.astype(F32), ((0, 0), (0, t * h - h))).reshape(nq, 2, 1, t * h)

    r8 = np.arange(SUBLANES)
    expo = np.stack([np.stack([np.full(SUBLANES, k), np.full(SUBLANES, k)]) for k in (1, 2, 4)]
                    + [np.stack([r8 + 1, SUBLANES - r8])])
    expo = jnp.asarray(t * expo, F32)[:, :, None, :, None]
    ar, ai = power(expo, are.reshape(2, nq, 1, 2 * p)[None], aim.reshape(2, nq, 1, 2 * p)[None])
    live = np.stack([np.stack([r8 >= k, r8 <= SUBLANES - 1 - k]) for k in (1, 2, 4)]
                    + [np.ones((2, SUBLANES), bool)])
    live = jnp.asarray(live, F32)[:, :, None, :, None]
    ar, ai = ar * live, ai * live
    tab = jnp.stack([ar, ai], axis=1).reshape(8, 2, nq, SUBLANES, 2 * p)
    tab = tab.transpose(2, 1, 0, 3, 4)
    return pl_tab, pt_tab, ct_tab, bt_tab, dsk, tab


def _select_lanes(x, sel):
    hi = x.astype(BF16)
    lo = (x - hi.astype(F32)).astype(BF16)
    return _dot(hi, sel) + _dot(lo, sel)


def _select_rows(sel, x):
    hi = x.astype(BF16)
    lo = (x - hi.astype(F32)).astype(BF16)
    return _dot(sel, hi) + _dot(sel, lo)


def _roll_two_vregs(x, r):
    x0, x1 = x[:, :LANES], x[:, LANES:]
    if r >= LANES:
        x0, x1, r = x1, x0, r - LANES
    if r == 0:
        return jnp.concatenate([x0, x1], axis=1)
    lane = jax.lax.broadcasted_iota(jnp.int32, x0.shape, 1)
    a0, a1 = pltpu.roll(x0, r, axis=1), pltpu.roll(x1, r, axis=1)
    return jnp.concatenate([jnp.where(lane < r, a1, a0), jnp.where(lane < r, a0, a1)], axis=1)


def _ssm_prep_kernel(pl_ref, pt_ref, ct_ref, bt_ref, dsk_ref, sl_ref, sr_ref, e_ref,
                     m_ref, b_ref, c_ref):
    t, h = CHUNK, SSM_GROUP
    kk = t * h
    lane = jax.lax.broadcasted_iota(jnp.int32, (h, kk), 1)
    rowi = jax.lax.broadcasted_iota(jnp.int32, (h, kk), 0)
    half = jax.lax.broadcasted_iota(jnp.int32, (LANES, kk), 0) // SSM_STATE

    cl, cq = [], []
    for d in range(2):
        ctr, cti = _select_lanes(ct_ref[0, d], e_ref[...]), _select_lanes(ct_ref[1, d], e_ref[...])
        for out, pat in ((cl, d), (cq, 2 + d)):
            pr, pi = _select_lanes(pl_ref[0, d], sl_ref[pat]), _select_lanes(pl_ref[1, d], sl_ref[pat])
            out.append((ctr * pr - cti * pi, ctr * pi + cti * pr))

    for d in range(2):
        for ri, val in enumerate((cq[d][0], -cq[d][1])):
            r0 = (2 * d + ri) * LANES
            for gi in range(2):
                c_ref[r0:r0 + LANES, gi * kk:(gi + 1) * kk] = (
                    jnp.where(half == gi, val, 0.0).astype(BF16))

    for d in range(2):
        pwr, pwi = _select_rows(sr_ref[d], pt_ref[0, d]), _select_rows(sr_ref[d], pt_ref[1, d])
        for gi in range(2):
            btr = jnp.concatenate([bt_ref[0, d, gi]] * t, axis=0)
            bti = jnp.concatenate([bt_ref[1, d, gi]] * t, axis=0)
            r0, c0 = gi * kk, 2 * d * LANES
            b_ref[r0:r0 + kk, c0:c0 + LANES] = (btr * pwr - bti * pwi).astype(BF16)
            b_ref[r0:r0 + kk, c0 + LANES:c0 + 2 * LANES] = (btr * pwi + bti * pwr).astype(BF16)

    for gi in range(2):
        gen = [_dot(bt_ref[0, d, gi].astype(BF16), cl[d][0].astype(BF16))
               - _dot(bt_ref[1, d, gi].astype(BF16), cl[d][1].astype(BF16)) for d in range(2)]
        diag = jnp.where(lane < h, gen[1], 0.0) + jnp.where(rowi == lane, dsk_ref[gi], 0.0)
        kf, kb = gen[0] + diag, gen[1]
        for s in range(t):
            blk = jnp.where(lane >= h * s, _roll_two_vregs(kf, h * s), _roll_two_vregs(kb, h * s))
            m_ref[gi, s * h:(s + 1) * h, :] = blk.astype(BF16)


def _ssm_prep_constants():
    t, h = CHUNK, SSM_GROUP
    kk = t * h
    step = np.arange(kk) // h
    k = np.arange(LANES)
    pats = [step, (t - step) % t, step + 1, t - step]
    sel_lane = np.stack([(k[:, None] == pat[None, :]) for pat in pats]).astype(np.float32)
    sel_row = np.stack([(pat[:, None] == k[None, :]) for pat in (t - 1 - step, step)])
    tile_ho = (k[:, None] == (np.arange(kk) % h)[None, :])
    as_bf16 = lambda a: jnp.asarray(a, F32).astype(BF16)
    return as_bf16(sel_lane), as_bf16(sel_row), as_bf16(tile_ho)


def _ssm_prep(pl_tab, pt_tab, ct_tab, bt_tab, dsk):
    depth, nq = pl_tab.shape[:2]
    kk = CHUNK * SSM_GROUP
    sel_lane, sel_row, tile_ho = _ssm_prep_constants()
    tab5 = pl.BlockSpec((None, None, 2, 2, LANES, LANES), lambda l, q: (l, q, 0, 0, 0, 0))
    return pl.pallas_call(
        _ssm_prep_kernel,
        grid=(depth, nq),
        in_specs=[tab5, tab5, tab5,
                  pl.BlockSpec((None, None, 2, 2, 2, SSM_GROUP, LANES),
                               lambda l, q: (l, q, 0, 0, 0, 0, 0)),
                  pl.BlockSpec((None, None, 2, 1, kk), lambda l, q: (l, q, 0, 0, 0)),
                  _const_spec(sel_lane.shape), _const_spec(sel_row.shape),
                  _const_spec(tile_ho.shape)],
        out_specs=[pl.BlockSpec((None, 2, kk, kk), lambda l, q: (l, q, 0, 0)),
                   pl.BlockSpec((None, None, 2 * kk, 2 * kk), lambda l, q: (l, q, 0, 0)),
                   pl.BlockSpec((None, None, 2 * kk, 2 * kk), lambda l, q: (l, q, 0, 0))],
        out_shape=[jax.ShapeDtypeStruct((depth, 2 * nq, kk, kk), BF16),
                   jax.ShapeDtypeStruct((depth, nq, 2 * kk, 2 * kk), BF16),
                   jax.ShapeDtypeStruct((depth, nq, 2 * kk, 2 * kk), BF16)],
        compiler_params=pltpu.CompilerParams(
            dimension_semantics=("parallel", "parallel"), vmem_limit_bytes=VMEM_LIMIT),
        name="ssm_prep",
    )(pl_tab, pt_tab, ct_tab, bt_tab, dsk, sel_lane, sel_row, tile_ho)


def _mix_mlp_kernel(x_ref, gr_ref, gi_ref, ys_ref, bdr_ref, bdi_ref, fg_ref, wg_ref, bg_ref,
                    sg_ref, wo_ref, pmg_ref, plg_ref, w1_ref, w2_ref, pog_ref, o_ref, *, ff_blk):
    gr, gi = gr_ref[...], gi_ref[...]
    yf = jnp.concatenate(
        [_dot(gr[:, c:c + MXU_DIM], bdr_ref[c:c + MXU_DIM, c:c + MXU_DIM])
         + _dot(gi[:, c:c + MXU_DIM], bdi_ref[c:c + MXU_DIM, c:c + MXU_DIM])
         for c in range(0, D_FOURIER, MXU_DIM)], axis=1)
    yf = _rms(yf, fg_ref[...])
    y = _unrotate_rows(ys_ref[:, 0:CHUNK, :].reshape(x_ref.shape[0], D_SSM))
    gl = jax.nn.gelu(y, approximate=True)
    gate = _dot(gl.astype(BF16), wg_ref[...]) + bg_ref[...]
    ys = _rms(gl * jax.nn.sigmoid(gate), sg_ref[...])
    cat = jnp.concatenate([yf.astype(BF16), ys.astype(BF16)], axis=-1)
    x1 = x_ref[...] + _rms(_dot(cat, wo_ref[...]), pmg_ref[...])
    h = _rms(x1, plg_ref[...]).astype(BF16)
    acc = jnp.zeros(x1.shape, F32)
    for j in range(D_FF // ff_blk):
        a = _dot(h, w1_ref[:, j * ff_blk:(j + 1) * ff_blk])
        a = jnp.square(jnp.maximum(a, 0.0)).astype(BF16)
        acc = acc + _dot(a, w2_ref[j * ff_blk:(j + 1) * ff_blk, :])
    o_ref[...] = x1 + _rms(acc, pog_ref[...])


def _mix_mlp(x2d, gri, ys, bdr, bdi, fg, wg, bg, sg, wo, pmg, plg, w1, w2, pog, tm, ff_blk, layer):
    n = x2d.shape[0]
    row = lambda i: (i, 0)
    lspec = functools.partial(_layer_spec, layer=layer)
    return pl.pallas_call(
        functools.partial(_mix_mlp_kernel, ff_blk=ff_blk),
        grid=(n // tm,),
        in_specs=[pl.BlockSpec((tm, D_MODEL), row),
                  pl.BlockSpec((None, tm, D_FOURIER), lambda i: (0, i, 0)),
                  pl.BlockSpec((None, tm, D_FOURIER), lambda i: (1, i, 0)),
                  pl.BlockSpec((tm // CHUNK, CHUNK_PITCH, D_SSM), lambda i: (i, 0, 0)),
                  lspec((D_FOURIER, D_FOURIER)), lspec((D_FOURIER, D_FOURIER)),
                  lspec((1, D_FOURIER)),
                  lspec((D_SSM, D_SSM)), lspec((1, D_SSM)), lspec((1, D_SSM)),
                  lspec((D_MODEL, D_MODEL)), lspec((1, D_MODEL)),
                  lspec((1, D_MODEL)),
                  lspec((D_MODEL, D_FF)), lspec((D_FF, D_MODEL)),
                  lspec((1, D_MODEL))],
        out_specs=pl.BlockSpec((tm, D_MODEL), row),
        out_shape=jax.ShapeDtypeStruct((n, D_MODEL), F32),
        compiler_params=pltpu.CompilerParams(
            dimension_semantics=("parallel",), vmem_limit_bytes=VMEM_LIMIT),
        name="mix_mlp",
    )(x2d, gri, gri, ys, bdr, bdi, fg, wg, bg, sg, wo, pmg, plg, w1, w2, pog)


def _fourier_channel_maps(w_f, seq):
    c = np.arange(FOURIER_HEAD_DIM)
    ang = 2.0 * np.pi * ((c[:, None] * c[None, :]) % FOURIER_HEAD_DIM) / FOURIER_HEAD_DIM
    scale = 1.0 / math.sqrt(seq * FOURIER_HEAD_DIM)
    hp = jax.lax.Precision.HIGHEST
    cw = jnp.einsum("cd,hde->hce", jnp.asarray(np.cos(ang) * scale, F32), w_f.astype(F32), precision=hp)
    sw = jnp.einsum("cd,hde->hce", jnp.asarray(np.sin(ang) * scale, F32), w_f.astype(F32), precision=hp)
    eye = jnp.eye(FOURIER_HEADS, dtype=F32)
    bd = lambda m: jnp.einsum("hce,hk->hcke", m, eye).reshape(D_FOURIER, D_FOURIER)
    return bd(cw).astype(BF16), bd(sw).astype(BF16)


def kernel(x, w_in, w_out, pre_mix_g, post_mix_g, pre_mlp_g, post_mlp_g, fourier_out_g, ssm_out_g,
           w_fourier, lam_re, lam_im, log_dt, b_re, b_im, c_re, c_im, d_skip, w_glu, b_glu,
           w_ff1, w_ff2):
    bsz, seq, _ = x.shape
    depth = w_in.shape[0]
    n = bsz * seq
    n1, n2 = FFT_N1, seq // FFT_N1
    nc = seq // CHUNK
    assert seq % (FFT_N1 * N2_BLK) == 0 and nc % RELAYOUT_CHUNKS == 0 and N2_BLK == CHUNK
    tm = min(512, n)
    wk_tab, m2_tab = _fft_tables(seq)

    rows = lambda v: v.reshape(depth, 1, -1).astype(F32)
    w_in_b, w_out_b, w_glu_b = w_in.astype(BF16), w_out.astype(BF16), w_glu.astype(BF16)
    w_ff1_b, w_ff2_b = w_ff1.astype(BF16), w_ff2.astype(BF16)
    pl_tab, pt_tab, ct_tab, bt_tab, dsk, tab = jax.vmap(_ssm_tables)(
        lam_re, lam_im, log_dt, b_re, b_im, c_re, c_im, d_skip)
    m_intra, b_pair, c_pair = _ssm_prep(pl_tab, pt_tab, ct_tab, bt_tab, dsk)
    bdr, bdi = jax.vmap(functools.partial(_fourier_channel_maps, seq=seq))(w_fourier)
    gains = [rows(v) for v in (pre_mix_g, fourier_out_g, b_glu, ssm_out_g, post_mix_g,
                               pre_mlp_g, post_mlp_g)]
    pre_mix, four_g, glu_b, ssm_g, post_mix, pre_mlp, post_mlp = gains

    x2d = x.reshape(n, D_MODEL).astype(F32)
    for i in range(depth):
        y, zs = _in_proj(x2d.reshape(bsz, n1, n2, D_MODEL), pre_mix, w_in_b, wk_tab, i)
        zs = zs.reshape(n // CHUNK * CHUNK_PITCH, D_SSM)
        g = _fft_stage2(y.reshape(bsz, n1, 2 * n2, D_FOURIER), m2_tab, kb=2 * SUBLANES)
        gri = g.reshape(2, n, D_FOURIER)
        ysr = _ssm_chunked(zs, m_intra, b_pair, c_pair, tab, bsz, seq, i)
        ysr = ysr.reshape(n // CHUNK, CHUNK_PITCH, D_SSM)
        x2d = _mix_mlp(x2d, gri, ysr, bdr, bdi, four_g, w_glu_b, glu_b, ssm_g, w_out_b,
                       post_mix, pre_mlp, w_ff1_b, w_ff2_b, post_mlp, tm, ff_blk=1024, layer=i)
    return x2d.reshape(bsz, seq, D_MODEL).astype(x.dtype)
```

```python
import functools
import math

import jax
import jax.numpy as jnp
import numpy as np
from jax.experimental import pallas as pl
from jax.experimental.pallas import tpu as pltpu

EPS = 1e-6
D_MODEL = 1024
D_FOURIER = 512
D_SSM = 512
FOURIER_HEADS = 8
FOURIER_HEAD_DIM = 64
SSM_GROUP = 16
SSM_GROUPS = 32
SSM_STATE = 64
D_FF = 4096

FFT_N1 = 64
CHUNK = 16
LANES = 128
SUBLANES = 8
MXU_DIM = 256
VMEM_LIMIT = 56 * 1024 * 1024

BF16 = jnp.bfloat16
F32 = jnp.float32


def _rms(x, g):
    return x * jax.lax.rsqrt(jnp.mean(x * x, axis=-1, keepdims=True) + EPS) * g


def _dot(a, b):
    return jnp.dot(a, b, preferred_element_type=F32)


def _const_spec(shape):
    nd = len(shape)
    return pl.BlockSpec(shape, lambda *_: (0,) * nd, pipeline_mode=pl.Buffered(1))


def _layer_spec(shape, layer):
    nd = len(shape)
    return pl.BlockSpec((None,) + tuple(shape), lambda *_: (layer,) + (0,) * nd,
                        pipeline_mode=pl.Buffered(1))


KRON_BLK = SUBLANES
N2_BLK = 2 * KRON_BLK


def _in_proj_kernel(x_ref, g_ref, w_ref, wk_ref, y_ref, zs_ref):
    n1 = x_ref.shape[0]
    x = x_ref[...].reshape(n1 * N2_BLK, D_MODEL)
    h = _rms(x, g_ref[...]).astype(BF16)
    z = _dot(h, w_ref[...])
    zf = z[:, :D_FOURIER].reshape(n1, N2_BLK // KRON_BLK, KRON_BLK, D_FOURIER)
    parts = []
    for a in range(N2_BLK // KRON_BLK):
        za = zf[:, a].reshape(n1 * KRON_BLK, D_FOURIER).astype(BF16)
        ya = _dot(wk_ref[...], za)
        parts.append(ya.reshape(2 * n1, KRON_BLK, D_FOURIER))
    y = jnp.concatenate(parts, axis=1)
    y_ref[...] = y.astype(BF16).reshape(n1, 2, N2_BLK, D_FOURIER)
    zs_ref[...] = _rotate_rows(z[:, D_FOURIER:]).reshape(n1, N2_BLK, D_SSM)


def _in_proj(x4, g, w_bf16, wk, layer):
    bsz, n1, n2, _ = x4.shape
    rows = n1 * KRON_BLK
    return pl.pallas_call(
        _in_proj_kernel,
        grid=(bsz, n2 // N2_BLK),
        in_specs=[pl.BlockSpec((None, n1, N2_BLK, D_MODEL), lambda b, j: (b, 0, j, 0)),
                  _layer_spec((1, D_MODEL), layer),
                  _layer_spec((D_MODEL, D_MODEL), layer),
                  _const_spec((2 * rows, rows))],
        out_specs=[pl.BlockSpec((None, n1, 2, N2_BLK, D_FOURIER), lambda b, j: (b, 0, 0, j, 0)),
                   pl.BlockSpec((None, n1, N2_BLK, D_SSM), lambda b, j: (b, 0, j, 0))],
        out_shape=[jax.ShapeDtypeStruct((bsz, n1, 2, n2, D_FOURIER), BF16),
                   jax.ShapeDtypeStruct((bsz, n1, n2, D_SSM), F32)],
        compiler_params=pltpu.CompilerParams(
            dimension_semantics=("parallel", "parallel"), vmem_limit_bytes=VMEM_LIMIT),
        name="in_proj",
    )(x4, g, w_bf16, wk)


REGROUP_UNROLL = 8


def _fft2_kernel(m_ref, y_ref, g_ref, z_ref, *, kb, n2, pitch):
    nslab = D_FOURIER // LANES
    for j in range(kb):
        z = _dot(m_ref[j], y_ref[j])
        for ri in range(2):
            for s in range(nslab):
                z_ref[ri, s, j * pitch:j * pitch + n2, :] = (
                    z[ri * n2:(ri + 1) * n2, s * LANES:(s + 1) * LANES])

    def regroup(i, _):
        for u in range(REGROUP_UNROLL):
            k2 = i * REGROUP_UNROLL + u
            for ri in range(2):
                for s in range(nslab):
                    rows = [z_ref[ri, s, pl.ds(k2 + a * SUBLANES * pitch, SUBLANES, stride=pitch), :]
                            for a in range(kb // SUBLANES)]
                    g_ref[ri, k2, :, s * LANES:(s + 1) * LANES] = (
                        jnp.concatenate(rows, axis=0).astype(BF16))
        return 0

    jax.lax.fori_loop(0, n2 // REGROUP_UNROLL, regroup, 0)


def _fft_stage2(y, m2, kb):
    b, n1, rows, c = y.shape
    n2 = rows // 2
    pitch = n2 + SUBLANES
    return pl.pallas_call(
        functools.partial(_fft2_kernel, kb=kb, n2=n2, pitch=pitch),
        grid=(n1 // kb, b),
        in_specs=[pl.BlockSpec((kb, rows, rows), lambda j, i: (j, 0, 0)),
                  pl.BlockSpec((None, kb, rows, c), lambda j, i: (i, j, 0, 0))],
        out_specs=pl.BlockSpec((2, None, n2, kb, c), lambda j, i: (0, i, 0, j, 0)),
        out_shape=jax.ShapeDtypeStruct((2, b, n2, n1, c), BF16),
        scratch_shapes=[pltpu.VMEM((2, c // LANES, kb * pitch, LANES), F32)],
        compiler_params=pltpu.CompilerParams(
            dimension_semantics=("parallel", "parallel"), vmem_limit_bytes=VMEM_LIMIT),
        name="fft_stage2",
    )(m2, y)


def _fft_tables(seq):
    n1, n2 = FFT_N1, seq // FFT_N1
    k1 = np.arange(n1)
    ang1 = 2.0 * np.pi * ((k1[:, None] * k1[None, :]) % n1) / n1
    w1 = np.stack([np.cos(ang1), -np.sin(ang1)], axis=1).reshape(2 * n1, n1)
    wk = np.einsum("kn,ab->kanb", w1, np.eye(KRON_BLK)).reshape(2 * n1 * KRON_BLK, n1 * KRON_BLK)
    k2 = np.arange(n2)
    freq = k1[:, None, None] + n1 * k2[None, :, None]
    ang2 = 2.0 * np.pi * ((freq * k2[None, None, :]) % seq) / seq
    mr, mi = np.cos(ang2), -np.sin(ang2)
    m2 = np.concatenate([np.concatenate([mr, -mi], axis=2),
                         np.concatenate([mi, mr], axis=2)], axis=1)
    return jnp.asarray(wk, F32).astype(BF16), jnp.asarray(m2, F32).astype(BF16)


GROUPS_PER_BLOCK = LANES // SSM_GROUP
PAIRS_PER_BLOCK = GROUPS_PER_BLOCK // 2
RELAYOUT_CHUNKS = 8 * SUBLANES


def _rotate_rows(v):
    return jnp.concatenate(
        [pltpu.roll(v[:, c:c + LANES], 0, axis=1, stride=SSM_GROUP, stride_axis=0)
         for c in range(0, v.shape[1], LANES)], axis=1)


def _unrotate_rows(v):
    row = jax.lax.broadcasted_iota(jnp.int32, v.shape, 0)
    cols = []
    for c in range(0, v.shape[1], LANES):
        w = v[:, c:c + LANES]
        for bit in (1, 2, 4):
            w = jnp.where((row[:, :LANES] & bit) != 0,
                          pltpu.roll(w, LANES - SSM_GROUP * bit, axis=1), w)
        cols.append(w)
    return jnp.concatenate(cols, axis=1)


def _ssm_kernel(z_ref, mi_ref, bs_ref, cs_ref, tab_ref, y_ref,
                u2_ref, s_ref, car_ref, yg_ref, *, nc):
    gpb, npair = GROUPS_PER_BLOCK, PAIRS_PER_BLOCK
    t16 = RELAYOUT_CHUNKS
    rows_it = t16 * CHUNK
    lane_blk = jax.lax.broadcasted_iota(jnp.int32, (t16, LANES), 1) // SSM_GROUP
    masks = [lane_blk == m for m in range(gpb)]

    def relayout_in(i, _):
        r0 = pl.multiple_of(i * rows_it, rows_it)
        ut = [z_ref[pl.ds(r0 + t, t16, stride=CHUNK), :] for t in range(CHUNK)]
        c0 = pl.multiple_of(i * t16, t16)
        for g in range(gpb):
            for j in range(CHUNK // gpb):
                acc = ut[gpb * j]
                for tt in range(1, gpb):
                    acc = jnp.where(masks[(g + tt) % gpb], ut[gpb * j + tt], acc)
                if g:
                    acc = pltpu.roll(acc, LANES - SSM_GROUP * g, axis=1)
                u2_ref[g, pl.ds(c0, t16), j * LANES:(j + 1) * LANES] = acc.astype(BF16)
        return 0

    jax.lax.fori_loop(0, nc // t16, relayout_in, 0)

    for q in range(npair):
        up = jnp.concatenate([u2_ref[2 * q], u2_ref[2 * q + 1]], axis=1)
        s_ref[q] = _dot(up, bs_ref[q])

    row = jax.lax.broadcasted_iota(jnp.int32, (SUBLANES, LANES), 0)
    nblk = nc // SUBLANES

    def scan_block(tab, sr, si, cr, ci, down):
        keep = (lambda sh: row >= sh) if down else (lambda sh: row <= SUBLANES - 1 - sh)
        amt = (lambda sh: sh) if down else (lambda sh: SUBLANES - sh)
        for k, sh in enumerate((1, 2, 4)):
            tr = pltpu.roll(sr, amt(sh), axis=0)
            ti = pltpu.roll(si, amt(sh), axis=0)
            ar, ai = tab[2 * k], tab[2 * k + 1]
            sr, si = sr + (ar * tr - ai * ti), si + (ar * ti + ai * tr)
        fr = sr + (tab[6] * cr - tab[7] * ci)
        fi = si + (tab[6] * ci + tab[7] * cr)
        outr = jnp.where(keep(1), pltpu.roll(fr, amt(1), axis=0), cr)
        outi = jnp.where(keep(1), pltpu.roll(fi, amt(1), axis=0), ci)
        edge = SUBLANES - 1 if down else 0
        ncr = jnp.broadcast_to(fr[edge:edge + 1, :], (SUBLANES, LANES))
        nci = jnp.broadcast_to(fi[edge:edge + 1, :], (SUBLANES, LANES))
        return outr, outi, ncr, nci

    def scan_body(i, carry):
        new = []
        for q in range(npair):
            for d in range(2):
                blk = i if d == 0 else nblk - 1 - i
                r0 = pl.multiple_of(blk * SUBLANES, SUBLANES)
                lo = 2 * d * LANES
                tab = [tab_ref[q, d, k] for k in range(8)]
                cr, ci = carry[4 * q + 2 * d], carry[4 * q + 2 * d + 1]
                outr, outi, cr, ci = scan_block(
                    tab, s_ref[q, pl.ds(r0, SUBLANES), lo:lo + LANES],
                    s_ref[q, pl.ds(r0, SUBLANES), lo + LANES:lo + 2 * LANES], cr, ci, d == 0)
                car_ref[q, pl.ds(r0, SUBLANES), lo:lo + LANES] = outr
                car_ref[q, pl.ds(r0, SUBLANES), lo + LANES:lo + 2 * LANES] = outi
                new += [cr, ci]
        return tuple(new)

    zero = jnp.zeros((SUBLANES, LANES), F32)
    jax.lax.fori_loop(0, nblk, scan_body, (zero,) * (4 * npair))

    kk = CHUNK * SSM_GROUP
    for q in range(npair):
        yi = _dot(car_ref[q].astype(BF16), cs_ref[q])
        for e in range(2):
            g = 2 * q + e
            yg_ref[g] = _dot(u2_ref[g], mi_ref[g]) + yi[:, e * kk:(e + 1) * kk]

    def relayout_out(i, _):
        c0 = pl.multiple_of(i * t16, t16)
        r0 = pl.multiple_of(i * rows_it, rows_it)
        yrot = []
        for g in range(gpb):
            halves = []
            for j in range(CHUNK // gpb):
                v = yg_ref[g, pl.ds(c0, t16), j * LANES:(j + 1) * LANES]
                halves.append(pltpu.roll(v, SSM_GROUP * g, axis=1) if g else v)
            yrot.append(halves)
        for t in range(CHUNK):
            j, tt = divmod(t, gpb)
            acc = yrot[0][j]
            for g in range(1, gpb):
                acc = jnp.where(masks[(g + tt) % gpb], yrot[g][j], acc)
            y_ref[pl.ds(r0 + t, t16, stride=CHUNK), :] = acc
        return 0

    jax.lax.fori_loop(0, nc // t16, relayout_out, 0)


def _ssm_chunked(zs, m_intra, b_pair, c_pair, tab, bsz, seq, layer):
    nc = seq // CHUNK
    gpb, npair = GROUPS_PER_BLOCK, PAIRS_PER_BLOCK
    k = CHUNK * SSM_GROUP
    return pl.pallas_call(
        functools.partial(_ssm_kernel, nc=nc),
        grid=(D_SSM // LANES, bsz),
        in_specs=[pl.BlockSpec((seq, LANES), lambda j, b: (b, j)),
                  pl.BlockSpec((None, gpb, k, k), lambda j, b: (layer, j, 0, 0)),
                  pl.BlockSpec((None, npair, 2 * k, 2 * k), lambda j, b: (layer, j, 0, 0)),
                  pl.BlockSpec((None, npair, 2 * k, 2 * k), lambda j, b: (layer, j, 0, 0)),
                  pl.BlockSpec((None, npair, 2, 8, SUBLANES, LANES),
                               lambda j, b: (layer, j, 0, 0, 0, 0))],
        out_specs=pl.BlockSpec((seq, LANES), lambda j, b: (b, j)),
        out_shape=jax.ShapeDtypeStruct((bsz * seq, D_SSM), F32),
        scratch_shapes=[pltpu.VMEM((gpb, nc, k), BF16),
                        pltpu.VMEM((npair, nc, 2 * k), F32),
                        pltpu.VMEM((npair, nc, 2 * k), F32),
                        pltpu.VMEM((gpb, nc, k), F32)],
        compiler_params=pltpu.CompilerParams(
            dimension_semantics=("parallel", "parallel"), vmem_limit_bytes=VMEM_LIMIT),
        name="ssm_chunk",
    )(zs, m_intra, b_pair, c_pair, tab)


def _ssm_tables(lam_re, lam_im, log_dt, b_re, b_im, c_re, c_im, d_skip):
    t, h, p, g = CHUNK, SSM_GROUP, SSM_STATE, SSM_GROUPS
    nq = g // 2
    dt = jnp.exp(log_dt.astype(F32))[..., None]
    are = lam_re.astype(F32) * dt
    aim = lam_im.astype(F32) * dt

    def power(expo, a_re, a_im):
        mag = jnp.exp(a_re * expo)
        return mag * jnp.cos(a_im * expo), mag * jnp.sin(a_im * expo)

    lbr, lbi = power(1.0, are, aim)
    den = lam_re * lam_re + lam_im * lam_im
    cr = ((lbr - 1.0) * lam_re + lbi * lam_im) / den
    ci = (lbi * lam_re - (lbr - 1.0) * lam_im) / den
    bbr = cr[..., None] * b_re - ci[..., None] * b_im
    bbi = cr[..., None] * b_im + ci[..., None] * b_re

    def pair_rows(x):
        return x.reshape(2, 2, nq, 2 * p, x.shape[-1]).transpose(2, 0, 1, 3, 4)

    lane_pad = lambda x: jnp.pad(x, [(0, 0)] * (x.ndim - 1) + [(0, LANES - x.shape[-1])])
    steps = jnp.arange(t + 1, dtype=F32)
    pw = jnp.stack(power(steps, are[..., None], aim[..., None]))
    pl_tab = lane_pad(pair_rows(pw))
    pt_tab = jnp.pad(jnp.swapaxes(pair_rows(pw), -1, -2),
                     [(0, 0)] * 3 + [(0, LANES - t - 1), (0, 0)])
    ct_tab = lane_pad(pair_rows(jnp.stack([jnp.swapaxes(c_re.astype(F32), -1, -2),
                                           jnp.swapaxes(c_im.astype(F32), -1, -2)])))
    bt = jnp.stack([jnp.swapaxes(bbr, -1, -2), jnp.swapaxes(bbi, -1, -2)])
    bt = bt.reshape(2, 2, nq, 2, h, p)
    zero = jnp.zeros_like(bt[:, :, :, 0])
    bt_tab = jnp.stack([jnp.concatenate([bt[:, :, :, 0], zero], -1),
                        jnp.concatenate([zero, bt[:, :, :, 1]], -1)], axis=3)
    bt_tab = bt_tab.transpose(2, 0, 1, 3, 4, 5)
    dsk = jnp.pad(d_skip.astype(F32), ((0, 0), (0, t * h - h))).reshape(nq, 2, 1, t * h)

    r8 = np.arange(SUBLANES)
    expo = np.stack([np.stack([np.full(SUBLANES, k), np.full(SUBLANES, k)]) for k in (1, 2, 4)]
                    + [np.stack([r8 + 1, SUBLANES - r8])])
    expo = jnp.asarray(t * expo, F32)[:, :, None, :, None]
    ar, ai = power(expo, are.reshape(2, nq, 1, 2 * p)[None], aim.reshape(2, nq, 1, 2 * p)[None])
    live = np.stack([np.stack([r8 >= k, r8 <= SUBLANES - 1 - k]) for k in (1, 2, 4)]
                    + [np.ones((2, SUBLANES), bool)])
    live = jnp.asarray(live, F32)[:, :, None, :, None]
    ar, ai = ar * live, ai * live
    tab = jnp.stack([ar, ai], axis=1).reshape(8, 2, nq, SUBLANES, 2 * p)
    tab = tab.transpose(2, 1, 0, 3, 4)
    return pl_tab, pt_tab, ct_tab, bt_tab, dsk, tab


def _select_lanes(x, sel):
    hi = x.astype(BF16)
    lo = (x - hi.astype(F32)).astype(BF16)
    return _dot(hi, sel) + _dot(lo, sel)


def _select_rows(sel, x):
    hi = x.astype(BF16)
    lo = (x - hi.astype(F32)).astype(BF16)
    return _dot(sel, hi) + _dot(sel, lo)


def _roll_two_vregs(x, r):
    x0, x1 = x[:, :LANES], x[:, LANES:]
    if r >= LANES:
        x0, x1, r = x1, x0, r - LANES
    if r == 0:
        return jnp.concatenate([x0, x1], axis=1)
    lane = jax.lax.broadcasted_iota(jnp.int32, x0.shape, 1)
    a0, a1 = pltpu.roll(x0, r, axis=1), pltpu.roll(x1, r, axis=1)
    return jnp.concatenate([jnp.where(lane < r, a1, a0), jnp.where(lane < r, a0, a1)], axis=1)


def _ssm_prep_kernel(pl_ref, pt_ref, ct_ref, bt_ref, dsk_ref, sl_ref, sr_ref, e_ref,
                     m_ref, b_ref, c_ref):
    t, h = CHUNK, SSM_GROUP
    kk = t * h
    lane = jax.lax.broadcasted_iota(jnp.int32, (h, kk), 1)
    rowi = jax.lax.broadcasted_iota(jnp.int32, (h, kk), 0)
    half = jax.lax.broadcasted_iota(jnp.int32, (LANES, kk), 0) // SSM_STATE

    cl, cq = [], []
    for d in range(2):
        ctr, cti = _select_lanes(ct_ref[0, d], e_ref[...]), _select_lanes(ct_ref[1, d], e_ref[...])
        for out, pat in ((cl, d), (cq, 2 + d)):
            pr, pi = _select_lanes(pl_ref[0, d], sl_ref[pat]), _select_lanes(pl_ref[1, d], sl_ref[pat])
            out.append((ctr * pr - cti * pi, ctr * pi + cti * pr))

    for d in range(2):
        for ri, val in enumerate((cq[d][0], -cq[d][1])):
            r0 = (2 * d + ri) * LANES
            for gi in range(2):
                c_ref[r0:r0 + LANES, gi * kk:(gi + 1) * kk] = (
                    jnp.where(half == gi, val, 0.0).astype(BF16))

    for d in range(2):
        pwr, pwi = _select_rows(sr_ref[d], pt_ref[0, d]), _select_rows(sr_ref[d], pt_ref[1, d])
        for gi in range(2):
            btr = jnp.concatenate([bt_ref[0, d, gi]] * t, axis=0)
            bti = jnp.concatenate([bt_ref[1, d, gi]] * t, axis=0)
            r0, c0 = gi * kk, 2 * d * LANES
            b_ref[r0:r0 + kk, c0:c0 + LANES] = (btr * pwr - bti * pwi).astype(BF16)
            b_ref[r0:r0 + kk, c0 + LANES:c0 + 2 * LANES] = (btr * pwi + bti * pwr).astype(BF16)

    for gi in range(2):
        gen = [_dot(bt_ref[0, d, gi].astype(BF16), cl[d][0].astype(BF16))
               - _dot(bt_ref[1, d, gi].astype(BF16), cl[d][1].astype(BF16)) for d in range(2)]
        diag = jnp.where(lane < h, gen[1], 0.0) + jnp.where(rowi == lane, dsk_ref[gi], 0.0)
        kf, kb = gen[0] + diag, gen[1]
        for s in range(t):
            blk = jnp.where(lane >= h * s, _roll_two_vregs(kf, h * s), _roll_two_vregs(kb, h * s))
            m_ref[gi, s * h:(s + 1) * h, :] = blk.astype(BF16)


def _ssm_prep_constants():
    t, h = CHUNK, SSM_GROUP
    kk = t * h
    step = np.arange(kk) // h
    k = np.arange(LANES)
    pats = [step, (t - step) % t, step + 1, t - step]
    sel_lane = np.stack([(k[:, None] == pat[None, :]) for pat in pats]).astype(np.float32)
    sel_row = np.stack([(pat[:, None] == k[None, :]) for pat in (t - 1 - step, step)])
    tile_ho = (k[:, None] == (np.arange(kk) % h)[None, :])
    as_bf16 = lambda a: jnp.asarray(a, F32).astype(BF16)
    return as_bf16(sel_lane), as_bf16(sel_row), as_bf16(tile_ho)


def _ssm_prep(pl_tab, pt_tab, ct_tab, bt_tab, dsk):
    depth, nq = pl_tab.shape[:2]
    kk = CHUNK * SSM_GROUP
    sel_lane, sel_row, tile_ho = _ssm_prep_constants()
    tab5 = pl.BlockSpec((None, None, 2, 2, LANES, LANES), lambda l, q: (l, q, 0, 0, 0, 0))
    return pl.pallas_call(
        _ssm_prep_kernel,
        grid=(depth, nq),
        in_specs=[tab5, tab5, tab5,
                  pl.BlockSpec((None, None, 2, 2, 2, SSM_GROUP, LANES),
                               lambda l, q: (l, q, 0, 0, 0, 0, 0)),
                  pl.BlockSpec((None, None, 2, 1, kk), lambda l, q: (l, q, 0, 0, 0)),
                  _const_spec(sel_lane.shape), _const_spec(sel_row.shape),
                  _const_spec(tile_ho.shape)],
        out_specs=[pl.BlockSpec((None, 2, kk, kk), lambda l, q: (l, q, 0, 0)),
                   pl.BlockSpec((None, None, 2 * kk, 2 * kk), lambda l, q: (l, q, 0, 0)),
                   pl.BlockSpec((None, None, 2 * kk, 2 * kk), lambda l, q: (l, q, 0, 0))],
        out_shape=[jax.ShapeDtypeStruct((depth, 2 * nq, kk, kk), BF16),
                   jax.ShapeDtypeStruct((depth, nq, 2 * kk, 2 * kk), BF16),
                   jax.ShapeDtypeStruct((depth, nq, 2 * kk, 2 * kk), BF16)],
        compiler_params=pltpu.CompilerParams(
            dimension_semantics=("parallel", "parallel"), vmem_limit_bytes=VMEM_LIMIT),
        name="ssm_prep",
    )(pl_tab, pt_tab, ct_tab, bt_tab, dsk, sel_lane, sel_row, tile_ho)


def _mix_mlp_kernel(x_ref, gr_ref, gi_ref, ys_ref, bdr_ref, bdi_ref, fg_ref, wg_ref, bg_ref,
                    sg_ref, wo_ref, pmg_ref, plg_ref, w1_ref, w2_ref, pog_ref, o_ref, *, ff_blk):
    gr, gi = gr_ref[...], gi_ref[...]
    yf = jnp.concatenate(
        [_dot(gr[:, c:c + MXU_DIM], bdr_ref[c:c + MXU_DIM, c:c + MXU_DIM])
         + _dot(gi[:, c:c + MXU_DIM], bdi_ref[c:c + MXU_DIM, c:c + MXU_DIM])
         for c in range(0, D_FOURIER, MXU_DIM)], axis=1)
    yf = _rms(yf, fg_ref[...])
    y = _unrotate_rows(ys_ref[...])
    gl = jax.nn.gelu(y, approximate=True)
    gate = _dot(gl.astype(BF16), wg_ref[...]) + bg_ref[...]
    ys = _rms(gl * jax.nn.sigmoid(gate), sg_ref[...])
    cat = jnp.concatenate([yf.astype(BF16), ys.astype(BF16)], axis=-1)
    x1 = x_ref[...] + _rms(_dot(cat, wo_ref[...]), pmg_ref[...])
    h = _rms(x1, plg_ref[...]).astype(BF16)
    acc = jnp.zeros(x1.shape, F32)
    for j in range(D_FF // ff_blk):
        a = _dot(h, w1_ref[:, j * ff_blk:(j + 1) * ff_blk])
        a = jnp.square(jnp.maximum(a, 0.0)).astype(BF16)
        acc = acc + _dot(a, w2_ref[j * ff_blk:(j + 1) * ff_blk, :])
    o_ref[...] = x1 + _rms(acc, pog_ref[...])


def _mix_mlp(x2d, gri, ys, bdr, bdi, fg, wg, bg, sg, wo, pmg, plg, w1, w2, pog, tm, ff_blk, layer):
    n = x2d.shape[0]
    row = lambda i: (i, 0)
    lspec = functools.partial(_layer_spec, layer=layer)
    return pl.pallas_call(
        functools.partial(_mix_mlp_kernel, ff_blk=ff_blk),
        grid=(n // tm,),
        in_specs=[pl.BlockSpec((tm, D_MODEL), row),
                  pl.BlockSpec((None, tm, D_FOURIER), lambda i: (0, i, 0)),
                  pl.BlockSpec((None, tm, D_FOURIER), lambda i: (1, i, 0)),
                  pl.BlockSpec((tm, D_SSM), row),
                  lspec((D_FOURIER, D_FOURIER)), lspec((D_FOURIER, D_FOURIER)),
                  lspec((1, D_FOURIER)),
                  lspec((D_SSM, D_SSM)), lspec((1, D_SSM)), lspec((1, D_SSM)),
                  lspec((D_MODEL, D_MODEL)), lspec((1, D_MODEL)),
                  lspec((1, D_MODEL)),
                  lspec((D_MODEL, D_FF)), lspec((D_FF, D_MODEL)),
                  lspec((1, D_MODEL))],
        out_specs=pl.BlockSpec((tm, D_MODEL), row),
        out_shape=jax.ShapeDtypeStruct((n, D_MODEL), F32),
        compiler_params=pltpu.CompilerParams(
            dimension_semantics=("parallel",), vmem_limit_bytes=VMEM_LIMIT),
        name="mix_mlp",
    )(x2d, gri, gri, ys, bdr, bdi, fg, wg, bg, sg, wo, pmg, plg, w1, w2, pog)


def _fourier_channel_maps(w_f, seq):
    c = np.arange(FOURIER_HEAD_DIM)
    ang = 2.0 * np.pi * ((c[:, None] * c[None, :]) % FOURIER_HEAD_DIM) / FOURIER_HEAD_DIM
    scale = 1.0 / math.sqrt(seq * FOURIER_HEAD_DIM)
    hp = jax.lax.Precision.HIGHEST
    cw = jnp.einsum("cd,hde->hce", jnp.asarray(np.cos(ang) * scale, F32), w_f.astype(F32), precision=hp)
    sw = jnp.einsum("cd,hde->hce", jnp.asarray(np.sin(ang) * scale, F32), w_f.astype(F32), precision=hp)
    eye = jnp.eye(FOURIER_HEADS, dtype=F32)
    bd = lambda m: jnp.einsum("hce,hk->hcke", m, eye).reshape(D_FOURIER, D_FOURIER)
    return bd(cw).astype(BF16), bd(sw).astype(BF16)


def kernel(x, w_in, w_out, pre_mix_g, post_mix_g, pre_mlp_g, post_mlp_g, fourier_out_g, ssm_out_g,
           w_fourier, lam_re, lam_im, log_dt, b_re, b_im, c_re, c_im, d_skip, w_glu, b_glu,
           w_ff1, w_ff2):
    bsz, seq, _ = x.shape
    depth = w_in.shape[0]
    n = bsz * seq
    n1, n2 = FFT_N1, seq // FFT_N1
    nc = seq // CHUNK
    assert seq % (FFT_N1 * N2_BLK) == 0 and nc % RELAYOUT_CHUNKS == 0
    tm = min(1024, n)
    wk_tab, m2_tab = _fft_tables(seq)

    rows = lambda v: v.reshape(depth, 1, -1).astype(F32)
    w_in_b, w_out_b, w_glu_b = w_in.astype(BF16), w_out.astype(BF16), w_glu.astype(BF16)
    w_ff1_b, w_ff2_b = w_ff1.astype(BF16), w_ff2.astype(BF16)
    pl_tab, pt_tab, ct_tab, bt_tab, dsk, tab = jax.vmap(_ssm_tables)(
        lam_re, lam_im, log_dt, b_re, b_im, c_re, c_im, d_skip)
    m_intra, b_pair, c_pair = _ssm_prep(pl_tab, pt_tab, ct_tab, bt_tab, dsk)
    bdr, bdi = jax.vmap(functools.partial(_fourier_channel_maps, seq=seq))(w_fourier)
    gains = [rows(v) for v in (pre_mix_g, fourier_out_g, b_glu, ssm_out_g, post_mix_g,
                               pre_mlp_g, post_mlp_g)]
    pre_mix, four_g, glu_b, ssm_g, post_mix, pre_mlp, post_mlp = gains

    x2d = x.reshape(n, D_MODEL).astype(F32)
    for i in range(depth):
        y, zs = _in_proj(x2d.reshape(bsz, n1, n2, D_MODEL), pre_mix, w_in_b, wk_tab, i)
        zs = zs.reshape(n, D_SSM)
        g = _fft_stage2(y.reshape(bsz, n1, 2 * n2, D_FOURIER), m2_tab, kb=2 * SUBLANES)
        gri = g.reshape(2, n, D_FOURIER)
        ysr = _ssm_chunked(zs, m_intra, b_pair, c_pair, tab, bsz, seq, i)
        x2d = _mix_mlp(x2d, gri, ysr, bdr, bdi, four_g, w_glu_b, glu_b, ssm_g, w_out_b,
                       post_mix, pre_mlp, w_ff1_b, w_ff2_b, post_mlp, tm, ff_blk=1024, layer=i)
    return x2d.reshape(bsz, seq, D_MODEL).astype(x.dtype)
```

```python
import functools
import math

import jax
import jax.numpy as jnp
import numpy as np
from jax.experimental import pallas as pl
from jax.experimental.pallas import tpu as pltpu

EPS = 1e-6
D_MODEL = 1024
D_FOURIER = 512
D_SSM = 512
FOURIER_HEADS = 8
FOURIER_HEAD_DIM = 64
SSM_GROUP = 16
SSM_GROUPS = 32
SSM_STATE = 64
D_FF = 4096

FFT_N1 = 64
CHUNK = 16
LANES = 128
SUBLANES = 8
MXU_DIM = 256
VMEM_LIMIT = 56 * 1024 * 1024

BF16 = jnp.bfloat16
F32 = jnp.float32


def _rms(x, g):
    return x * jax.lax.rsqrt(jnp.mean(x * x, axis=-1, keepdims=True) + EPS) * g


def _dot(a, b):
    return jnp.dot(a, b, preferred_element_type=F32)


def _const_spec(shape):
    nd = len(shape)
    return pl.BlockSpec(shape, lambda *_: (0,) * nd, pipeline_mode=pl.Buffered(1))


def _layer_spec(shape, layer):
    nd = len(shape)
    return pl.BlockSpec((None,) + tuple(shape), lambda *_: (layer,) + (0,) * nd,
                        pipeline_mode=pl.Buffered(1))


KRON_BLK = SUBLANES
N2_BLK = 2 * KRON_BLK


def _in_proj_kernel(x_ref, g_ref, w_ref, wk_ref, y_ref, zs_ref):
    n1 = x_ref.shape[0]
    x = x_ref[...].reshape(n1 * N2_BLK, D_MODEL)
    h = _rms(x, g_ref[...]).astype(BF16)
    z = _dot(h, w_ref[...])
    zf = z[:, :D_FOURIER].reshape(n1, N2_BLK // KRON_BLK, KRON_BLK, D_FOURIER)
    parts = []
    for a in range(N2_BLK // KRON_BLK):
        za = zf[:, a].reshape(n1 * KRON_BLK, D_FOURIER).astype(BF16)
        ya = _dot(wk_ref[...], za)
        parts.append(ya.reshape(2 * n1, KRON_BLK, D_FOURIER))
    y = jnp.concatenate(parts, axis=1)
    y_ref[...] = y.astype(BF16).reshape(n1, 2, N2_BLK, D_FOURIER)
    zs_ref[...] = _rotate_rows(z[:, D_FOURIER:]).reshape(n1, N2_BLK, D_SSM)


def _in_proj(x4, g, w_bf16, wk, layer):
    bsz, n1, n2, _ = x4.shape
    rows = n1 * KRON_BLK
    return pl.pallas_call(
        _in_proj_kernel,
        grid=(bsz, n2 // N2_BLK),
        in_specs=[pl.BlockSpec((None, n1, N2_BLK, D_MODEL), lambda b, j: (b, 0, j, 0)),
                  _layer_spec((1, D_MODEL), layer),
                  _layer_spec((D_MODEL, D_MODEL), layer),
                  _const_spec((2 * rows, rows))],
        out_specs=[pl.BlockSpec((None, n1, 2, N2_BLK, D_FOURIER), lambda b, j: (b, 0, 0, j, 0)),
                   pl.BlockSpec((None, n1, N2_BLK, D_SSM), lambda b, j: (b, 0, j, 0))],
        out_shape=[jax.ShapeDtypeStruct((bsz, n1, 2, n2, D_FOURIER), BF16),
                   jax.ShapeDtypeStruct((bsz, n1, n2, D_SSM), F32)],
        compiler_params=pltpu.CompilerParams(
            dimension_semantics=("parallel", "parallel"), vmem_limit_bytes=VMEM_LIMIT),
        name="in_proj",
    )(x4, g, w_bf16, wk)


REGROUP_UNROLL = 8


def _fft2_kernel(m_ref, y_ref, g_ref, z_ref, *, kb, n2, pitch):
    nslab = D_FOURIER // LANES
    for j in range(kb):
        z = _dot(m_ref[j], y_ref[j])
        for ri in range(2):
            for s in range(nslab):
                z_ref[ri, s, j * pitch:j * pitch + n2, :] = (
                    z[ri * n2:(ri + 1) * n2, s * LANES:(s + 1) * LANES])

    def regroup(i, _):
        for u in range(REGROUP_UNROLL):
            k2 = i * REGROUP_UNROLL + u
            for ri in range(2):
                for s in range(nslab):
                    rows = [z_ref[ri, s, pl.ds(k2 + a * SUBLANES * pitch, SUBLANES, stride=pitch), :]
                            for a in range(kb // SUBLANES)]
                    g_ref[ri, k2, :, s * LANES:(s + 1) * LANES] = (
                        jnp.concatenate(rows, axis=0).astype(BF16))
        return 0

    jax.lax.fori_loop(0, n2 // REGROUP_UNROLL, regroup, 0)


def _fft_stage2(y, m2, kb):
    b, n1, rows, c = y.shape
    n2 = rows // 2
    pitch = n2 + SUBLANES
    return pl.pallas_call(
        functools.partial(_fft2_kernel, kb=kb, n2=n2, pitch=pitch),
        grid=(n1 // kb, b),
        in_specs=[pl.BlockSpec((kb, rows, rows), lambda j, i: (j, 0, 0)),
                  pl.BlockSpec((None, kb, rows, c), lambda j, i: (i, j, 0, 0))],
        out_specs=pl.BlockSpec((2, None, n2, kb, c), lambda j, i: (0, i, 0, j, 0)),
        out_shape=jax.ShapeDtypeStruct((2, b, n2, n1, c), BF16),
        scratch_shapes=[pltpu.VMEM((2, c // LANES, kb * pitch, LANES), F32)],
        compiler_params=pltpu.CompilerParams(
            dimension_semantics=("parallel", "parallel"), vmem_limit_bytes=VMEM_LIMIT),
        name="fft_stage2",
    )(m2, y)


def _fft_tables(seq):
    n1, n2 = FFT_N1, seq // FFT_N1
    k1 = np.arange(n1)
    ang1 = 2.0 * np.pi * ((k1[:, None] * k1[None, :]) % n1) / n1
    w1 = np.stack([np.cos(ang1), -np.sin(ang1)], axis=1).reshape(2 * n1, n1)
    wk = np.einsum("kn,ab->kanb", w1, np.eye(KRON_BLK)).reshape(2 * n1 * KRON_BLK, n1 * KRON_BLK)
    k2 = np.arange(n2)
    freq = k1[:, None, None] + n1 * k2[None, :, None]
    ang2 = 2.0 * np.pi * ((freq * k2[None, None, :]) % seq) / seq
    mr, mi = np.cos(ang2), -np.sin(ang2)
    m2 = np.concatenate([np.concatenate([mr, -mi], axis=2),
                         np.concatenate([mi, mr], axis=2)], axis=1)
    return jnp.asarray(wk, F32).astype(BF16), jnp.asarray(m2, F32).astype(BF16)


GROUPS_PER_BLOCK = LANES // SSM_GROUP
PAIRS_PER_BLOCK = GROUPS_PER_BLOCK // 2
RELAYOUT_CHUNKS = 8 * SUBLANES


def _rotate_rows(v):
    return jnp.concatenate(
        [pltpu.roll(v[:, c:c + LANES], 0, axis=1, stride=SSM_GROUP, stride_axis=0)
         for c in range(0, v.shape[1], LANES)], axis=1)


def _unrotate_rows(v):
    row = jax.lax.broadcasted_iota(jnp.int32, v.shape, 0)
    cols = []
    for c in range(0, v.shape[1], LANES):
        w = v[:, c:c + LANES]
        for bit in (1, 2, 4):
            w = jnp.where((row[:, :LANES] & bit) != 0,
                          pltpu.roll(w, LANES - SSM_GROUP * bit, axis=1), w)
        cols.append(w)
    return jnp.concatenate(cols, axis=1)


def _ssm_kernel(z_ref, mi_ref, bs_ref, cs_ref, tab_ref, y_ref,
                u2_ref, s_ref, car_ref, yg_ref, *, nc):
    gpb, npair = GROUPS_PER_BLOCK, PAIRS_PER_BLOCK
    t16 = RELAYOUT_CHUNKS
    rows_it = t16 * CHUNK
    lane_blk = jax.lax.broadcasted_iota(jnp.int32, (t16, LANES), 1) // SSM_GROUP
    masks = [lane_blk == m for m in range(gpb)]

    def relayout_in(i, _):
        r0 = pl.multiple_of(i * rows_it, rows_it)
        ut = [z_ref[pl.ds(r0 + t, t16, stride=CHUNK), :] for t in range(CHUNK)]
        c0 = pl.multiple_of(i * t16, t16)
        for g in range(gpb):
            for j in range(CHUNK // gpb):
                acc = ut[gpb * j]
                for tt in range(1, gpb):
                    acc = jnp.where(masks[(g + tt) % gpb], ut[gpb * j + tt], acc)
                if g:
                    acc = pltpu.roll(acc, LANES - SSM_GROUP * g, axis=1)
                u2_ref[g, pl.ds(c0, t16), j * LANES:(j + 1) * LANES] = acc.astype(BF16)
        return 0

    jax.lax.fori_loop(0, nc // t16, relayout_in, 0)

    for q in range(npair):
        up = jnp.concatenate([u2_ref[2 * q], u2_ref[2 * q + 1]], axis=1)
        s_ref[q] = _dot(up, bs_ref[q])

    row = jax.lax.broadcasted_iota(jnp.int32, (SUBLANES, LANES), 0)
    nblk = nc // SUBLANES

    def scan_block(tab, sr, si, cr, ci, down):
        keep = (lambda sh: row >= sh) if down else (lambda sh: row <= SUBLANES - 1 - sh)
        amt = (lambda sh: sh) if down else (lambda sh: SUBLANES - sh)
        for k, sh in enumerate((1, 2, 4)):
            tr = pltpu.roll(sr, amt(sh), axis=0)
            ti = pltpu.roll(si, amt(sh), axis=0)
            ar, ai = tab[2 * k], tab[2 * k + 1]
            sr, si = sr + (ar * tr - ai * ti), si + (ar * ti + ai * tr)
        fr = sr + (tab[6] * cr - tab[7] * ci)
        fi = si + (tab[6] * ci + tab[7] * cr)
        outr = jnp.where(keep(1), pltpu.roll(fr, amt(1), axis=0), cr)
        outi = jnp.where(keep(1), pltpu.roll(fi, amt(1), axis=0), ci)
        edge = SUBLANES - 1 if down else 0
        ncr = jnp.broadcast_to(fr[edge:edge + 1, :], (SUBLANES, LANES))
        nci = jnp.broadcast_to(fi[edge:edge + 1, :], (SUBLANES, LANES))
        return outr, outi, ncr, nci

    def scan_body(i, carry):
        new = []
        for q in range(npair):
            for d in range(2):
                blk = i if d == 0 else nblk - 1 - i
                r0 = pl.multiple_of(blk * SUBLANES, SUBLANES)
                lo = 2 * d * LANES
                tab = [tab_ref[q, d, k] for k in range(8)]
                cr, ci = carry[4 * q + 2 * d], carry[4 * q + 2 * d + 1]
                outr, outi, cr, ci = scan_block(
                    tab, s_ref[q, pl.ds(r0, SUBLANES), lo:lo + LANES],
                    s_ref[q, pl.ds(r0, SUBLANES), lo + LANES:lo + 2 * LANES], cr, ci, d == 0)
                car_ref[q, pl.ds(r0, SUBLANES), lo:lo + LANES] = outr
                car_ref[q, pl.ds(r0, SUBLANES), lo + LANES:lo + 2 * LANES] = outi
                new += [cr, ci]
        return tuple(new)

    zero = jnp.zeros((SUBLANES, LANES), F32)
    jax.lax.fori_loop(0, nblk, scan_body, (zero,) * (4 * npair))

    kk = CHUNK * SSM_GROUP
    for q in range(npair):
        yi = _dot(car_ref[q].astype(BF16), cs_ref[q])
        for e in range(2):
            g = 2 * q + e
            yg_ref[g] = _dot(u2_ref[g], mi_ref[g]) + yi[:, e * kk:(e + 1) * kk]

    def relayout_out(i, _):
        c0 = pl.multiple_of(i * t16, t16)
        r0 = pl.multiple_of(i * rows_it, rows_it)
        yrot = []
        for g in range(gpb):
            halves = []
            for j in range(CHUNK // gpb):
                v = yg_ref[g, pl.ds(c0, t16), j * LANES:(j + 1) * LANES]
                halves.append(pltpu.roll(v, SSM_GROUP * g, axis=1) if g else v)
            yrot.append(halves)
        for t in range(CHUNK):
            j, tt = divmod(t, gpb)
            acc = yrot[0][j]
            for g in range(1, gpb):
                acc = jnp.where(masks[(g + tt) % gpb], yrot[g][j], acc)
            y_ref[pl.ds(r0 + t, t16, stride=CHUNK), :] = acc
        return 0

    jax.lax.fori_loop(0, nc // t16, relayout_out, 0)


def _ssm_chunked(zs, m_intra, b_pair, c_pair, tab, bsz, seq, layer):
    nc = seq // CHUNK
    gpb, npair = GROUPS_PER_BLOCK, PAIRS_PER_BLOCK
    k = CHUNK * SSM_GROUP
    return pl.pallas_call(
        functools.partial(_ssm_kernel, nc=nc),
        grid=(D_SSM // LANES, bsz),
        in_specs=[pl.BlockSpec((seq, LANES), lambda j, b: (b, j)),
                  pl.BlockSpec((None, gpb, k, k), lambda j, b: (layer, j, 0, 0)),
                  pl.BlockSpec((None, npair, 2 * k, 2 * k), lambda j, b: (layer, j, 0, 0)),
                  pl.BlockSpec((None, npair, 2 * k, 2 * k), lambda j, b: (layer, j, 0, 0)),
                  pl.BlockSpec((None, npair, 2, 8, SUBLANES, LANES),
                               lambda j, b: (layer, j, 0, 0, 0, 0))],
        out_specs=pl.BlockSpec((seq, LANES), lambda j, b: (b, j)),
        out_shape=jax.ShapeDtypeStruct((bsz * seq, D_SSM), F32),
        scratch_shapes=[pltpu.VMEM((gpb, nc, k), BF16),
                        pltpu.VMEM((npair, nc, 2 * k), F32),
                        pltpu.VMEM((npair, nc, 2 * k), F32),
                        pltpu.VMEM((gpb, nc, k), F32)],
        compiler_params=pltpu.CompilerParams(
            dimension_semantics=("parallel", "parallel"), vmem_limit_bytes=VMEM_LIMIT),
        name="ssm_chunk",
    )(zs, m_intra, b_pair, c_pair, tab)


def _ssm_tables(lam_re, lam_im, log_dt, b_re, b_im, c_re, c_im, d_skip):
    t, h, p, g = CHUNK, SSM_GROUP, SSM_STATE, SSM_GROUPS
    nq = g // 2
    dt = jnp.exp(log_dt.astype(F32))[..., None]
    are = lam_re.astype(F32) * dt
    aim = lam_im.astype(F32) * dt

    def power(expo, a_re, a_im):
        mag = jnp.exp(a_re * expo)
        return mag * jnp.cos(a_im * expo), mag * jnp.sin(a_im * expo)

    lbr, lbi = power(1.0, are, aim)
    den = lam_re * lam_re + lam_im * lam_im
    cr = ((lbr - 1.0) * lam_re + lbi * lam_im) / den
    ci = (lbi * lam_re - (lbr - 1.0) * lam_im) / den
    bbr = cr[..., None] * b_re - ci[..., None] * b_im
    bbi = cr[..., None] * b_im + ci[..., None] * b_re

    def pair_rows(x):
        return x.reshape(2, 2, nq, 2 * p, x.shape[-1]).transpose(2, 0, 1, 3, 4)

    lane_pad = lambda x: jnp.pad(x, [(0, 0)] * (x.ndim - 1) + [(0, LANES - x.shape[-1])])
    steps = jnp.arange(t + 1, dtype=F32)
    pw = jnp.stack(power(steps, are[..., None], aim[..., None]))
    pl_tab = lane_pad(pair_rows(pw))
    pt_tab = jnp.pad(jnp.swapaxes(pair_rows(pw), -1, -2),
                     [(0, 0)] * 3 + [(0, LANES - t - 1), (0, 0)])
    ct_tab = lane_pad(pair_rows(jnp.stack([jnp.swapaxes(c_re.astype(F32), -1, -2),
                                           jnp.swapaxes(c_im.astype(F32), -1, -2)])))
    bt = jnp.stack([jnp.swapaxes(bbr, -1, -2), jnp.swapaxes(bbi, -1, -2)])
    bt = bt.reshape(2, 2, nq, 2, h, p)
    zero = jnp.zeros_like(bt[:, :, :, 0])
    bt_tab = jnp.stack([jnp.concatenate([bt[:, :, :, 0], zero], -1),
                        jnp.concatenate([zero, bt[:, :, :, 1]], -1)], axis=3)
    bt_tab = bt_tab.transpose(2, 0, 1, 3, 4, 5)
    dsk = jnp.pad(d_skip.astype(F32), ((0, 0), (0, t * h - h))).reshape(nq, 2, 1, t * h)

    r8 = np.arange(SUBLANES)
    expo = np.stack([np.stack([np.full(SUBLANES, k), np.full(SUBLANES, k)]) for k in (1, 2, 4)]
                    + [np.stack([r8 + 1, SUBLANES - r8])])
    expo = jnp.asarray(t * expo, F32)[:, :, None, :, None]
    ar, ai = power(expo, are.reshape(2, nq, 1, 2 * p)[None], aim.reshape(2, nq, 1, 2 * p)[None])
    live = np.stack([np.stack([r8 >= k, r8 <= SUBLANES - 1 - k]) for k in (1, 2, 4)]
                    + [np.ones((2, SUBLANES), bool)])
    live = jnp.asarray(live, F32)[:, :, None, :, None]
    ar, ai = ar * live, ai * live
    tab = jnp.stack([ar, ai], axis=1).reshape(8, 2, nq, SUBLANES, 2 * p)
    tab = tab.transpose(2, 1, 0, 3, 4)
    return pl_tab, pt_tab, ct_tab, bt_tab, dsk, tab


def _select_lanes(x, sel):
    hi = x.astype(BF16)
    lo = (x - hi.astype(F32)).astype(BF16)
    return _dot(hi, sel) + _dot(lo, sel)


def _select_rows(sel, x):
    hi = x.astype(BF16)
    lo = (x - hi.astype(F32)).astype(BF16)
    return _dot(sel, hi) + _dot(sel, lo)


def _roll_two_vregs(x, r):
    x0, x1 = x[:, :LANES], x[:, LANES:]
    if r >= LANES:
        x0, x1, r = x1, x0, r - LANES
    if r == 0:
        return jnp.concatenate([x0, x1], axis=1)
    lane = jax.lax.broadcasted_iota(jnp.int32, x0.shape, 1)
    a0, a1 = pltpu.roll(x0, r, axis=1), pltpu.roll(x1, r, axis=1)
    return jnp.concatenate([jnp.where(lane < r, a1, a0), jnp.where(lane < r, a0, a1)], axis=1)


def _ssm_prep_kernel(pl_ref, pt_ref, ct_ref, bt_ref, dsk_ref, sl_ref, sr_ref, e_ref,
                     m_ref, b_ref, c_ref):
    t, h = CHUNK, SSM_GROUP
    kk = t * h
    lane = jax.lax.broadcasted_iota(jnp.int32, (h, kk), 1)
    rowi = jax.lax.broadcasted_iota(jnp.int32, (h, kk), 0)
    half = jax.lax.broadcasted_iota(jnp.int32, (LANES, kk), 0) // SSM_STATE

    cl, cq = [], []
    for d in range(2):
        ctr, cti = _select_lanes(ct_ref[0, d], e_ref[...]), _select_lanes(ct_ref[1, d], e_ref[...])
        for out, pat in ((cl, d), (cq, 2 + d)):
            pr, pi = _select_lanes(pl_ref[0, d], sl_ref[pat]), _select_lanes(pl_ref[1, d], sl_ref[pat])
            out.append((ctr * pr - cti * pi, ctr * pi + cti * pr))

    for d in range(2):
        for ri, val in enumerate((cq[d][0], -cq[d][1])):
            r0 = (2 * d + ri) * LANES
            for gi in range(2):
                c_ref[r0:r0 + LANES, gi * kk:(gi + 1) * kk] = (
                    jnp.where(half == gi, val, 0.0).astype(BF16))

    for d in range(2):
        pwr, pwi = _select_rows(sr_ref[d], pt_ref[0, d]), _select_rows(sr_ref[d], pt_ref[1, d])
        for gi in range(2):
            btr = jnp.concatenate([bt_ref[0, d, gi]] * t, axis=0)
            bti = jnp.concatenate([bt_ref[1, d, gi]] * t, axis=0)
            r0, c0 = gi * kk, 2 * d * LANES
            b_ref[r0:r0 + kk, c0:c0 + LANES] = (btr * pwr - bti * pwi).astype(BF16)
            b_ref[r0:r0 + kk, c0 + LANES:c0 + 2 * LANES] = (btr * pwi + bti * pwr).astype(BF16)

    for gi in range(2):
        gen = [_dot(bt_ref[0, d, gi].astype(BF16), cl[d][0].astype(BF16))
               - _dot(bt_ref[1, d, gi].astype(BF16), cl[d][1].astype(BF16)) for d in range(2)]
        diag = jnp.where(lane < h, gen[1], 0.0) + jnp.where(rowi == lane, dsk_ref[gi], 0.0)
        kf, kb = gen[0] + diag, gen[1]
        for s in range(t):
            blk = jnp.where(lane >= h * s, _roll_two_vregs(kf, h * s), _roll_two_vregs(kb, h * s))
            m_ref[gi, s * h:(s + 1) * h, :] = blk.astype(BF16)


def _ssm_prep_constants():
    t, h = CHUNK, SSM_GROUP
    kk = t * h
    step = np.arange(kk) // h
    k = np.arange(LANES)
    pats = [step, (t - step) % t, step + 1, t - step]
    sel_lane = np.stack([(k[:, None] == pat[None, :]) for pat in pats]).astype(np.float32)
    sel_row = np.stack([(pat[:, None] == k[None, :]) for pat in (t - 1 - step, step)])
    tile_ho = (k[:, None] == (np.arange(kk) % h)[None, :])
    as_bf16 = lambda a: jnp.asarray(a, F32).astype(BF16)
    return as_bf16(sel_lane), as_bf16(sel_row), as_bf16(tile_ho)


def _ssm_prep(pl_tab, pt_tab, ct_tab, bt_tab, dsk):
    depth, nq = pl_tab.shape[:2]
    kk = CHUNK * SSM_GROUP
    sel_lane, sel_row, tile_ho = _ssm_prep_constants()
    tab5 = pl.BlockSpec((None, None, 2, 2, LANES, LANES), lambda l, q: (l, q, 0, 0, 0, 0))
    return pl.pallas_call(
        _ssm_prep_kernel,
        grid=(depth, nq),
        in_specs=[tab5, tab5, tab5,
                  pl.BlockSpec((None, None, 2, 2, 2, SSM_GROUP, LANES),
                               lambda l, q: (l, q, 0, 0, 0, 0, 0)),
                  pl.BlockSpec((None, None, 2, 1, kk), lambda l, q: (l, q, 0, 0, 0)),
                  _const_spec(sel_lane.shape), _const_spec(sel_row.shape),
                  _const_spec(tile_ho.shape)],
        out_specs=[pl.BlockSpec((None, 2, kk, kk), lambda l, q: (l, q, 0, 0)),
                   pl.BlockSpec((None, None, 2 * kk, 2 * kk), lambda l, q: (l, q, 0, 0)),
                   pl.BlockSpec((None, None, 2 * kk, 2 * kk), lambda l, q: (l, q, 0, 0))],
        out_shape=[jax.ShapeDtypeStruct((depth, 2 * nq, kk, kk), BF16),
                   jax.ShapeDtypeStruct((depth, nq, 2 * kk, 2 * kk), BF16),
                   jax.ShapeDtypeStruct((depth, nq, 2 * kk, 2 * kk), BF16)],
        compiler_params=pltpu.CompilerParams(
            dimension_semantics=("parallel", "parallel"), vmem_limit_bytes=VMEM_LIMIT),
        name="ssm_prep",
    )(pl_tab, pt_tab, ct_tab, bt_tab, dsk, sel_lane, sel_row, tile_ho)


MIX_SUBBLOCKS = 2


def _mix_mlp_kernel(x_ref, gr_ref, gi_ref, ys_ref, bdr_ref, bdi_ref, fg_ref, wg_ref, bg_ref,
                    sg_ref, wo_ref, pmg_ref, plg_ref, w1_ref, w2_ref, pog_ref, o_ref, *, ff_blk,
                    nsub):
    rows = x_ref.shape[0] // nsub

    def mix(k):
        r = slice(k * rows, (k + 1) * rows)
        gr, gi = gr_ref[r, :], gi_ref[r, :]
        yf = jnp.concatenate(
            [_dot(gr[:, c:c + MXU_DIM], bdr_ref[c:c + MXU_DIM, c:c + MXU_DIM])
             + _dot(gi[:, c:c + MXU_DIM], bdi_ref[c:c + MXU_DIM, c:c + MXU_DIM])
             for c in range(0, D_FOURIER, MXU_DIM)], axis=1)
        yf = _rms(yf, fg_ref[...])
        y = _unrotate_rows(ys_ref[r, :])
        gl = jax.nn.gelu(y, approximate=True)
        gate = _dot(gl.astype(BF16), wg_ref[...]) + bg_ref[...]
        ys = _rms(gl * jax.nn.sigmoid(gate), sg_ref[...])
        cat = jnp.concatenate([yf.astype(BF16), ys.astype(BF16)], axis=-1)
        x1 = x_ref[r, :] + _rms(_dot(cat, wo_ref[...]), pmg_ref[...])
        return x1, _rms(x1, plg_ref[...]).astype(BF16)

    def mlp(k, x1, h):
        acc = jnp.zeros(x1.shape, F32)
        for j in range(D_FF // ff_blk):
            a = _dot(h, w1_ref[:, j * ff_blk:(j + 1) * ff_blk])
            a = jnp.square(jnp.maximum(a, 0.0)).astype(BF16)
            acc = acc + _dot(a, w2_ref[j * ff_blk:(j + 1) * ff_blk, :])
        o_ref[k * rows:(k + 1) * rows, :] = x1 + _rms(acc, pog_ref[...])

    nxt = mix(0)
    for k in range(nsub):
        cur = nxt
        if k + 1 < nsub:
            nxt = mix(k + 1)
        mlp(k, *cur)


def _mix_mlp(x2d, gri, ys, bdr, bdi, fg, wg, bg, sg, wo, pmg, plg, w1, w2, pog, tm, ff_blk, layer):
    n = x2d.shape[0]
    row = lambda i: (i, 0)
    lspec = functools.partial(_layer_spec, layer=layer)
    return pl.pallas_call(
        functools.partial(_mix_mlp_kernel, ff_blk=ff_blk, nsub=MIX_SUBBLOCKS),
        grid=(n // tm,),
        in_specs=[pl.BlockSpec((tm, D_MODEL), row),
                  pl.BlockSpec((None, tm, D_FOURIER), lambda i: (0, i, 0)),
                  pl.BlockSpec((None, tm, D_FOURIER), lambda i: (1, i, 0)),
                  pl.BlockSpec((tm, D_SSM), row),
                  lspec((D_FOURIER, D_FOURIER)), lspec((D_FOURIER, D_FOURIER)),
                  lspec((1, D_FOURIER)),
                  lspec((D_SSM, D_SSM)), lspec((1, D_SSM)), lspec((1, D_SSM)),
                  lspec((D_MODEL, D_MODEL)), lspec((1, D_MODEL)),
                  lspec((1, D_MODEL)),
                  lspec((D_MODEL, D_FF)), lspec((D_FF, D_MODEL)),
                  lspec((1, D_MODEL))],
        out_specs=pl.BlockSpec((tm, D_MODEL), row),
        out_shape=jax.ShapeDtypeStruct((n, D_MODEL), F32),
        compiler_params=pltpu.CompilerParams(
            dimension_semantics=("parallel",), vmem_limit_bytes=VMEM_LIMIT),
        name="mix_mlp",
    )(x2d, gri, gri, ys, bdr, bdi, fg, wg, bg, sg, wo, pmg, plg, w1, w2, pog)


def _fourier_channel_maps(w_f, seq):
    c = np.arange(FOURIER_HEAD_DIM)
    ang = 2.0 * np.pi * ((c[:, None] * c[None, :]) % FOURIER_HEAD_DIM) / FOURIER_HEAD_DIM
    scale = 1.0 / math.sqrt(seq * FOURIER_HEAD_DIM)
    hp = jax.lax.Precision.HIGHEST
    cw = jnp.einsum("cd,hde->hce", jnp.asarray(np.cos(ang) * scale, F32), w_f.astype(F32), precision=hp)
    sw = jnp.einsum("cd,hde->hce", jnp.asarray(np.sin(ang) * scale, F32), w_f.astype(F32), precision=hp)
    eye = jnp.eye(FOURIER_HEADS, dtype=F32)
    bd = lambda m: jnp.einsum("hce,hk->hcke", m, eye).reshape(D_FOURIER, D_FOURIER)
    return bd(cw).astype(BF16), bd(sw).astype(BF16)


def kernel(x, w_in, w_out, pre_mix_g, post_mix_g, pre_mlp_g, post_mlp_g, fourier_out_g, ssm_out_g,
           w_fourier, lam_re, lam_im, log_dt, b_re, b_im, c_re, c_im, d_skip, w_glu, b_glu,
           w_ff1, w_ff2):
    bsz, seq, _ = x.shape
    depth = w_in.shape[0]
    n = bsz * seq
    n1, n2 = FFT_N1, seq // FFT_N1
    nc = seq // CHUNK
    assert seq % (FFT_N1 * N2_BLK) == 0 and nc % RELAYOUT_CHUNKS == 0
    tm = min(1024, n)
    wk_tab, m2_tab = _fft_tables(seq)

    rows = lambda v: v.reshape(depth, 1, -1).astype(F32)
    w_in_b, w_out_b, w_glu_b = w_in.astype(BF16), w_out.astype(BF16), w_glu.astype(BF16)
    w_ff1_b, w_ff2_b = w_ff1.astype(BF16), w_ff2.astype(BF16)
    pl_tab, pt_tab, ct_tab, bt_tab, dsk, tab = jax.vmap(_ssm_tables)(
        lam_re, lam_im, log_dt, b_re, b_im, c_re, c_im, d_skip)
    m_intra, b_pair, c_pair = _ssm_prep(pl_tab, pt_tab, ct_tab, bt_tab, dsk)
    bdr, bdi = jax.vmap(functools.partial(_fourier_channel_maps, seq=seq))(w_fourier)
    gains = [rows(v) for v in (pre_mix_g, fourier_out_g, b_glu, ssm_out_g, post_mix_g,
                               pre_mlp_g, post_mlp_g)]
    pre_mix, four_g, glu_b, ssm_g, post_mix, pre_mlp, post_mlp = gains

    x2d = x.reshape(n, D_MODEL).astype(F32)
    for i in range(depth):
        y, zs = _in_proj(x2d.reshape(bsz, n1, n2, D_MODEL), pre_mix, w_in_b, wk_tab, i)
        zs = zs.reshape(n, D_SSM)
        g = _fft_stage2(y.reshape(bsz, n1, 2 * n2, D_FOURIER), m2_tab, kb=2 * SUBLANES)
        gri = g.reshape(2, n, D_FOURIER)
        ysr = _ssm_chunked(zs, m_intra, b_pair, c_pair, tab, bsz, seq, i)
        x2d = _mix_mlp(x2d, gri, ysr, bdr, bdi, four_g, w_glu_b, glu_b, ssm_g, w_out_b,
                       post_mix, pre_mlp, w_ff1_b, w_ff2_b, post_mlp, tm, ff_blk=1024, layer=i)
    return x2d.reshape(bsz, seq, D_MODEL).astype(x.dtype)
```

```python
import functools
import math

import jax
import jax.numpy as jnp
import numpy as np
from jax.experimental import pallas as pl
from jax.experimental.pallas import tpu as pltpu

EPS = 1e-6
D_MODEL = 1024
D_FOURIER = 512
D_SSM = 512
FOURIER_HEADS = 8
FOURIER_HEAD_DIM = 64
SSM_GROUP = 16
SSM_GROUPS = 32
SSM_STATE = 64
D_FF = 4096

FFT_N1 = 64
CHUNK = 16
LANES = 128
SUBLANES = 8
MXU_DIM = 256
VMEM_LIMIT = 56 * 1024 * 1024

BF16 = jnp.bfloat16
F32 = jnp.float32


def _rms(x, g):
    return x * jax.lax.rsqrt(jnp.mean(x * x, axis=-1, keepdims=True) + EPS) * g


def _dot(a, b):
    return jnp.dot(a, b, preferred_element_type=F32)


def _const_spec(shape):
    nd = len(shape)
    return pl.BlockSpec(shape, lambda *_: (0,) * nd, pipeline_mode=pl.Buffered(1))


def _layer_spec(shape, layer):
    nd = len(shape)
    return pl.BlockSpec((None,) + tuple(shape), lambda *_: (layer,) + (0,) * nd,
                        pipeline_mode=pl.Buffered(1))


KRON_BLK = SUBLANES
N2_BLK = 2 * KRON_BLK


def _in_proj_kernel(x_ref, g_ref, w_ref, wk_ref, y_ref, zs_ref):
    n1 = x_ref.shape[0]
    x = x_ref[...].reshape(n1 * N2_BLK, D_MODEL)
    h = _rms(x, g_ref[...]).astype(BF16)
    z = _dot(h, w_ref[...])
    zf = z[:, :D_FOURIER].reshape(n1, N2_BLK // KRON_BLK, KRON_BLK, D_FOURIER)
    parts = []
    for a in range(N2_BLK // KRON_BLK):
        za = zf[:, a].reshape(n1 * KRON_BLK, D_FOURIER).astype(BF16)
        ya = _dot(wk_ref[...], za)
        parts.append(ya.reshape(2 * n1, KRON_BLK, D_FOURIER))
    y = jnp.concatenate(parts, axis=1)
    y_ref[...] = y.astype(BF16).reshape(n1, 2, N2_BLK, D_FOURIER)
    zs_ref[...] = _rotate_rows(z[:, D_FOURIER:]).reshape(n1, N2_BLK, D_SSM)


def _in_proj(x4, g, w_bf16, wk, layer):
    bsz, n1, n2, _ = x4.shape
    rows = n1 * KRON_BLK
    return pl.pallas_call(
        _in_proj_kernel,
        grid=(bsz, n2 // N2_BLK),
        in_specs=[pl.BlockSpec((None, n1, N2_BLK, D_MODEL), lambda b, j: (b, 0, j, 0)),
                  _layer_spec((1, D_MODEL), layer),
                  _layer_spec((D_MODEL, D_MODEL), layer),
                  _const_spec((2 * rows, rows))],
        out_specs=[pl.BlockSpec((None, n1, 2, N2_BLK, D_FOURIER), lambda b, j: (b, 0, 0, j, 0)),
                   pl.BlockSpec((None, n1, N2_BLK, D_SSM), lambda b, j: (b, 0, j, 0))],
        out_shape=[jax.ShapeDtypeStruct((bsz, n1, 2, n2, D_FOURIER), BF16),
                   jax.ShapeDtypeStruct((bsz, n1, n2, D_SSM), F32)],
        compiler_params=pltpu.CompilerParams(
            dimension_semantics=("parallel", "parallel"), vmem_limit_bytes=VMEM_LIMIT),
        name="in_proj",
    )(x4, g, w_bf16, wk)


REGROUP_UNROLL = 8


def _fft2_kernel(m_ref, y_ref, g_ref, z_ref, *, kb, n2, pitch):
    nslab = D_FOURIER // LANES
    for j in range(kb):
        z = _dot(m_ref[j], y_ref[j])
        for ri in range(2):
            for s in range(nslab):
                z_ref[ri, s, j * pitch:j * pitch + n2, :] = (
                    z[ri * n2:(ri + 1) * n2, s * LANES:(s + 1) * LANES])

    def regroup(i, _):
        for u in range(REGROUP_UNROLL):
            k2 = i * REGROUP_UNROLL + u
            for ri in range(2):
                for s in range(nslab):
                    rows = [z_ref[ri, s, pl.ds(k2 + a * SUBLANES * pitch, SUBLANES, stride=pitch), :]
                            for a in range(kb // SUBLANES)]
                    g_ref[ri, k2, :, s * LANES:(s + 1) * LANES] = (
                        jnp.concatenate(rows, axis=0).astype(BF16))
        return 0

    jax.lax.fori_loop(0, n2 // REGROUP_UNROLL, regroup, 0)


def _fft_stage2(y, m2, kb):
    b, n1, rows, c = y.shape
    n2 = rows // 2
    pitch = n2 + SUBLANES
    return pl.pallas_call(
        functools.partial(_fft2_kernel, kb=kb, n2=n2, pitch=pitch),
        grid=(n1 // kb, b),
        in_specs=[pl.BlockSpec((kb, rows, rows), lambda j, i: (j, 0, 0)),
                  pl.BlockSpec((None, kb, rows, c), lambda j, i: (i, j, 0, 0))],
        out_specs=pl.BlockSpec((2, None, n2, kb, c), lambda j, i: (0, i, 0, j, 0)),
        out_shape=jax.ShapeDtypeStruct((2, b, n2, n1, c), BF16),
        scratch_shapes=[pltpu.VMEM((2, c // LANES, kb * pitch, LANES), F32)],
        compiler_params=pltpu.CompilerParams(
            dimension_semantics=("parallel", "parallel"), vmem_limit_bytes=VMEM_LIMIT),
        name="fft_stage2",
    )(m2, y)


def _fft_tables(seq):
    n1, n2 = FFT_N1, seq // FFT_N1
    k1 = np.arange(n1)
    ang1 = 2.0 * np.pi * ((k1[:, None] * k1[None, :]) % n1) / n1
    w1 = np.stack([np.cos(ang1), -np.sin(ang1)], axis=1).reshape(2 * n1, n1)
    wk = np.einsum("kn,ab->kanb", w1, np.eye(KRON_BLK)).reshape(2 * n1 * KRON_BLK, n1 * KRON_BLK)
    k2 = np.arange(n2)
    freq = k1[:, None, None] + n1 * k2[None, :, None]
    ang2 = 2.0 * np.pi * ((freq * k2[None, None, :]) % seq) / seq
    mr, mi = np.cos(ang2), -np.sin(ang2)
    m2 = np.concatenate([np.concatenate([mr, -mi], axis=2),
                         np.concatenate([mi, mr], axis=2)], axis=1)
    return jnp.asarray(wk, F32).astype(BF16), jnp.asarray(m2, F32).astype(BF16)


GROUPS_PER_BLOCK = LANES // SSM_GROUP
PAIRS_PER_BLOCK = GROUPS_PER_BLOCK // 2
RELAYOUT_CHUNKS = 8 * SUBLANES
SCAN_STEPS = (1, 2, 4)
SCAN_SLOTS = 2 * (len(SCAN_STEPS) + 1)


def _rotate_rows(v):
    return jnp.concatenate(
        [pltpu.roll(v[:, c:c + LANES], 0, axis=1, stride=SSM_GROUP, stride_axis=0)
         for c in range(0, v.shape[1], LANES)], axis=1)


def _unrotate_rows(v):
    row = jax.lax.broadcasted_iota(jnp.int32, v.shape, 0)
    cols = []
    for c in range(0, v.shape[1], LANES):
        w = v[:, c:c + LANES]
        for bit in (1, 2, 4):
            w = jnp.where((row[:, :LANES] & bit) != 0,
                          pltpu.roll(w, LANES - SSM_GROUP * bit, axis=1), w)
        cols.append(w)
    return jnp.concatenate(cols, axis=1)


def _ssm_kernel(z_ref, mi_ref, bs_ref, cs_ref, tab_ref, y_ref,
                u2_ref, s_ref, car_ref, yg_ref, *, nc):
    gpb, npair = GROUPS_PER_BLOCK, PAIRS_PER_BLOCK
    t16 = RELAYOUT_CHUNKS
    rows_it = t16 * CHUNK
    lane_blk = jax.lax.broadcasted_iota(jnp.int32, (t16, LANES), 1) // SSM_GROUP
    masks = [lane_blk == m for m in range(gpb)]

    def relayout_in(i, _):
        r0 = pl.multiple_of(i * rows_it, rows_it)
        ut = [z_ref[pl.ds(r0 + t, t16, stride=CHUNK), :] for t in range(CHUNK)]
        c0 = pl.multiple_of(i * t16, t16)
        for g in range(gpb):
            for j in range(CHUNK // gpb):
                acc = ut[gpb * j]
                for tt in range(1, gpb):
                    acc = jnp.where(masks[(g + tt) % gpb], ut[gpb * j + tt], acc)
                if g:
                    acc = pltpu.roll(acc, LANES - SSM_GROUP * g, axis=1)
                u2_ref[g, pl.ds(c0, t16), j * LANES:(j + 1) * LANES] = acc.astype(BF16)
        return 0

    jax.lax.fori_loop(0, nc // t16, relayout_in, 0)

    for q in range(npair):
        up = jnp.concatenate([u2_ref[2 * q], u2_ref[2 * q + 1]], axis=1)
        s_ref[q] = _dot(up, bs_ref[q])

    row = jax.lax.broadcasted_iota(jnp.int32, (SUBLANES, LANES), 0)
    nblk = nc // SUBLANES

    def scan_block(tab, sr, si, cr, ci, down):
        keep = (lambda sh: row >= sh) if down else (lambda sh: row <= SUBLANES - 1 - sh)
        amt = (lambda sh: sh) if down else (lambda sh: SUBLANES - sh)
        for k, sh in enumerate(SCAN_STEPS):
            tr = pltpu.roll(sr, amt(sh), axis=0)
            ti = pltpu.roll(si, amt(sh), axis=0)
            ar, ai = tab[2 * k], tab[2 * k + 1]
            sr, si = sr + (ar * tr - ai * ti), si + (ar * ti + ai * tr)
        er, ei = tab[SCAN_SLOTS - 2], tab[SCAN_SLOTS - 1]
        fr = sr + (er * cr - ei * ci)
        fi = si + (er * ci + ei * cr)
        outr = jnp.where(keep(1), pltpu.roll(fr, amt(1), axis=0), cr)
        outi = jnp.where(keep(1), pltpu.roll(fi, amt(1), axis=0), ci)
        edge = SUBLANES - 1 if down else 0
        ncr = jnp.broadcast_to(fr[edge:edge + 1, :], (SUBLANES, LANES))
        nci = jnp.broadcast_to(fi[edge:edge + 1, :], (SUBLANES, LANES))
        return outr, outi, ncr, nci

    def scan_body(i, carry):
        new = []
        for q in range(npair):
            for d in range(2):
                blk = i if d == 0 else nblk - 1 - i
                r0 = pl.multiple_of(blk * SUBLANES, SUBLANES)
                lo = 2 * d * LANES
                tab = [tab_ref[q, d, k] for k in range(SCAN_SLOTS)]
                cr, ci = carry[4 * q + 2 * d], carry[4 * q + 2 * d + 1]
                outr, outi, cr, ci = scan_block(
                    tab, s_ref[q, pl.ds(r0, SUBLANES), lo:lo + LANES],
                    s_ref[q, pl.ds(r0, SUBLANES), lo + LANES:lo + 2 * LANES], cr, ci, d == 0)
                car_ref[q, pl.ds(r0, SUBLANES), lo:lo + LANES] = outr
                car_ref[q, pl.ds(r0, SUBLANES), lo + LANES:lo + 2 * LANES] = outi
                new += [cr, ci]
        return tuple(new)

    zero = jnp.zeros((SUBLANES, LANES), F32)
    jax.lax.fori_loop(0, nblk, scan_body, (zero,) * (4 * npair))

    kk = CHUNK * SSM_GROUP
    for q in range(npair):
        yi = _dot(car_ref[q].astype(BF16), cs_ref[q])
        for e in range(2):
            g = 2 * q + e
            yg_ref[g] = _dot(u2_ref[g], mi_ref[g]) + yi[:, e * kk:(e + 1) * kk]

    def relayout_out(i, _):
        c0 = pl.multiple_of(i * t16, t16)
        r0 = pl.multiple_of(i * rows_it, rows_it)
        yrot = []
        for g in range(gpb):
            halves = []
            for j in range(CHUNK // gpb):
                v = yg_ref[g, pl.ds(c0, t16), j * LANES:(j + 1) * LANES]
                halves.append(pltpu.roll(v, SSM_GROUP * g, axis=1) if g else v)
            yrot.append(halves)
        for t in range(CHUNK):
            j, tt = divmod(t, gpb)
            acc = yrot[0][j]
            for g in range(1, gpb):
                acc = jnp.where(masks[(g + tt) % gpb], yrot[g][j], acc)
            y_ref[pl.ds(r0 + t, t16, stride=CHUNK), :] = acc
        return 0

    jax.lax.fori_loop(0, nc // t16, relayout_out, 0)


def _ssm_chunked(zs, m_intra, b_pair, c_pair, tab, bsz, seq, layer):
    nc = seq // CHUNK
    gpb, npair = GROUPS_PER_BLOCK, PAIRS_PER_BLOCK
    k = CHUNK * SSM_GROUP
    return pl.pallas_call(
        functools.partial(_ssm_kernel, nc=nc),
        grid=(D_SSM // LANES, bsz),
        in_specs=[pl.BlockSpec((seq, LANES), lambda j, b: (b, j)),
                  pl.BlockSpec((None, gpb, k, k), lambda j, b: (layer, j, 0, 0)),
                  pl.BlockSpec((None, npair, 2 * k, 2 * k), lambda j, b: (layer, j, 0, 0)),
                  pl.BlockSpec((None, npair, 2 * k, 2 * k), lambda j, b: (layer, j, 0, 0)),
                  pl.BlockSpec((None, npair, 2, SCAN_SLOTS, SUBLANES, LANES),
                               lambda j, b: (layer, j, 0, 0, 0, 0))],
        out_specs=pl.BlockSpec((seq, LANES), lambda j, b: (b, j)),
        out_shape=jax.ShapeDtypeStruct((bsz * seq, D_SSM), F32),
        scratch_shapes=[pltpu.VMEM((gpb, nc, k), BF16),
                        pltpu.VMEM((npair, nc, 2 * k), F32),
                        pltpu.VMEM((npair, nc, 2 * k), F32),
                        pltpu.VMEM((gpb, nc, k), F32)],
        compiler_params=pltpu.CompilerParams(
            dimension_semantics=("parallel", "parallel"), vmem_limit_bytes=VMEM_LIMIT),
        name="ssm_chunk",
    )(zs, m_intra, b_pair, c_pair, tab)


def _ssm_tables(lam_re, lam_im, log_dt, b_re, b_im, c_re, c_im, d_skip):
    t, h, p, g = CHUNK, SSM_GROUP, SSM_STATE, SSM_GROUPS
    nq = g // 2
    dt = jnp.exp(log_dt.astype(F32))[..., None]
    are = lam_re.astype(F32) * dt
    aim = lam_im.astype(F32) * dt

    def power(expo, a_re, a_im):
        mag = jnp.exp(a_re * expo)
        return mag * jnp.cos(a_im * expo), mag * jnp.sin(a_im * expo)

    lbr, lbi = power(1.0, are, aim)
    den = lam_re * lam_re + lam_im * lam_im
    cr = ((lbr - 1.0) * lam_re + lbi * lam_im) / den
    ci = (lbi * lam_re - (lbr - 1.0) * lam_im) / den
    bbr = cr[..., None] * b_re - ci[..., None] * b_im
    bbi = cr[..., None] * b_im + ci[..., None] * b_re

    def pair_rows(x):
        return x.reshape(2, 2, nq, 2 * p, x.shape[-1]).transpose(2, 0, 1, 3, 4)

    lane_pad = lambda x: jnp.pad(x, [(0, 0)] * (x.ndim - 1) + [(0, LANES - x.shape[-1])])
    steps = jnp.arange(t + 1, dtype=F32)
    pw = jnp.stack(power(steps, are[..., None], aim[..., None]))
    pl_tab = lane_pad(pair_rows(pw))
    pt_tab = jnp.pad(jnp.swapaxes(pair_rows(pw), -1, -2),
                     [(0, 0)] * 3 + [(0, LANES - t - 1), (0, 0)])
    ct_tab = lane_pad(pair_rows(jnp.stack([jnp.swapaxes(c_re.astype(F32), -1, -2),
                                           jnp.swapaxes(c_im.astype(F32), -1, -2)])))
    bt = jnp.stack([jnp.swapaxes(bbr, -1, -2), jnp.swapaxes(bbi, -1, -2)])
    bt = bt.reshape(2, 2, nq, 2, h, p)
    zero = jnp.zeros_like(bt[:, :, :, 0])
    bt_tab = jnp.stack([jnp.concatenate([bt[:, :, :, 0], zero], -1),
                        jnp.concatenate([zero, bt[:, :, :, 1]], -1)], axis=3)
    bt_tab = bt_tab.transpose(2, 0, 1, 3, 4, 5)
    dsk = jnp.pad(d_skip.astype(F32), ((0, 0), (0, t * h - h))).reshape(nq, 2, 1, t * h)

    r8 = np.arange(SUBLANES)
    expo = np.stack([np.stack([np.full(SUBLANES, k), np.full(SUBLANES, k)]) for k in SCAN_STEPS]
                    + [np.stack([r8 + 1, SUBLANES - r8])])
    expo = jnp.asarray(t * expo, F32)[:, :, None, :, None]
    ar, ai = power(expo, are.reshape(2, nq, 1, 2 * p)[None], aim.reshape(2, nq, 1, 2 * p)[None])
    live = np.stack([np.stack([r8 >= k, r8 <= SUBLANES - 1 - k]) for k in SCAN_STEPS]
                    + [np.ones((2, SUBLANES), bool)])
    live = jnp.asarray(live, F32)[:, :, None, :, None]
    ar, ai = ar * live, ai * live
    tab = jnp.stack([ar, ai], axis=1).reshape(SCAN_SLOTS, 2, nq, SUBLANES, 2 * p)
    tab = tab.transpose(2, 1, 0, 3, 4)
    return pl_tab, pt_tab, ct_tab, bt_tab, dsk, tab


def _select_lanes(x, sel):
    hi = x.astype(BF16)
    lo = (x - hi.astype(F32)).astype(BF16)
    return _dot(hi, sel) + _dot(lo, sel)


def _select_rows(sel, x):
    hi = x.astype(BF16)
    lo = (x - hi.astype(F32)).astype(BF16)
    return _dot(sel, hi) + _dot(sel, lo)


def _roll_two_vregs(x, r):
    x0, x1 = x[:, :LANES], x[:, LANES:]
    if r >= LANES:
        x0, x1, r = x1, x0, r - LANES
    if r == 0:
        return jnp.concatenate([x0, x1], axis=1)
    lane = jax.lax.broadcasted_iota(jnp.int32, x0.shape, 1)
    a0, a1 = pltpu.roll(x0, r, axis=1), pltpu.roll(x1, r, axis=1)
    return jnp.concatenate([jnp.where(lane < r, a1, a0), jnp.where(lane < r, a0, a1)], axis=1)


def _ssm_prep_kernel(pl_ref, pt_ref, ct_ref, bt_ref, dsk_ref, sl_ref, sr_ref, e_ref,
                     m_ref, b_ref, c_ref):
    t, h = CHUNK, SSM_GROUP
    kk = t * h
    lane = jax.lax.broadcasted_iota(jnp.int32, (h, kk), 1)
    rowi = jax.lax.broadcasted_iota(jnp.int32, (h, kk), 0)
    half = jax.lax.broadcasted_iota(jnp.int32, (LANES, kk), 0) // SSM_STATE

    cl, cq = [], []
    for d in range(2):
        ctr, cti = _select_lanes(ct_ref[0, d], e_ref[...]), _select_lanes(ct_ref[1, d], e_ref[...])
        for out, pat in ((cl, d), (cq, 2 + d)):
            pr, pi = _select_lanes(pl_ref[0, d], sl_ref[pat]), _select_lanes(pl_ref[1, d], sl_ref[pat])
            out.append((ctr * pr - cti * pi, ctr * pi + cti * pr))

    for d in range(2):
        for ri, val in enumerate((cq[d][0], -cq[d][1])):
            r0 = (2 * d + ri) * LANES
            for gi in range(2):
                c_ref[r0:r0 + LANES, gi * kk:(gi + 1) * kk] = (
                    jnp.where(half == gi, val, 0.0).astype(BF16))

    for d in range(2):
        pwr, pwi = _select_rows(sr_ref[d], pt_ref[0, d]), _select_rows(sr_ref[d], pt_ref[1, d])
        for gi in range(2):
            btr = jnp.concatenate([bt_ref[0, d, gi]] * t, axis=0)
            bti = jnp.concatenate([bt_ref[1, d, gi]] * t, axis=0)
            r0, c0 = gi * kk, 2 * d * LANES
            b_ref[r0:r0 + kk, c0:c0 + LANES] = (btr * pwr - bti * pwi).astype(BF16)
            b_ref[r0:r0 + kk, c0 + LANES:c0 + 2 * LANES] = (btr * pwi + bti * pwr).astype(BF16)

    for gi in range(2):
        gen = [_dot(bt_ref[0, d, gi].astype(BF16), cl[d][0].astype(BF16))
               - _dot(bt_ref[1, d, gi].astype(BF16), cl[d][1].astype(BF16)) for d in range(2)]
        diag = jnp.where(lane < h, gen[1], 0.0) + jnp.where(rowi == lane, dsk_ref[gi], 0.0)
        kf, kb = gen[0] + diag, gen[1]
        for s in range(t):
            blk = jnp.where(lane >= h * s, _roll_two_vregs(kf, h * s), _roll_two_vregs(kb, h * s))
            m_ref[gi, s * h:(s + 1) * h, :] = blk.astype(BF16)


def _ssm_prep_constants():
    t, h = CHUNK, SSM_GROUP
    kk = t * h
    step = np.arange(kk) // h
    k = np.arange(LANES)
    pats = [step, (t - step) % t, step + 1, t - step]
    sel_lane = np.stack([(k[:, None] == pat[None, :]) for pat in pats]).astype(np.float32)
    sel_row = np.stack([(pat[:, None] == k[None, :]) for pat in (t - 1 - step, step)])
    tile_ho = (k[:, None] == (np.arange(kk) % h)[None, :])
    as_bf16 = lambda a: jnp.asarray(a, F32).astype(BF16)
    return as_bf16(sel_lane), as_bf16(sel_row), as_bf16(tile_ho)


def _ssm_prep(pl_tab, pt_tab, ct_tab, bt_tab, dsk):
    depth, nq = pl_tab.shape[:2]
    kk = CHUNK * SSM_GROUP
    sel_lane, sel_row, tile_ho = _ssm_prep_constants()
    tab5 = pl.BlockSpec((None, None, 2, 2, LANES, LANES), lambda l, q: (l, q, 0, 0, 0, 0))
    return pl.pallas_call(
        _ssm_prep_kernel,
        grid=(depth, nq),
        in_specs=[tab5, tab5, tab5,
                  pl.BlockSpec((None, None, 2, 2, 2, SSM_GROUP, LANES),
                               lambda l, q: (l, q, 0, 0, 0, 0, 0)),
                  pl.BlockSpec((None, None, 2, 1, kk), lambda l, q: (l, q, 0, 0, 0)),
                  _const_spec(sel_lane.shape), _const_spec(sel_row.shape),
                  _const_spec(tile_ho.shape)],
        out_specs=[pl.BlockSpec((None, 2, kk, kk), lambda l, q: (l, q, 0, 0)),
                   pl.BlockSpec((None, None, 2 * kk, 2 * kk), lambda l, q: (l, q, 0, 0)),
                   pl.BlockSpec((None, None, 2 * kk, 2 * kk), lambda l, q: (l, q, 0, 0))],
        out_shape=[jax.ShapeDtypeStruct((depth, 2 * nq, kk, kk), BF16),
                   jax.ShapeDtypeStruct((depth, nq, 2 * kk, 2 * kk), BF16),
                   jax.ShapeDtypeStruct((depth, nq, 2 * kk, 2 * kk), BF16)],
        compiler_params=pltpu.CompilerParams(
            dimension_semantics=("parallel", "parallel"), vmem_limit_bytes=VMEM_LIMIT),
        name="ssm_prep",
    )(pl_tab, pt_tab, ct_tab, bt_tab, dsk, sel_lane, sel_row, tile_ho)


MIX_ROWS = 1024
MIX_SUBBLOCKS = 2
FF_BLOCK = 1024
FFT2_K1_BLOCK = 2 * SUBLANES


def _mix_mlp_kernel(x_ref, gr_ref, gi_ref, ys_ref, bdr_ref, bdi_ref, fg_ref, wg_ref, bg_ref,
                    sg_ref, wo_ref, pmg_ref, plg_ref, w1_ref, w2_ref, pog_ref, o_ref, *, ff_blk,
                    nsub):
    rows = x_ref.shape[0] // nsub

    def mix(k):
        r = slice(k * rows, (k + 1) * rows)
        gr, gi = gr_ref[r, :], gi_ref[r, :]
        yf = jnp.concatenate(
            [_dot(gr[:, c:c + MXU_DIM], bdr_ref[c:c + MXU_DIM, c:c + MXU_DIM])
             + _dot(gi[:, c:c + MXU_DIM], bdi_ref[c:c + MXU_DIM, c:c + MXU_DIM])
             for c in range(0, D_FOURIER, MXU_DIM)], axis=1)
        yf = _rms(yf, fg_ref[...])
        y = _unrotate_rows(ys_ref[r, :])
        gl = jax.nn.gelu(y, approximate=True)
        gate = _dot(gl.astype(BF16), wg_ref[...]) + bg_ref[...]
        ys = _rms(gl * jax.nn.sigmoid(gate), sg_ref[...])
        cat = jnp.concatenate([yf.astype(BF16), ys.astype(BF16)], axis=-1)
        x1 = x_ref[r, :] + _rms(_dot(cat, wo_ref[...]), pmg_ref[...])
        return x1, _rms(x1, plg_ref[...]).astype(BF16)

    def mlp(k, x1, h):
        acc = jnp.zeros(x1.shape, F32)
        for j in range(D_FF // ff_blk):
            a = _dot(h, w1_ref[:, j * ff_blk:(j + 1) * ff_blk])
            a = jnp.square(jnp.maximum(a, 0.0)).astype(BF16)
            acc = acc + _dot(a, w2_ref[j * ff_blk:(j + 1) * ff_blk, :])
        o_ref[k * rows:(k + 1) * rows, :] = x1 + _rms(acc, pog_ref[...])

    nxt = mix(0)
    for k in range(nsub):
        cur = nxt
        if k + 1 < nsub:
            nxt = mix(k + 1)
        mlp(k, *cur)


def _mix_mlp(x2d, gri, ys, bdr, bdi, fg, wg, bg, sg, wo, pmg, plg, w1, w2, pog, tm, ff_blk, layer):
    n = x2d.shape[0]
    row = lambda i: (i, 0)
    lspec = functools.partial(_layer_spec, layer=layer)
    return pl.pallas_call(
        functools.partial(_mix_mlp_kernel, ff_blk=ff_blk, nsub=MIX_SUBBLOCKS),
        grid=(n // tm,),
        in_specs=[pl.BlockSpec((tm, D_MODEL), row),
                  pl.BlockSpec((None, tm, D_FOURIER), lambda i: (0, i, 0)),
                  pl.BlockSpec((None, tm, D_FOURIER), lambda i: (1, i, 0)),
                  pl.BlockSpec((tm, D_SSM), row),
                  lspec((D_FOURIER, D_FOURIER)), lspec((D_FOURIER, D_FOURIER)),
                  lspec((1, D_FOURIER)),
                  lspec((D_SSM, D_SSM)), lspec((1, D_SSM)), lspec((1, D_SSM)),
                  lspec((D_MODEL, D_MODEL)), lspec((1, D_MODEL)),
                  lspec((1, D_MODEL)),
                  lspec((D_MODEL, D_FF)), lspec((D_FF, D_MODEL)),
                  lspec((1, D_MODEL))],
        out_specs=pl.BlockSpec((tm, D_MODEL), row),
        out_shape=jax.ShapeDtypeStruct((n, D_MODEL), F32),
        compiler_params=pltpu.CompilerParams(
            dimension_semantics=("parallel",), vmem_limit_bytes=VMEM_LIMIT),
        name="mix_mlp",
    )(x2d, gri, gri, ys, bdr, bdi, fg, wg, bg, sg, wo, pmg, plg, w1, w2, pog)


def _fourier_channel_maps(w_f, seq):
    c = np.arange(FOURIER_HEAD_DIM)
    ang = 2.0 * np.pi * ((c[:, None] * c[None, :]) % FOURIER_HEAD_DIM) / FOURIER_HEAD_DIM
    scale = 1.0 / math.sqrt(seq * FOURIER_HEAD_DIM)
    hp = jax.lax.Precision.HIGHEST
    cw = jnp.einsum("cd,hde->hce", jnp.asarray(np.cos(ang) * scale, F32), w_f.astype(F32), precision=hp)
    sw = jnp.einsum("cd,hde->hce", jnp.asarray(np.sin(ang) * scale, F32), w_f.astype(F32), precision=hp)
    eye = jnp.eye(FOURIER_HEADS, dtype=F32)
    bd = lambda m: jnp.einsum("hce,hk->hcke", m, eye).reshape(D_FOURIER, D_FOURIER)
    return bd(cw).astype(BF16), bd(sw).astype(BF16)


def kernel(x, w_in, w_out, pre_mix_g, post_mix_g, pre_mlp_g, post_mlp_g, fourier_out_g, ssm_out_g,
           w_fourier, lam_re, lam_im, log_dt, b_re, b_im, c_re, c_im, d_skip, w_glu, b_glu,
           w_ff1, w_ff2):
    bsz, seq, _ = x.shape
    depth = w_in.shape[0]
    n = bsz * seq
    n1, n2 = FFT_N1, seq // FFT_N1
    nc = seq // CHUNK
    assert seq % (FFT_N1 * N2_BLK) == 0 and nc % RELAYOUT_CHUNKS == 0
    tm = min(MIX_ROWS, n)
    wk_tab, m2_tab = _fft_tables(seq)

    rows = lambda v: v.reshape(depth, 1, -1).astype(F32)
    w_in_b, w_out_b, w_glu_b = w_in.astype(BF16), w_out.astype(BF16), w_glu.astype(BF16)
    w_ff1_b, w_ff2_b = w_ff1.astype(BF16), w_ff2.astype(BF16)
    pl_tab, pt_tab, ct_tab, bt_tab, dsk, tab = jax.vmap(_ssm_tables)(
        lam_re, lam_im, log_dt, b_re, b_im, c_re, c_im, d_skip)
    m_intra, b_pair, c_pair = _ssm_prep(pl_tab, pt_tab, ct_tab, bt_tab, dsk)
    bdr, bdi = jax.vmap(functools.partial(_fourier_channel_maps, seq=seq))(w_fourier)
    gains = [rows(v) for v in (pre_mix_g, fourier_out_g, b_glu, ssm_out_g, post_mix_g,
                               pre_mlp_g, post_mlp_g)]
    pre_mix, four_g, glu_b, ssm_g, post_mix, pre_mlp, post_mlp = gains

    x2d = x.reshape(n, D_MODEL).astype(F32)
    for i in range(depth):
        y, zs = _in_proj(x2d.reshape(bsz, n1, n2, D_MODEL), pre_mix, w_in_b, wk_tab, i)
        zs = zs.reshape(n, D_SSM)
        g = _fft_stage2(y.reshape(bsz, n1, 2 * n2, D_FOURIER), m2_tab, kb=FFT2_K1_BLOCK)
        gri = g.reshape(2, n, D_FOURIER)
        ysr = _ssm_chunked(zs, m_intra, b_pair, c_pair, tab, bsz, seq, i)
        x2d = _mix_mlp(x2d, gri, ysr, bdr, bdi, four_g, w_glu_b, glu_b, ssm_g, w_out_b,
                       post_mix, pre_mlp, w_ff1_b, w_ff2_b, post_mlp, tm, ff_blk=FF_BLOCK, layer=i)
    return x2d.reshape(bsz, seq, D_MODEL).astype(x.dtype)
```

```python
import functools
import math

import jax
import jax.numpy as jnp
import numpy as np
from jax.experimental import pallas as pl
from jax.experimental.pallas import tpu as pltpu

EPS = 1e-6
D_MODEL = 1024
D_FOURIER = 512
D_SSM = 512
FOURIER_HEADS = 8
FOURIER_HEAD_DIM = 64
SSM_GROUP = 16
SSM_GROUPS = 32
SSM_STATE = 64
D_FF = 4096

FFT_N1 = 64
CHUNK = 16
LANES = 128
SUBLANES = 8
MXU_DIM = 256
VMEM_LIMIT = 56 * 1024 * 1024

BF16 = jnp.bfloat16
F32 = jnp.float32


def _rms(x, g):
    return x * jax.lax.rsqrt(jnp.mean(x * x, axis=-1, keepdims=True) + EPS) * g


def _dot(a, b):
    return jnp.dot(a, b, preferred_element_type=F32)


def _const_spec(shape):
    nd = len(shape)
    return pl.BlockSpec(shape, lambda *_: (0,) * nd, pipeline_mode=pl.Buffered(1))


def _layer_spec(shape, layer):
    nd = len(shape)
    return pl.BlockSpec((None,) + tuple(shape), lambda *_: (layer,) + (0,) * nd,
                        pipeline_mode=pl.Buffered(1))


KRON_BLK = SUBLANES
N2_BLK = 2 * KRON_BLK


def _in_proj_kernel(x_ref, g_ref, w_ref, wk_ref, y_ref, zs_ref, zt_ref):
    n1 = x_ref.shape[0]
    x = x_ref[...].reshape(n1 * N2_BLK, D_MODEL)
    h = _rms(x, g_ref[...]).astype(BF16)
    z = _dot(h, w_ref[...])
    zf = z[:, :D_FOURIER].reshape(n1, N2_BLK // KRON_BLK, KRON_BLK, D_FOURIER)
    parts = []
    for a in range(N2_BLK // KRON_BLK):
        za = zf[:, a].reshape(n1 * KRON_BLK, D_FOURIER).astype(BF16)
        ya = _dot(wk_ref[...], za)
        parts.append(ya.reshape(2 * n1, KRON_BLK, D_FOURIER))
    y = jnp.concatenate(parts, axis=1)
    y_ref[...] = y.astype(BF16).reshape(n1, 2, N2_BLK, D_FOURIER)
    zrot = _rotate_rows(z[:, D_FOURIER:])
    for s in range(D_SSM // LANES):
        zt_ref[s] = zrot[:, s * LANES:(s + 1) * LANES]
    for t in range(CHUNK):
        for s in range(D_SSM // LANES):
            zs_ref[t * n1:(t + 1) * n1, s * LANES:(s + 1) * LANES] = (
                zt_ref[s, pl.ds(t, n1, stride=CHUNK), :])


def _in_proj(x4, g, w_bf16, wk, layer):
    bsz, n1, n2, _ = x4.shape
    rows = n1 * KRON_BLK
    return pl.pallas_call(
        _in_proj_kernel,
        grid=(bsz, n2 // N2_BLK),
        in_specs=[pl.BlockSpec((None, n1, N2_BLK, D_MODEL), lambda b, j: (b, 0, j, 0)),
                  _layer_spec((1, D_MODEL), layer),
                  _layer_spec((D_MODEL, D_MODEL), layer),
                  _const_spec((2 * rows, rows))],
        out_specs=[pl.BlockSpec((None, n1, 2, N2_BLK, D_FOURIER), lambda b, j: (b, 0, 0, j, 0)),
                   pl.BlockSpec((None, None, N2_BLK * n1, D_SSM), lambda b, j: (b, j, 0, 0))],
        out_shape=[jax.ShapeDtypeStruct((bsz, n1, 2, n2, D_FOURIER), BF16),
                   jax.ShapeDtypeStruct((bsz, n2 // N2_BLK, N2_BLK * n1, D_SSM), F32)],
        scratch_shapes=[pltpu.VMEM((D_SSM // LANES, N2_BLK * n1, LANES), F32)],
        compiler_params=pltpu.CompilerParams(
            dimension_semantics=("parallel", "parallel"), vmem_limit_bytes=VMEM_LIMIT),
        name="in_proj",
    )(x4, g, w_bf16, wk)


REGROUP_UNROLL = 8


def _fft2_kernel(m_ref, y_ref, g_ref, z_ref, *, kb, n2, pitch):
    nslab = D_FOURIER // LANES
    for j in range(kb):
        z = _dot(m_ref[j], y_ref[j])
        for ri in range(2):
            for s in range(nslab):
                z_ref[ri, s, j * pitch:j * pitch + n2, :] = (
                    z[ri * n2:(ri + 1) * n2, s * LANES:(s + 1) * LANES])

    def regroup(i, _):
        for u in range(REGROUP_UNROLL):
            k2 = i * REGROUP_UNROLL + u
            for ri in range(2):
                for s in range(nslab):
                    rows = [z_ref[ri, s, pl.ds(k2 + a * SUBLANES * pitch, SUBLANES, stride=pitch), :]
                            for a in range(kb // SUBLANES)]
                    g_ref[ri, k2, :, s * LANES:(s + 1) * LANES] = (
                        jnp.concatenate(rows, axis=0).astype(BF16))
        return 0

    jax.lax.fori_loop(0, n2 // REGROUP_UNROLL, regroup, 0)


def _fft_stage2(y, m2, kb):
    b, n1, rows, c = y.shape
    n2 = rows // 2
    pitch = n2 + SUBLANES
    return pl.pallas_call(
        functools.partial(_fft2_kernel, kb=kb, n2=n2, pitch=pitch),
        grid=(n1 // kb, b),
        in_specs=[pl.BlockSpec((kb, rows, rows), lambda j, i: (j, 0, 0)),
                  pl.BlockSpec((None, kb, rows, c), lambda j, i: (i, j, 0, 0))],
        out_specs=pl.BlockSpec((2, None, n2, kb, c), lambda j, i: (0, i, 0, j, 0)),
        out_shape=jax.ShapeDtypeStruct((2, b, n2, n1, c), BF16),
        scratch_shapes=[pltpu.VMEM((2, c // LANES, kb * pitch, LANES), F32)],
        compiler_params=pltpu.CompilerParams(
            dimension_semantics=("parallel", "parallel"), vmem_limit_bytes=VMEM_LIMIT),
        name="fft_stage2",
    )(m2, y)


def _fft_tables(seq):
    n1, n2 = FFT_N1, seq // FFT_N1
    k1 = np.arange(n1)
    ang1 = 2.0 * np.pi * ((k1[:, None] * k1[None, :]) % n1) / n1
    w1 = np.stack([np.cos(ang1), -np.sin(ang1)], axis=1).reshape(2 * n1, n1)
    wk = np.einsum("kn,ab->kanb", w1, np.eye(KRON_BLK)).reshape(2 * n1 * KRON_BLK, n1 * KRON_BLK)
    k2 = np.arange(n2)
    freq = k1[:, None, None] + n1 * k2[None, :, None]
    ang2 = 2.0 * np.pi * ((freq * k2[None, None, :]) % seq) / seq
    mr, mi = np.cos(ang2), -np.sin(ang2)
    m2 = np.concatenate([np.concatenate([mr, -mi], axis=2),
                         np.concatenate([mi, mr], axis=2)], axis=1)
    return jnp.asarray(wk, F32).astype(BF16), jnp.asarray(m2, F32).astype(BF16)


GROUPS_PER_BLOCK = LANES // SSM_GROUP
PAIRS_PER_BLOCK = GROUPS_PER_BLOCK // 2
RELAYOUT_CHUNKS = 8 * SUBLANES
SCAN_STEPS = (1, 2, 4)
SCAN_SLOTS = 2 * (len(SCAN_STEPS) + 1)


def _rotate_rows(v):
    return jnp.concatenate(
        [pltpu.roll(v[:, c:c + LANES], 0, axis=1, stride=SSM_GROUP, stride_axis=0)
         for c in range(0, v.shape[1], LANES)], axis=1)


def _unrotate_rows(v):
    row = jax.lax.broadcasted_iota(jnp.int32, v.shape, 0)
    cols = []
    for c in range(0, v.shape[1], LANES):
        w = v[:, c:c + LANES]
        for bit in (1, 2, 4):
            w = jnp.where((row[:, :LANES] & bit) != 0,
                          pltpu.roll(w, LANES - SSM_GROUP * bit, axis=1), w)
        cols.append(w)
    return jnp.concatenate(cols, axis=1)


def _ssm_kernel(z_ref, mi_ref, bs_ref, cs_ref, pw_ref, tab_ref, y_ref,
                u2_ref, s_ref, car_ref, yg_ref, *, nj, nrow):
    gpb, npair = GROUPS_PER_BLOCK, PAIRS_PER_BLOCK
    t16 = RELAYOUT_CHUNKS
    parts = nrow // t16
    lane_blk = jax.lax.broadcasted_iota(jnp.int32, (t16, LANES), 1) // SSM_GROUP
    masks = [lane_blk == m for m in range(gpb)]

    def relayout_in(i, _):
        j, r0 = i // parts, pl.multiple_of((i % parts) * t16, t16)
        ut = [z_ref[j, pl.ds(t * nrow + r0, t16), :] for t in range(CHUNK)]
        c0 = pl.multiple_of(j * nrow + r0, t16)
        for g in range(gpb):
            for hf in range(CHUNK // gpb):
                acc = ut[gpb * hf]
                for tt in range(1, gpb):
                    acc = jnp.where(masks[(g + tt) % gpb], ut[gpb * hf + tt], acc)
                if g:
                    acc = pltpu.roll(acc, LANES - SSM_GROUP * g, axis=1)
                u2_ref[g, pl.ds(c0, t16), hf * LANES:(hf + 1) * LANES] = acc.astype(BF16)
        return 0

    jax.lax.fori_loop(0, nj * parts, relayout_in, 0)

    for q in range(npair):
        up = jnp.concatenate([u2_ref[2 * q], u2_ref[2 * q + 1]], axis=1)
        s_ref[q] = _dot(up, bs_ref[q])

    row = jax.lax.broadcasted_iota(jnp.int32, (SUBLANES, LANES), 0)
    nvreg = nrow // SUBLANES

    def scan_block(tab, sr, si, cr, ci, down):
        keep = (lambda sh: row >= sh) if down else (lambda sh: row <= SUBLANES - 1 - sh)
        amt = (lambda sh: sh) if down else (lambda sh: SUBLANES - sh)
        for k, sh in enumerate(SCAN_STEPS):
            tr = pltpu.roll(sr, amt(sh), axis=0)
            ti = pltpu.roll(si, amt(sh), axis=0)
            ar, ai = tab[2 * k], tab[2 * k + 1]
            sr, si = sr + (ar * tr - ai * ti), si + (ar * ti + ai * tr)
        er, ei = tab[SCAN_SLOTS - 2], tab[SCAN_SLOTS - 1]
        fr = sr + (er * cr - ei * ci)
        fi = si + (er * ci + ei * cr)
        outr = jnp.where(keep(1), pltpu.roll(fr, amt(1), axis=0), cr)
        outi = jnp.where(keep(1), pltpu.roll(fi, amt(1), axis=0), ci)
        edge = SUBLANES - 1 if down else 0
        ncr = jnp.broadcast_to(fr[edge:edge + 1, :], (SUBLANES, LANES))
        nci = jnp.broadcast_to(fi[edge:edge + 1, :], (SUBLANES, LANES))
        return outr, outi, ncr, nci

    def tile_rows(x):
        return jnp.concatenate([x] * nvreg, axis=0)

    zero = jnp.zeros((SUBLANES, LANES), F32)
    for q in range(npair):
        for d in range(2):
            down = d == 0
            lo_r, lo_i = 2 * d * LANES, (2 * d + 1) * LANES
            slab = lambda j: slice(j * nrow, (j + 1) * nrow)
            order = list(range(nj)) if down else list(range(nj - 1, -1, -1))
            a_r, a_i = tile_rows(pw_ref[q, d, 0]), tile_rows(pw_ref[q, d, 1])
            lr = li = None
            for idx, j in enumerate(order):
                sr = s_ref[q, slab(j), lo_r:lo_r + LANES]
                si = s_ref[q, slab(j), lo_i:lo_i + LANES]
                if idx == 0:
                    lr, li = sr, si
                else:
                    lr, li = sr + (a_r * lr - a_i * li), si + (a_r * li + a_i * lr)
                if idx + 1 < nj:
                    car_ref[q, slab(order[idx + 1]), lo_r:lo_r + LANES] = lr
                    car_ref[q, slab(order[idx + 1]), lo_i:lo_i + LANES] = li
            tab = [tab_ref[q, d, k] for k in range(SCAN_SLOTS)]
            cr, ci = zero, zero
            gr_rows, gi_rows = [None] * nvreg, [None] * nvreg
            for v in (range(nvreg) if down else range(nvreg - 1, -1, -1)):
                rows8 = slice(v * SUBLANES, (v + 1) * SUBLANES)
                gr_rows[v], gi_rows[v], cr, ci = scan_block(tab, lr[rows8], li[rows8], cr, ci, down)
            gr, gi = jnp.concatenate(gr_rows, axis=0), jnp.concatenate(gi_rows, axis=0)
            car_ref[q, slab(order[0]), lo_r:lo_r + LANES] = gr
            car_ref[q, slab(order[0]), lo_i:lo_i + LANES] = gi
            for idx, j in enumerate(order[1:], start=1):
                p_r = tile_rows(pw_ref[q, d, 2 * (idx - 1)])
                p_i = tile_rows(pw_ref[q, d, 2 * (idx - 1) + 1])
                car_ref[q, slab(j), lo_r:lo_r + LANES] += p_r * gr - p_i * gi
                car_ref[q, slab(j), lo_i:lo_i + LANES] += p_r * gi + p_i * gr

    kk = CHUNK * SSM_GROUP
    for q in range(npair):
        yi = _dot(car_ref[q].astype(BF16), cs_ref[q])
        for e in range(2):
            g = 2 * q + e
            yg_ref[g] = _dot(u2_ref[g], mi_ref[g]) + yi[:, e * kk:(e + 1) * kk]

    def relayout_out(i, _):
        j, r0 = i // parts, pl.multiple_of((i % parts) * t16, t16)
        c0 = pl.multiple_of(j * nrow + r0, t16)
        yrot = []
        for g in range(gpb):
            halves = []
            for hf in range(CHUNK // gpb):
                v = yg_ref[g, pl.ds(c0, t16), hf * LANES:(hf + 1) * LANES]
                halves.append(pltpu.roll(v, SSM_GROUP * g, axis=1) if g else v)
            yrot.append(halves)
        for t in range(CHUNK):
            hf, tt = divmod(t, gpb)
            acc = yrot[0][hf]
            for g in range(1, gpb):
                acc = jnp.where(masks[(g + tt) % gpb], yrot[g][hf], acc)
            y_ref[j, pl.ds(r0 * CHUNK + t, t16, stride=CHUNK), :] = acc
        return 0

    jax.lax.fori_loop(0, nj * parts, relayout_out, 0)


def _ssm_chunked(zs, m_intra, b_pair, c_pair, pw, tab, layer):
    bsz, nj, rows, _ = zs.shape
    nrow = rows // CHUNK
    nc = nj * nrow
    gpb, npair = GROUPS_PER_BLOCK, PAIRS_PER_BLOCK
    k = CHUNK * SSM_GROUP
    return pl.pallas_call(
        functools.partial(_ssm_kernel, nj=nj, nrow=nrow),
        grid=(D_SSM // LANES, bsz),
        in_specs=[pl.BlockSpec((None, nj, rows, LANES), lambda j, b: (b, 0, 0, j)),
                  pl.BlockSpec((None, gpb, k, k), lambda j, b: (layer, j, 0, 0)),
                  pl.BlockSpec((None, npair, 2 * k, 2 * k), lambda j, b: (layer, j, 0, 0)),
                  pl.BlockSpec((None, npair, 2 * k, 2 * k), lambda j, b: (layer, j, 0, 0)),
                  pl.BlockSpec((None, npair, 2, pw.shape[3], SUBLANES, LANES),
                               lambda j, b: (layer, j, 0, 0, 0, 0)),
                  pl.BlockSpec((None, npair, 2, SCAN_SLOTS, SUBLANES, LANES),
                               lambda j, b: (layer, j, 0, 0, 0, 0))],
        out_specs=pl.BlockSpec((None, nj, rows, LANES), lambda j, b: (b, 0, 0, j)),
        out_shape=jax.ShapeDtypeStruct((bsz, nj, rows, D_SSM), F32),
        scratch_shapes=[pltpu.VMEM((gpb, nc, k), BF16),
                        pltpu.VMEM((npair, nc, 2 * k), F32),
                        pltpu.VMEM((npair, nc, 2 * k), F32),
                        pltpu.VMEM((gpb, nc, k), F32)],
        compiler_params=pltpu.CompilerParams(
            dimension_semantics=("parallel", "parallel"), vmem_limit_bytes=VMEM_LIMIT),
        name="ssm_chunk",
    )(zs, m_intra, b_pair, c_pair, pw, tab)


def _ssm_tables(lam_re, lam_im, log_dt, b_re, b_im, c_re, c_im, d_skip, *, nj):
    t, h, p, g = CHUNK, SSM_GROUP, SSM_STATE, SSM_GROUPS
    nq = g // 2
    dt = jnp.exp(log_dt.astype(F32))[..., None]
    are = lam_re.astype(F32) * dt
    aim = lam_im.astype(F32) * dt

    def power(expo, a_re, a_im):
        mag = jnp.exp(a_re * expo)
        return mag * jnp.cos(a_im * expo), mag * jnp.sin(a_im * expo)

    lbr, lbi = power(1.0, are, aim)
    den = lam_re * lam_re + lam_im * lam_im
    cr = ((lbr - 1.0) * lam_re + lbi * lam_im) / den
    ci = (lbi * lam_re - (lbr - 1.0) * lam_im) / den
    bbr = cr[..., None] * b_re - ci[..., None] * b_im
    bbi = cr[..., None] * b_im + ci[..., None] * b_re

    def pair_rows(x):
        return x.reshape(2, 2, nq, 2 * p, x.shape[-1]).transpose(2, 0, 1, 3, 4)

    lane_pad = lambda x: jnp.pad(x, [(0, 0)] * (x.ndim - 1) + [(0, LANES - x.shape[-1])])
    steps = jnp.arange(t + 1, dtype=F32)
    pw = jnp.stack(power(steps, are[..., None], aim[..., None]))
    pl_tab = lane_pad(pair_rows(pw))
    pt_tab = jnp.pad(jnp.swapaxes(pair_rows(pw), -1, -2),
                     [(0, 0)] * 3 + [(0, LANES - t - 1), (0, 0)])
    ct_tab = lane_pad(pair_rows(jnp.stack([jnp.swapaxes(c_re.astype(F32), -1, -2),
                                           jnp.swapaxes(c_im.astype(F32), -1, -2)])))
    bt = jnp.stack([jnp.swapaxes(bbr, -1, -2), jnp.swapaxes(bbi, -1, -2)])
    bt = bt.reshape(2, 2, nq, 2, h, p)
    zero = jnp.zeros_like(bt[:, :, :, 0])
    bt_tab = jnp.stack([jnp.concatenate([bt[:, :, :, 0], zero], -1),
                        jnp.concatenate([zero, bt[:, :, :, 1]], -1)], axis=3)
    bt_tab = bt_tab.transpose(2, 0, 1, 3, 4, 5)
    dsk = jnp.pad(d_skip.astype(F32), ((0, 0), (0, t * h - h))).reshape(nq, 2, 1, t * h)

    pair_lanes = lambda x: x.reshape(2, nq, 1, 2 * p)
    ksteps = jnp.arange(1, max(nj, 2), dtype=F32)[:, None, None, None, None] * t
    pwr, pwi = power(ksteps, pair_lanes(are)[None], pair_lanes(aim)[None])
    pw = jnp.stack([pwr, pwi], axis=1).reshape(-1, 2, nq, 1, 2 * p)
    pw = jnp.broadcast_to(pw, pw.shape[:3] + (SUBLANES, 2 * p)).transpose(2, 1, 0, 3, 4)

    r8 = np.arange(SUBLANES)
    expo = np.stack([np.stack([np.full(SUBLANES, k), np.full(SUBLANES, k)]) for k in SCAN_STEPS]
                    + [np.stack([r8 + 1, SUBLANES - r8])])
    expo = jnp.asarray(t * nj * expo, F32)[:, :, None, :, None]
    ar, ai = power(expo, pair_lanes(are)[None], pair_lanes(aim)[None])
    live = np.stack([np.stack([r8 >= k, r8 <= SUBLANES - 1 - k]) for k in SCAN_STEPS]
                    + [np.ones((2, SUBLANES), bool)])
    live = jnp.asarray(live, F32)[:, :, None, :, None]
    ar, ai = ar * live, ai * live
    tab = jnp.stack([ar, ai], axis=1).reshape(SCAN_SLOTS, 2, nq, SUBLANES, 2 * p)
    tab = tab.transpose(2, 1, 0, 3, 4)
    return pl_tab, pt_tab, ct_tab, bt_tab, dsk, pw, tab


def _select_lanes(x, sel):
    hi = x.astype(BF16)
    lo = (x - hi.astype(F32)).astype(BF16)
    return _dot(hi, sel) + _dot(lo, sel)


def _select_rows(sel, x):
    hi = x.astype(BF16)
    lo = (x - hi.astype(F32)).astype(BF16)
    return _dot(sel, hi) + _dot(sel, lo)


def _roll_two_vregs(x, r):
    x0, x1 = x[:, :LANES], x[:, LANES:]
    if r >= LANES:
        x0, x1, r = x1, x0, r - LANES
    if r == 0:
        return jnp.concatenate([x0, x1], axis=1)
    lane = jax.lax.broadcasted_iota(jnp.int32, x0.shape, 1)
    a0, a1 = pltpu.roll(x0, r, axis=1), pltpu.roll(x1, r, axis=1)
    return jnp.concatenate([jnp.where(lane < r, a1, a0), jnp.where(lane < r, a0, a1)], axis=1)


def _ssm_prep_kernel(pl_ref, pt_ref, ct_ref, bt_ref, dsk_ref, sl_ref, sr_ref, e_ref,
                     m_ref, b_ref, c_ref):
    t, h = CHUNK, SSM_GROUP
    kk = t * h
    lane = jax.lax.broadcasted_iota(jnp.int32, (h, kk), 1)
    rowi = jax.lax.broadcasted_iota(jnp.int32, (h, kk), 0)
    half = jax.lax.broadcasted_iota(jnp.int32, (LANES, kk), 0) // SSM_STATE

    cl, cq = [], []
    for d in range(2):
        ctr, cti = _select_lanes(ct_ref[0, d], e_ref[...]), _select_lanes(ct_ref[1, d], e_ref[...])
        for out, pat in ((cl, d), (cq, 2 + d)):
            pr, pi = _select_lanes(pl_ref[0, d], sl_ref[pat]), _select_lanes(pl_ref[1, d], sl_ref[pat])
            out.append((ctr * pr - cti * pi, ctr * pi + cti * pr))

    for d in range(2):
        for ri, val in enumerate((cq[d][0], -cq[d][1])):
            r0 = (2 * d + ri) * LANES
            for gi in range(2):
                c_ref[r0:r0 + LANES, gi * kk:(gi + 1) * kk] = (
                    jnp.where(half == gi, val, 0.0).astype(BF16))

    for d in range(2):
        pwr, pwi = _select_rows(sr_ref[d], pt_ref[0, d]), _select_rows(sr_ref[d], pt_ref[1, d])
        for gi in range(2):
            btr = jnp.concatenate([bt_ref[0, d, gi]] * t, axis=0)
            bti = jnp.concatenate([bt_ref[1, d, gi]] * t, axis=0)
            r0, c0 = gi * kk, 2 * d * LANES
            b_ref[r0:r0 + kk, c0:c0 + LANES] = (btr * pwr - bti * pwi).astype(BF16)
            b_ref[r0:r0 + kk, c0 + LANES:c0 + 2 * LANES] = (btr * pwi + bti * pwr).astype(BF16)

    for gi in range(2):
        gen = [_dot(bt_ref[0, d, gi].astype(BF16), cl[d][0].astype(BF16))
               - _dot(bt_ref[1, d, gi].astype(BF16), cl[d][1].astype(BF16)) for d in range(2)]
        diag = jnp.where(lane < h, gen[1], 0.0) + jnp.where(rowi == lane, dsk_ref[gi], 0.0)
        kf, kb = gen[0] + diag, gen[1]
        for s in range(t):
            blk = jnp.where(lane >= h * s, _roll_two_vregs(kf, h * s), _roll_two_vregs(kb, h * s))
            m_ref[gi, s * h:(s + 1) * h, :] = blk.astype(BF16)


def _ssm_prep_constants():
    t, h = CHUNK, SSM_GROUP
    kk = t * h
    step = np.arange(kk) // h
    k = np.arange(LANES)
    pats = [step, (t - step) % t, step + 1, t - step]
    sel_lane = np.stack([(k[:, None] == pat[None, :]) for pat in pats]).astype(np.float32)
    sel_row = np.stack([(pat[:, None] == k[None, :]) for pat in (t - 1 - step, step)])
    tile_ho = (k[:, None] == (np.arange(kk) % h)[None, :])
    as_bf16 = lambda a: jnp.asarray(a, F32).astype(BF16)
    return as_bf16(sel_lane), as_bf16(sel_row), as_bf16(tile_ho)


def _ssm_prep(pl_tab, pt_tab, ct_tab, bt_tab, dsk):
    depth, nq = pl_tab.shape[:2]
    kk = CHUNK * SSM_GROUP
    sel_lane, sel_row, tile_ho = _ssm_prep_constants()
    tab5 = pl.BlockSpec((None, None, 2, 2, LANES, LANES), lambda l, q: (l, q, 0, 0, 0, 0))
    return pl.pallas_call(
        _ssm_prep_kernel,
        grid=(depth, nq),
        in_specs=[tab5, tab5, tab5,
                  pl.BlockSpec((None, None, 2, 2, 2, SSM_GROUP, LANES),
                               lambda l, q: (l, q, 0, 0, 0, 0, 0)),
                  pl.BlockSpec((None, None, 2, 1, kk), lambda l, q: (l, q, 0, 0, 0)),
                  _const_spec(sel_lane.shape), _const_spec(sel_row.shape),
                  _const_spec(tile_ho.shape)],
        out_specs=[pl.BlockSpec((None, 2, kk, kk), lambda l, q: (l, q, 0, 0)),
                   pl.BlockSpec((None, None, 2 * kk, 2 * kk), lambda l, q: (l, q, 0, 0)),
                   pl.BlockSpec((None, None, 2 * kk, 2 * kk), lambda l, q: (l, q, 0, 0))],
        out_shape=[jax.ShapeDtypeStruct((depth, 2 * nq, kk, kk), BF16),
                   jax.ShapeDtypeStruct((depth, nq, 2 * kk, 2 * kk), BF16),
                   jax.ShapeDtypeStruct((depth, nq, 2 * kk, 2 * kk), BF16)],
        compiler_params=pltpu.CompilerParams(
            dimension_semantics=("parallel", "parallel"), vmem_limit_bytes=VMEM_LIMIT),
        name="ssm_prep",
    )(pl_tab, pt_tab, ct_tab, bt_tab, dsk, sel_lane, sel_row, tile_ho)


MIX_ROWS = 1024
MIX_SUBBLOCKS = 2
FF_BLOCK = 1024
FFT2_K1_BLOCK = 2 * SUBLANES


def _mix_mlp_kernel(x_ref, gr_ref, gi_ref, ys_ref, bdr_ref, bdi_ref, fg_ref, wg_ref, bg_ref,
                    sg_ref, wo_ref, pmg_ref, plg_ref, w1_ref, w2_ref, pog_ref, o_ref, *, ff_blk,
                    nsub):
    n1s = x_ref.shape[0] // nsub
    rows = n1s * CHUNK

    def mix(k):
        r = slice(k * rows, (k + 1) * rows)
        s1 = slice(k * n1s, (k + 1) * n1s)
        gr = gr_ref[s1].reshape(rows, D_FOURIER)
        gi = gi_ref[s1].reshape(rows, D_FOURIER)
        yf = jnp.concatenate(
            [_dot(gr[:, c:c + MXU_DIM], bdr_ref[c:c + MXU_DIM, c:c + MXU_DIM])
             + _dot(gi[:, c:c + MXU_DIM], bdi_ref[c:c + MXU_DIM, c:c + MXU_DIM])
             for c in range(0, D_FOURIER, MXU_DIM)], axis=1)
        yf = _rms(yf, fg_ref[...])
        y = _unrotate_rows(ys_ref[r, :])
        gl = jax.nn.gelu(y, approximate=True)
        gate = _dot(gl.astype(BF16), wg_ref[...]) + bg_ref[...]
        ys = _rms(gl * jax.nn.sigmoid(gate), sg_ref[...])
        cat = jnp.concatenate([yf.astype(BF16), ys.astype(BF16)], axis=-1)
        x1 = x_ref[s1].reshape(rows, D_MODEL) + _rms(_dot(cat, wo_ref[...]), pmg_ref[...])
        return x1, _rms(x1, plg_ref[...]).astype(BF16)

    def mlp(k, x1, h):
        acc = jnp.zeros(x1.shape, F32)
        for j in range(D_FF // ff_blk):
            a = _dot(h, w1_ref[:, j * ff_blk:(j + 1) * ff_blk])
            a = jnp.square(jnp.maximum(a, 0.0)).astype(BF16)
            acc = acc + _dot(a, w2_ref[j * ff_blk:(j + 1) * ff_blk, :])
        o_ref[k * n1s:(k + 1) * n1s] = (x1 + _rms(acc, pog_ref[...])).reshape(n1s, CHUNK, D_MODEL)

    nxt = mix(0)
    for k in range(nsub):
        cur = nxt
        if k + 1 < nsub:
            nxt = mix(k + 1)
        mlp(k, *cur)


def _mix_mlp(x5, gri, ys, bdr, bdi, fg, wg, bg, sg, wo, pmg, plg, w1, w2, pog, ff_blk, layer):
    bsz, n1, nj, t, _ = x5.shape
    lspec = functools.partial(_layer_spec, layer=layer)
    slab = lambda c: pl.BlockSpec((None, n1, None, t, c), lambda b, j: (b, 0, j, 0, 0))
    return pl.pallas_call(
        functools.partial(_mix_mlp_kernel, ff_blk=ff_blk, nsub=MIX_SUBBLOCKS),
        grid=(bsz, nj),
        in_specs=[slab(D_MODEL),
                  pl.BlockSpec((None, None, n1, None, t, D_FOURIER),
                               lambda b, j: (0, b, 0, j, 0, 0)),
                  pl.BlockSpec((None, None, n1, None, t, D_FOURIER),
                               lambda b, j: (1, b, 0, j, 0, 0)),
                  pl.BlockSpec((None, None, n1 * t, D_SSM), lambda b, j: (b, j, 0, 0)),
                  lspec((D_FOURIER, D_FOURIER)), lspec((D_FOURIER, D_FOURIER)),
                  lspec((1, D_FOURIER)),
                  lspec((D_SSM, D_SSM)), lspec((1, D_SSM)), lspec((1, D_SSM)),
                  lspec((D_MODEL, D_MODEL)), lspec((1, D_MODEL)),
                  lspec((1, D_MODEL)),
                  lspec((D_MODEL, D_FF)), lspec((D_FF, D_MODEL)),
                  lspec((1, D_MODEL))],
        out_specs=slab(D_MODEL),
        out_shape=jax.ShapeDtypeStruct(x5.shape, F32),
        compiler_params=pltpu.CompilerParams(
            dimension_semantics=("parallel", "parallel"), vmem_limit_bytes=VMEM_LIMIT),
        name="mix_mlp",
    )(x5, gri, gri, ys, bdr, bdi, fg, wg, bg, sg, wo, pmg, plg, w1, w2, pog)


def _fourier_channel_maps(w_f, seq):
    c = np.arange(FOURIER_HEAD_DIM)
    ang = 2.0 * np.pi * ((c[:, None] * c[None, :]) % FOURIER_HEAD_DIM) / FOURIER_HEAD_DIM
    scale = 1.0 / math.sqrt(seq * FOURIER_HEAD_DIM)
    hp = jax.lax.Precision.HIGHEST
    cw = jnp.einsum("cd,hde->hce", jnp.asarray(np.cos(ang) * scale, F32), w_f.astype(F32), precision=hp)
    sw = jnp.einsum("cd,hde->hce", jnp.asarray(np.sin(ang) * scale, F32), w_f.astype(F32), precision=hp)
    eye = jnp.eye(FOURIER_HEADS, dtype=F32)
    bd = lambda m: jnp.einsum("hce,hk->hcke", m, eye).reshape(D_FOURIER, D_FOURIER)
    return bd(cw).astype(BF16), bd(sw).astype(BF16)


def kernel(x, w_in, w_out, pre_mix_g, post_mix_g, pre_mlp_g, post_mlp_g, fourier_out_g, ssm_out_g,
           w_fourier, lam_re, lam_im, log_dt, b_re, b_im, c_re, c_im, d_skip, w_glu, b_glu,
           w_ff1, w_ff2):
    bsz, seq, _ = x.shape
    depth = w_in.shape[0]
    n = bsz * seq
    n1, n2 = FFT_N1, seq // FFT_N1
    nj = n2 // CHUNK
    assert seq % (FFT_N1 * N2_BLK) == 0 and N2_BLK == CHUNK and n1 % RELAYOUT_CHUNKS == 0
    wk_tab, m2_tab = _fft_tables(seq)

    rows = lambda v: v.reshape(depth, 1, -1).astype(F32)
    w_in_b, w_out_b, w_glu_b = w_in.astype(BF16), w_out.astype(BF16), w_glu.astype(BF16)
    w_ff1_b, w_ff2_b = w_ff1.astype(BF16), w_ff2.astype(BF16)
    pl_tab, pt_tab, ct_tab, bt_tab, dsk, pw, tab = jax.vmap(
        functools.partial(_ssm_tables, nj=nj))(
            lam_re, lam_im, log_dt, b_re, b_im, c_re, c_im, d_skip)
    m_intra, b_pair, c_pair = _ssm_prep(pl_tab, pt_tab, ct_tab, bt_tab, dsk)
    bdr, bdi = jax.vmap(functools.partial(_fourier_channel_maps, seq=seq))(w_fourier)
    gains = [rows(v) for v in (pre_mix_g, fourier_out_g, b_glu, ssm_out_g, post_mix_g,
                               pre_mlp_g, post_mlp_g)]
    pre_mix, four_g, glu_b, ssm_g, post_mix, pre_mlp, post_mlp = gains

    xs = x.reshape(bsz, n1, n2, D_MODEL).astype(F32)
    for i in range(depth):
        y, zs = _in_proj(xs, pre_mix, w_in_b, wk_tab, i)
        g = _fft_stage2(y.reshape(bsz, n1, 2 * n2, D_FOURIER), m2_tab, kb=FFT2_K1_BLOCK)
        gri = g.reshape(2, bsz, n1, nj, CHUNK, D_FOURIER)
        ysr = _ssm_chunked(zs, m_intra, b_pair, c_pair, pw, tab, i)
        xs = _mix_mlp(xs.reshape(bsz, n1, nj, CHUNK, D_MODEL), gri, ysr, bdr, bdi, four_g,
                      w_glu_b, glu_b, ssm_g, w_out_b, post_mix, pre_mlp, w_ff1_b, w_ff2_b,
                      post_mlp, ff_blk=FF_BLOCK, layer=i).reshape(bsz, n1, n2, D_MODEL)
    return xs.reshape(bsz, seq, D_MODEL).astype(x.dtype)
```

```python
import functools
import math

import jax
import jax.numpy as jnp
import numpy as np
from jax.experimental import pallas as pl
from jax.experimental.pallas import tpu as pltpu

EPS = 1e-6
D_MODEL = 1024
D_FOURIER = 512
D_SSM = 512
FOURIER_HEADS = 8
FOURIER_HEAD_DIM = 64
SSM_GROUP = 16
SSM_GROUPS = 32
SSM_STATE = 64
D_FF = 4096

FFT_N1 = 64
CHUNK = 16
LANES = 128
SUBLANES = 8
MXU_DIM = 256
VMEM_LIMIT = 56 * 1024 * 1024

BF16 = jnp.bfloat16
F32 = jnp.float32


def _rms(x, g):
    return x * jax.lax.rsqrt(jnp.mean(x * x, axis=-1, keepdims=True) + EPS) * g


def _dot(a, b):
    return jnp.dot(a, b, preferred_element_type=F32)


def _const_spec(shape):
    nd = len(shape)
    return pl.BlockSpec(shape, lambda *_: (0,) * nd, pipeline_mode=pl.Buffered(1))


def _layer_spec(shape, layer):
    nd = len(shape)
    return pl.BlockSpec((None,) + tuple(shape), lambda *_: (layer,) + (0,) * nd,
                        pipeline_mode=pl.Buffered(1))


KRON_BLK = SUBLANES
N2_BLK = 2 * KRON_BLK
REGROUP_PITCH = N2_BLK + SUBLANES


def _in_proj_kernel(x_ref, g_ref, w_ref, wk_ref, y_ref, zs_ref, zt_ref):
    n1 = x_ref.shape[0]
    x = x_ref[...].reshape(n1 * N2_BLK, D_MODEL)
    h = _rms(x, g_ref[...]).astype(BF16)
    z = _dot(h, w_ref[...])
    zf = z[:, :D_FOURIER].reshape(n1, N2_BLK // KRON_BLK, KRON_BLK, D_FOURIER)
    parts = []
    for a in range(N2_BLK // KRON_BLK):
        za = zf[:, a].reshape(n1 * KRON_BLK, D_FOURIER).astype(BF16)
        ya = _dot(wk_ref[...], za)
        parts.append(ya.reshape(2 * n1, KRON_BLK, D_FOURIER))
    y = jnp.concatenate(parts, axis=1)
    y_ref[...] = y.astype(BF16).reshape(n1, 2, N2_BLK, D_FOURIER)
    zrot = _rotate_rows(z[:, D_FOURIER:])
    for s in range(D_SSM // LANES):
        for a in range(n1):
            zt_ref[s, a * REGROUP_PITCH:a * REGROUP_PITCH + CHUNK, :] = (
                zrot[a * CHUNK:(a + 1) * CHUNK, s * LANES:(s + 1) * LANES])
    for t in range(CHUNK):
        for s in range(D_SSM // LANES):
            zs_ref[t * n1:(t + 1) * n1, s * LANES:(s + 1) * LANES] = (
                zt_ref[s, pl.ds(t, n1, stride=REGROUP_PITCH), :])


def _in_proj(x4, g, w_bf16, wk, layer):
    bsz, n1, n2, _ = x4.shape
    rows = n1 * KRON_BLK
    return pl.pallas_call(
        _in_proj_kernel,
        grid=(bsz, n2 // N2_BLK),
        in_specs=[pl.BlockSpec((None, n1, N2_BLK, D_MODEL), lambda b, j: (b, 0, j, 0)),
                  _layer_spec((1, D_MODEL), layer),
                  _layer_spec((D_MODEL, D_MODEL), layer),
                  _const_spec((2 * rows, rows))],
        out_specs=[pl.BlockSpec((None, n1, 2, N2_BLK, D_FOURIER), lambda b, j: (b, 0, 0, j, 0)),
                   pl.BlockSpec((None, None, N2_BLK * n1, D_SSM), lambda b, j: (b, j, 0, 0))],
        out_shape=[jax.ShapeDtypeStruct((bsz, n1, 2, n2, D_FOURIER), BF16),
                   jax.ShapeDtypeStruct((bsz, n2 // N2_BLK, N2_BLK * n1, D_SSM), F32)],
        scratch_shapes=[pltpu.VMEM((D_SSM // LANES, REGROUP_PITCH * n1, LANES), F32)],
        compiler_params=pltpu.CompilerParams(
            dimension_semantics=("parallel", "parallel"), vmem_limit_bytes=VMEM_LIMIT),
        name="in_proj",
    )(x4, g, w_bf16, wk)


REGROUP_UNROLL = 8


def _fft2_kernel(m_ref, y_ref, g_ref, z_ref, *, kb, n2, pitch):
    nslab = D_FOURIER // LANES
    for j in range(kb):
        z = _dot(m_ref[j], y_ref[j])
        for ri in range(2):
            for s in range(nslab):
                z_ref[ri, s, j * pitch:j * pitch + n2, :] = (
                    z[ri * n2:(ri + 1) * n2, s * LANES:(s + 1) * LANES])

    def regroup(i, _):
        for u in range(REGROUP_UNROLL):
            k2 = i * REGROUP_UNROLL + u
            for ri in range(2):
                for s in range(nslab):
                    rows = [z_ref[ri, s, pl.ds(k2 + a * SUBLANES * pitch, SUBLANES, stride=pitch), :]
                            for a in range(kb // SUBLANES)]
                    g_ref[ri, k2, :, s * LANES:(s + 1) * LANES] = (
                        jnp.concatenate(rows, axis=0).astype(BF16))
        return 0

    jax.lax.fori_loop(0, n2 // REGROUP_UNROLL, regroup, 0)


def _fft_stage2(y, m2, kb):
    b, n1, rows, c = y.shape
    n2 = rows // 2
    pitch = n2 + SUBLANES
    return pl.pallas_call(
        functools.partial(_fft2_kernel, kb=kb, n2=n2, pitch=pitch),
        grid=(n1 // kb, b),
        in_specs=[pl.BlockSpec((kb, rows, rows), lambda j, i: (j, 0, 0)),
                  pl.BlockSpec((None, kb, rows, c), lambda j, i: (i, j, 0, 0))],
        out_specs=pl.BlockSpec((2, None, n2, kb, c), lambda j, i: (0, i, 0, j, 0)),
        out_shape=jax.ShapeDtypeStruct((2, b, n2, n1, c), BF16),
        scratch_shapes=[pltpu.VMEM((2, c // LANES, kb * pitch, LANES), F32)],
        compiler_params=pltpu.CompilerParams(
            dimension_semantics=("parallel", "parallel"), vmem_limit_bytes=VMEM_LIMIT),
        name="fft_stage2",
    )(m2, y)


def _fft_tables(seq):
    n1, n2 = FFT_N1, seq // FFT_N1
    k1 = np.arange(n1)
    ang1 = 2.0 * np.pi * ((k1[:, None] * k1[None, :]) % n1) / n1
    w1 = np.stack([np.cos(ang1), -np.sin(ang1)], axis=1).reshape(2 * n1, n1)
    wk = np.einsum("kn,ab->kanb", w1, np.eye(KRON_BLK)).reshape(2 * n1 * KRON_BLK, n1 * KRON_BLK)
    k2 = np.arange(n2)
    freq = k1[:, None, None] + n1 * k2[None, :, None]
    ang2 = 2.0 * np.pi * ((freq * k2[None, None, :]) % seq) / seq
    mr, mi = np.cos(ang2), -np.sin(ang2)
    m2 = np.concatenate([np.concatenate([mr, -mi], axis=2),
                         np.concatenate([mi, mr], axis=2)], axis=1)
    return jnp.asarray(wk, F32).astype(BF16), jnp.asarray(m2, F32).astype(BF16)


GROUPS_PER_BLOCK = LANES // SSM_GROUP
PAIRS_PER_BLOCK = GROUPS_PER_BLOCK // 2
RELAYOUT_CHUNKS = 8 * SUBLANES
SCAN_STEPS = (1, 2, 4)
SCAN_SLOTS = 2 * (len(SCAN_STEPS) + 1)


def _rotate_rows(v):
    return jnp.concatenate(
        [pltpu.roll(v[:, c:c + LANES], 0, axis=1, stride=SSM_GROUP, stride_axis=0)
         for c in range(0, v.shape[1], LANES)], axis=1)


def _unrotate_rows(v):
    row = jax.lax.broadcasted_iota(jnp.int32, v.shape, 0)
    cols = []
    for c in range(0, v.shape[1], LANES):
        w = v[:, c:c + LANES]
        for bit in (1, 2, 4):
            w = jnp.where((row[:, :LANES] & bit) != 0,
                          pltpu.roll(w, LANES - SSM_GROUP * bit, axis=1), w)
        cols.append(w)
    return jnp.concatenate(cols, axis=1)


def _ssm_kernel(z_ref, mi_ref, bs_ref, cs_ref, pw_ref, tab_ref, y_ref,
                u2_ref, s_ref, car_ref, yg_ref, *, nj, nrow):
    gpb, npair = GROUPS_PER_BLOCK, PAIRS_PER_BLOCK
    t16 = RELAYOUT_CHUNKS
    parts = nrow // t16
    lane_blk = jax.lax.broadcasted_iota(jnp.int32, (t16, LANES), 1) // SSM_GROUP
    masks = [lane_blk == m for m in range(gpb)]

    def relayout_in(i, _):
        j, r0 = i // parts, pl.multiple_of((i % parts) * t16, t16)
        ut = [z_ref[j, pl.ds(t * nrow + r0, t16), :] for t in range(CHUNK)]
        c0 = pl.multiple_of(j * nrow + r0, t16)
        for g in range(gpb):
            for hf in range(CHUNK // gpb):
                acc = ut[gpb * hf]
                for tt in range(1, gpb):
                    acc = jnp.where(masks[(g + tt) % gpb], ut[gpb * hf + tt], acc)
                if g:
                    acc = pltpu.roll(acc, LANES - SSM_GROUP * g, axis=1)
                u2_ref[g, pl.ds(c0, t16), hf * LANES:(hf + 1) * LANES] = acc.astype(BF16)
        return 0

    jax.lax.fori_loop(0, nj * parts, relayout_in, 0)

    for q in range(npair):
        up = jnp.concatenate([u2_ref[2 * q], u2_ref[2 * q + 1]], axis=1)
        s_ref[q] = _dot(up, bs_ref[q])

    row = jax.lax.broadcasted_iota(jnp.int32, (SUBLANES, LANES), 0)
    nvreg = nrow // SUBLANES

    def scan_block(tab, sr, si, cr, ci, down):
        keep = (lambda sh: row >= sh) if down else (lambda sh: row <= SUBLANES - 1 - sh)
        amt = (lambda sh: sh) if down else (lambda sh: SUBLANES - sh)
        for k, sh in enumerate(SCAN_STEPS):
            tr = pltpu.roll(sr, amt(sh), axis=0)
            ti = pltpu.roll(si, amt(sh), axis=0)
            ar, ai = tab[2 * k], tab[2 * k + 1]
            sr, si = sr + (ar * tr - ai * ti), si + (ar * ti + ai * tr)
        er, ei = tab[SCAN_SLOTS - 2], tab[SCAN_SLOTS - 1]
        fr = sr + (er * cr - ei * ci)
        fi = si + (er * ci + ei * cr)
        outr = jnp.where(keep(1), pltpu.roll(fr, amt(1), axis=0), cr)
        outi = jnp.where(keep(1), pltpu.roll(fi, amt(1), axis=0), ci)
        edge = SUBLANES - 1 if down else 0
        ncr = jnp.broadcast_to(fr[edge:edge + 1, :], (SUBLANES, LANES))
        nci = jnp.broadcast_to(fi[edge:edge + 1, :], (SUBLANES, LANES))
        return outr, outi, ncr, nci

    def tile_rows(x):
        return jnp.concatenate([x] * nvreg, axis=0)

    zero = jnp.zeros((SUBLANES, LANES), F32)
    for q in range(npair):
        for d in range(2):
            down = d == 0
            lo_r, lo_i = 2 * d * LANES, (2 * d + 1) * LANES
            slab = lambda j: slice(j * nrow, (j + 1) * nrow)
            order = list(range(nj)) if down else list(range(nj - 1, -1, -1))
            a_r, a_i = tile_rows(pw_ref[q, d, 0]), tile_rows(pw_ref[q, d, 1])
            lr = li = None
            for idx, j in enumerate(order):
                sr = s_ref[q, slab(j), lo_r:lo_r + LANES]
                si = s_ref[q, slab(j), lo_i:lo_i + LANES]
                if idx == 0:
                    lr, li = sr, si
                else:
                    lr, li = sr + (a_r * lr - a_i * li), si + (a_r * li + a_i * lr)
                if idx + 1 < nj:
                    car_ref[q, slab(order[idx + 1]), lo_r:lo_r + LANES] = lr
                    car_ref[q, slab(order[idx + 1]), lo_i:lo_i + LANES] = li
            tab = [tab_ref[q, d, k] for k in range(SCAN_SLOTS)]
            cr, ci = zero, zero
            gr_rows, gi_rows = [None] * nvreg, [None] * nvreg
            for v in (range(nvreg) if down else range(nvreg - 1, -1, -1)):
                rows8 = slice(v * SUBLANES, (v + 1) * SUBLANES)
                gr_rows[v], gi_rows[v], cr, ci = scan_block(tab, lr[rows8], li[rows8], cr, ci, down)
            gr, gi = jnp.concatenate(gr_rows, axis=0), jnp.concatenate(gi_rows, axis=0)
            car_ref[q, slab(order[0]), lo_r:lo_r + LANES] = gr
            car_ref[q, slab(order[0]), lo_i:lo_i + LANES] = gi
            for idx, j in enumerate(order[1:], start=1):
                p_r = tile_rows(pw_ref[q, d, 2 * (idx - 1)])
                p_i = tile_rows(pw_ref[q, d, 2 * (idx - 1) + 1])
                car_ref[q, slab(j), lo_r:lo_r + LANES] += p_r * gr - p_i * gi
                car_ref[q, slab(j), lo_i:lo_i + LANES] += p_r * gi + p_i * gr

    kk = CHUNK * SSM_GROUP
    for q in range(npair):
        yi = _dot(car_ref[q].astype(BF16), cs_ref[q])
        for e in range(2):
            g = 2 * q + e
            yg_ref[g] = _dot(u2_ref[g], mi_ref[g]) + yi[:, e * kk:(e + 1) * kk]

    def relayout_out(i, _):
        j, r0 = i // parts, pl.multiple_of((i % parts) * t16, t16)
        c0 = pl.multiple_of(j * nrow + r0, t16)
        yrot = []
        for g in range(gpb):
            halves = []
            for hf in range(CHUNK // gpb):
                v = yg_ref[g, pl.ds(c0, t16), hf * LANES:(hf + 1) * LANES]
                halves.append(pltpu.roll(v, SSM_GROUP * g, axis=1) if g else v)
            yrot.append(halves)
        for t in range(CHUNK):
            hf, tt = divmod(t, gpb)
            acc = yrot[0][hf]
            for g in range(1, gpb):
                acc = jnp.where(masks[(g + tt) % gpb], yrot[g][hf], acc)
            y_ref[j, pl.ds(r0 * CHUNK + t, t16, stride=CHUNK), :] = acc
        return 0

    jax.lax.fori_loop(0, nj * parts, relayout_out, 0)


def _ssm_chunked(zs, m_intra, b_pair, c_pair, pw, tab, layer):
    bsz, nj, rows, _ = zs.shape
    nrow = rows // CHUNK
    nc = nj * nrow
    gpb, npair = GROUPS_PER_BLOCK, PAIRS_PER_BLOCK
    k = CHUNK * SSM_GROUP
    return pl.pallas_call(
        functools.partial(_ssm_kernel, nj=nj, nrow=nrow),
        grid=(D_SSM // LANES, bsz),
        in_specs=[pl.BlockSpec((None, nj, rows, LANES), lambda j, b: (b, 0, 0, j)),
                  pl.BlockSpec((None, gpb, k, k), lambda j, b: (layer, j, 0, 0)),
                  pl.BlockSpec((None, npair, 2 * k, 2 * k), lambda j, b: (layer, j, 0, 0)),
                  pl.BlockSpec((None, npair, 2 * k, 2 * k), lambda j, b: (layer, j, 0, 0)),
                  pl.BlockSpec((None, npair, 2, pw.shape[3], SUBLANES, LANES),
                               lambda j, b: (layer, j, 0, 0, 0, 0)),
                  pl.BlockSpec((None, npair, 2, SCAN_SLOTS, SUBLANES, LANES),
                               lambda j, b: (layer, j, 0, 0, 0, 0))],
        out_specs=pl.BlockSpec((None, nj, rows, LANES), lambda j, b: (b, 0, 0, j)),
        out_shape=jax.ShapeDtypeStruct((bsz, nj, rows, D_SSM), F32),
        scratch_shapes=[pltpu.VMEM((gpb, nc, k), BF16),
                        pltpu.VMEM((npair, nc, 2 * k), F32),
                        pltpu.VMEM((npair, nc, 2 * k), F32),
                        pltpu.VMEM((gpb, nc, k), F32)],
        compiler_params=pltpu.CompilerParams(
            dimension_semantics=("parallel", "parallel"), vmem_limit_bytes=VMEM_LIMIT),
        name="ssm_chunk",
    )(zs, m_intra, b_pair, c_pair, pw, tab)


def _ssm_tables(lam_re, lam_im, log_dt, b_re, b_im, c_re, c_im, d_skip, *, nj):
    t, h, p, g = CHUNK, SSM_GROUP, SSM_STATE, SSM_GROUPS
    nq = g // 2
    dt = jnp.exp(log_dt.astype(F32))[..., None]
    are = lam_re.astype(F32) * dt
    aim = lam_im.astype(F32) * dt

    def power(expo, a_re, a_im):
        mag = jnp.exp(a_re * expo)
        return mag * jnp.cos(a_im * expo), mag * jnp.sin(a_im * expo)

    lbr, lbi = power(1.0, are, aim)
    den = lam_re * lam_re + lam_im * lam_im
    cr = ((lbr - 1.0) * lam_re + lbi * lam_im) / den
    ci = (lbi * lam_re - (lbr - 1.0) * lam_im) / den
    bbr = cr[..., None] * b_re - ci[..., None] * b_im
    bbi = cr[..., None] * b_im + ci[..., None] * b_re

    def pair_rows(x):
        return x.reshape(2, 2, nq, 2 * p, x.shape[-1]).transpose(2, 0, 1, 3, 4)

    lane_pad = lambda x: jnp.pad(x, [(0, 0)] * (x.ndim - 1) + [(0, LANES - x.shape[-1])])
    steps = jnp.arange(t + 1, dtype=F32)
    pw = jnp.stack(power(steps, are[..., None], aim[..., None]))
    pl_tab = lane_pad(pair_rows(pw))
    pt_tab = jnp.pad(jnp.swapaxes(pair_rows(pw), -1, -2),
                     [(0, 0)] * 3 + [(0, LANES - t - 1), (0, 0)])
    ct_tab = lane_pad(pair_rows(jnp.stack([jnp.swapaxes(c_re.astype(F32), -1, -2),
                                           jnp.swapaxes(c_im.astype(F32), -1, -2)])))
    bt = jnp.stack([jnp.swapaxes(bbr, -1, -2), jnp.swapaxes(bbi, -1, -2)])
    bt = bt.reshape(2, 2, nq, 2, h, p)
    zero = jnp.zeros_like(bt[:, :, :, 0])
    bt_tab = jnp.stack([jnp.concatenate([bt[:, :, :, 0], zero], -1),
                        jnp.concatenate([zero, bt[:, :, :, 1]], -1)], axis=3)
    bt_tab = bt_tab.transpose(2, 0, 1, 3, 4, 5)
    dsk = jnp.pad(d_skip.astype(F32), ((0, 0), (0, t * h - h))).reshape(nq, 2, 1, t * h)

    pair_lanes = lambda x: x.reshape(2, nq, 1, 2 * p)
    ksteps = jnp.arange(1, max(nj, 2), dtype=F32)[:, None, None, None, None] * t
    pwr, pwi = power(ksteps, pair_lanes(are)[None], pair_lanes(aim)[None])
    pw = jnp.stack([pwr, pwi], axis=1).reshape(-1, 2, nq, 1, 2 * p)
    pw = jnp.broadcast_to(pw, pw.shape[:3] + (SUBLANES, 2 * p)).transpose(2, 1, 0, 3, 4)

    r8 = np.arange(SUBLANES)
    expo = np.stack([np.stack([np.full(SUBLANES, k), np.full(SUBLANES, k)]) for k in SCAN_STEPS]
                    + [np.stack([r8 + 1, SUBLANES - r8])])
    expo = jnp.asarray(t * nj * expo, F32)[:, :, None, :, None]
    ar, ai = power(expo, pair_lanes(are)[None], pair_lanes(aim)[None])
    live = np.stack([np.stack([r8 >= k, r8 <= SUBLANES - 1 - k]) for k in SCAN_STEPS]
                    + [np.ones((2, SUBLANES), bool)])
    live = jnp.asarray(live, F32)[:, :, None, :, None]
    ar, ai = ar * live, ai * live
    tab = jnp.stack([ar, ai], axis=1).reshape(SCAN_SLOTS, 2, nq, SUBLANES, 2 * p)
    tab = tab.transpose(2, 1, 0, 3, 4)
    return pl_tab, pt_tab, ct_tab, bt_tab, dsk, pw, tab


def _select_lanes(x, sel):
    hi = x.astype(BF16)
    lo = (x - hi.astype(F32)).astype(BF16)
    return _dot(hi, sel) + _dot(lo, sel)


def _select_rows(sel, x):
    hi = x.astype(BF16)
    lo = (x - hi.astype(F32)).astype(BF16)
    return _dot(sel, hi) + _dot(sel, lo)


def _roll_two_vregs(x, r):
    x0, x1 = x[:, :LANES], x[:, LANES:]
    if r >= LANES:
        x0, x1, r = x1, x0, r - LANES
    if r == 0:
        return jnp.concatenate([x0, x1], axis=1)
    lane = jax.lax.broadcasted_iota(jnp.int32, x0.shape, 1)
    a0, a1 = pltpu.roll(x0, r, axis=1), pltpu.roll(x1, r, axis=1)
    return jnp.concatenate([jnp.where(lane < r, a1, a0), jnp.where(lane < r, a0, a1)], axis=1)


def _ssm_prep_kernel(pl_ref, pt_ref, ct_ref, bt_ref, dsk_ref, sl_ref, sr_ref, e_ref,
                     m_ref, b_ref, c_ref):
    t, h = CHUNK, SSM_GROUP
    kk = t * h
    lane = jax.lax.broadcasted_iota(jnp.int32, (h, kk), 1)
    rowi = jax.lax.broadcasted_iota(jnp.int32, (h, kk), 0)
    half = jax.lax.broadcasted_iota(jnp.int32, (LANES, kk), 0) // SSM_STATE

    cl, cq = [], []
    for d in range(2):
        ctr, cti = _select_lanes(ct_ref[0, d], e_ref[...]), _select_lanes(ct_ref[1, d], e_ref[...])
        for out, pat in ((cl, d), (cq, 2 + d)):
            pr, pi = _select_lanes(pl_ref[0, d], sl_ref[pat]), _select_lanes(pl_ref[1, d], sl_ref[pat])
            out.append((ctr * pr - cti * pi, ctr * pi + cti * pr))

    for d in range(2):
        for ri, val in enumerate((cq[d][0], -cq[d][1])):
            r0 = (2 * d + ri) * LANES
            for gi in range(2):
                c_ref[r0:r0 + LANES, gi * kk:(gi + 1) * kk] = (
                    jnp.where(half == gi, val, 0.0).astype(BF16))

    for d in range(2):
        pwr, pwi = _select_rows(sr_ref[d], pt_ref[0, d]), _select_rows(sr_ref[d], pt_ref[1, d])
        for gi in range(2):
            btr = jnp.concatenate([bt_ref[0, d, gi]] * t, axis=0)
            bti = jnp.concatenate([bt_ref[1, d, gi]] * t, axis=0)
            r0, c0 = gi * kk, 2 * d * LANES
            b_ref[r0:r0 + kk, c0:c0 + LANES] = (btr * pwr - bti * pwi).astype(BF16)
            b_ref[r0:r0 + kk, c0 + LANES:c0 + 2 * LANES] = (btr * pwi + bti * pwr).astype(BF16)

    for gi in range(2):
        gen = [_dot(bt_ref[0, d, gi].astype(BF16), cl[d][0].astype(BF16))
               - _dot(bt_ref[1, d, gi].astype(BF16), cl[d][1].astype(BF16)) for d in range(2)]
        diag = jnp.where(lane < h, gen[1], 0.0) + jnp.where(rowi == lane, dsk_ref[gi], 0.0)
        kf, kb = gen[0] + diag, gen[1]
        for s in range(t):
            blk = jnp.where(lane >= h * s, _roll_two_vregs(kf, h * s), _roll_two_vregs(kb, h * s))
            m_ref[gi, s * h:(s + 1) * h, :] = blk.astype(BF16)


def _ssm_prep_constants():
    t, h = CHUNK, SSM_GROUP
    kk = t * h
    step = np.arange(kk) // h
    k = np.arange(LANES)
    pats = [step, (t - step) % t, step + 1, t - step]
    sel_lane = np.stack([(k[:, None] == pat[None, :]) for pat in pats]).astype(np.float32)
    sel_row = np.stack([(pat[:, None] == k[None, :]) for pat in (t - 1 - step, step)])
    tile_ho = (k[:, None] == (np.arange(kk) % h)[None, :])
    as_bf16 = lambda a: jnp.asarray(a, F32).astype(BF16)
    return as_bf16(sel_lane), as_bf16(sel_row), as_bf16(tile_ho)


def _ssm_prep(pl_tab, pt_tab, ct_tab, bt_tab, dsk):
    depth, nq = pl_tab.shape[:2]
    kk = CHUNK * SSM_GROUP
    sel_lane, sel_row, tile_ho = _ssm_prep_constants()
    tab5 = pl.BlockSpec((None, None, 2, 2, LANES, LANES), lambda l, q: (l, q, 0, 0, 0, 0))
    return pl.pallas_call(
        _ssm_prep_kernel,
        grid=(depth, nq),
        in_specs=[tab5, tab5, tab5,
                  pl.BlockSpec((None, None, 2, 2, 2, SSM_GROUP, LANES),
                               lambda l, q: (l, q, 0, 0, 0, 0, 0)),
                  pl.BlockSpec((None, None, 2, 1, kk), lambda l, q: (l, q, 0, 0, 0)),
                  _const_spec(sel_lane.shape), _const_spec(sel_row.shape),
                  _const_spec(tile_ho.shape)],
        out_specs=[pl.BlockSpec((None, 2, kk, kk), lambda l, q: (l, q, 0, 0)),
                   pl.BlockSpec((None, None, 2 * kk, 2 * kk), lambda l, q: (l, q, 0, 0)),
                   pl.BlockSpec((None, None, 2 * kk, 2 * kk), lambda l, q: (l, q, 0, 0))],
        out_shape=[jax.ShapeDtypeStruct((depth, 2 * nq, kk, kk), BF16),
                   jax.ShapeDtypeStruct((depth, nq, 2 * kk, 2 * kk), BF16),
                   jax.ShapeDtypeStruct((depth, nq, 2 * kk, 2 * kk), BF16)],
        compiler_params=pltpu.CompilerParams(
            dimension_semantics=("parallel", "parallel"), vmem_limit_bytes=VMEM_LIMIT),
        name="ssm_prep",
    )(pl_tab, pt_tab, ct_tab, bt_tab, dsk, sel_lane, sel_row, tile_ho)


MIX_SUBBLOCKS = 2
FF_BLOCK = 1024
FFT2_K1_BLOCK = 2 * SUBLANES


def _mix_mlp_kernel(x_ref, gr_ref, gi_ref, ys_ref, bdr_ref, bdi_ref, fg_ref, wg_ref, bg_ref,
                    sg_ref, wo_ref, pmg_ref, plg_ref, w1_ref, w2_ref, pog_ref, o_ref, *, ff_blk,
                    nsub):
    n1s = x_ref.shape[0] // nsub
    rows = n1s * CHUNK

    def mix(k):
        r = slice(k * rows, (k + 1) * rows)
        s1 = slice(k * n1s, (k + 1) * n1s)
        gr = gr_ref[s1].reshape(rows, D_FOURIER)
        gi = gi_ref[s1].reshape(rows, D_FOURIER)
        yf = jnp.concatenate(
            [_dot(gr[:, c:c + MXU_DIM], bdr_ref[c:c + MXU_DIM, c:c + MXU_DIM])
             + _dot(gi[:, c:c + MXU_DIM], bdi_ref[c:c + MXU_DIM, c:c + MXU_DIM])
             for c in range(0, D_FOURIER, MXU_DIM)], axis=1)
        yf = _rms(yf, fg_ref[...])
        y = _unrotate_rows(ys_ref[r, :])
        gl = jax.nn.gelu(y, approximate=True)
        gate = _dot(gl.astype(BF16), wg_ref[...]) + bg_ref[...]
        ys = _rms(gl * jax.nn.sigmoid(gate), sg_ref[...])
        cat = jnp.concatenate([yf.astype(BF16), ys.astype(BF16)], axis=-1)
        x1 = x_ref[s1].reshape(rows, D_MODEL) + _rms(_dot(cat, wo_ref[...]), pmg_ref[...])
        return x1, _rms(x1, plg_ref[...]).astype(BF16)

    def mlp(k, x1, h):
        acc = jnp.zeros(x1.shape, F32)
        for j in range(D_FF // ff_blk):
            a = _dot(h, w1_ref[:, j * ff_blk:(j + 1) * ff_blk])
            a = jnp.square(jnp.maximum(a, 0.0)).astype(BF16)
            acc = acc + _dot(a, w2_ref[j * ff_blk:(j + 1) * ff_blk, :])
        o_ref[k * n1s:(k + 1) * n1s] = (x1 + _rms(acc, pog_ref[...])).reshape(n1s, CHUNK, D_MODEL)

    nxt = mix(0)
    for k in range(nsub):
        cur = nxt
        if k + 1 < nsub:
            nxt = mix(k + 1)
        mlp(k, *cur)


def _mix_mlp(x5, gri, ys, bdr, bdi, fg, wg, bg, sg, wo, pmg, plg, w1, w2, pog, ff_blk, layer):
    bsz, n1, nj, t, _ = x5.shape
    lspec = functools.partial(_layer_spec, layer=layer)
    slab = lambda c: pl.BlockSpec((None, n1, None, t, c), lambda b, j: (b, 0, j, 0, 0))
    return pl.pallas_call(
        functools.partial(_mix_mlp_kernel, ff_blk=ff_blk, nsub=MIX_SUBBLOCKS),
        grid=(bsz, nj),
        in_specs=[slab(D_MODEL),
                  pl.BlockSpec((None, None, n1, None, t, D_FOURIER),
                               lambda b, j: (0, b, 0, j, 0, 0)),
                  pl.BlockSpec((None, None, n1, None, t, D_FOURIER),
                               lambda b, j: (1, b, 0, j, 0, 0)),
                  pl.BlockSpec((None, None, n1 * t, D_SSM), lambda b, j: (b, j, 0, 0)),
                  lspec((D_FOURIER, D_FOURIER)), lspec((D_FOURIER, D_FOURIER)),
                  lspec((1, D_FOURIER)),
                  lspec((D_SSM, D_SSM)), lspec((1, D_SSM)), lspec((1, D_SSM)),
                  lspec((D_MODEL, D_MODEL)), lspec((1, D_MODEL)),
                  lspec((1, D_MODEL)),
                  lspec((D_MODEL, D_FF)), lspec((D_FF, D_MODEL)),
                  lspec((1, D_MODEL))],
        out_specs=slab(D_MODEL),
        out_shape=jax.ShapeDtypeStruct(x5.shape, F32),
        compiler_params=pltpu.CompilerParams(
            dimension_semantics=("parallel", "parallel"), vmem_limit_bytes=VMEM_LIMIT),
        name="mix_mlp",
    )(x5, gri, gri, ys, bdr, bdi, fg, wg, bg, sg, wo, pmg, plg, w1, w2, pog)


def _fourier_channel_maps(w_f, seq):
    c = np.arange(FOURIER_HEAD_DIM)
    ang = 2.0 * np.pi * ((c[:, None] * c[None, :]) % FOURIER_HEAD_DIM) / FOURIER_HEAD_DIM
    scale = 1.0 / math.sqrt(seq * FOURIER_HEAD_DIM)
    hp = jax.lax.Precision.HIGHEST
    cw = jnp.einsum("cd,hde->hce", jnp.asarray(np.cos(ang) * scale, F32), w_f.astype(F32), precision=hp)
    sw = jnp.einsum("cd,hde->hce", jnp.asarray(np.sin(ang) * scale, F32), w_f.astype(F32), precision=hp)
    eye = jnp.eye(FOURIER_HEADS, dtype=F32)
    bd = lambda m: jnp.einsum("hce,hk->hcke", m, eye).reshape(D_FOURIER, D_FOURIER)
    return bd(cw).astype(BF16), bd(sw).astype(BF16)


def kernel(x, w_in, w_out, pre_mix_g, post_mix_g, pre_mlp_g, post_mlp_g, fourier_out_g, ssm_out_g,
           w_fourier, lam_re, lam_im, log_dt, b_re, b_im, c_re, c_im, d_skip, w_glu, b_glu,
           w_ff1, w_ff2):
    bsz, seq, _ = x.shape
    depth = w_in.shape[0]
    n = bsz * seq
    n1, n2 = FFT_N1, seq // FFT_N1
    nj = n2 // CHUNK
    assert seq % (FFT_N1 * N2_BLK) == 0 and N2_BLK == CHUNK and n1 % RELAYOUT_CHUNKS == 0
    wk_tab, m2_tab = _fft_tables(seq)

    rows = lambda v: v.reshape(depth, 1, -1).astype(F32)
    w_in_b, w_out_b, w_glu_b = w_in.astype(BF16), w_out.astype(BF16), w_glu.astype(BF16)
    w_ff1_b, w_ff2_b = w_ff1.astype(BF16), w_ff2.astype(BF16)
    pl_tab, pt_tab, ct_tab, bt_tab, dsk, pw, tab = jax.vmap(
        functools.partial(_ssm_tables, nj=nj))(
            lam_re, lam_im, log_dt, b_re, b_im, c_re, c_im, d_skip)
    m_intra, b_pair, c_pair = _ssm_prep(pl_tab, pt_tab, ct_tab, bt_tab, dsk)
    bdr, bdi = jax.vmap(functools.partial(_fourier_channel_maps, seq=seq))(w_fourier)
    gains = [rows(v) for v in (pre_mix_g, fourier_out_g, b_glu, ssm_out_g, post_mix_g,
                               pre_mlp_g, post_mlp_g)]
    pre_mix, four_g, glu_b, ssm_g, post_mix, pre_mlp, post_mlp = gains

    xs = x.reshape(bsz, n1, n2, D_MODEL).astype(F32)
    for i in range(depth):
        y, zs = _in_proj(xs, pre_mix, w_in_b, wk_tab, i)
        g = _fft_stage2(y.reshape(bsz, n1, 2 * n2, D_FOURIER), m2_tab, kb=FFT2_K1_BLOCK)
        gri = g.reshape(2, bsz, n1, nj, CHUNK, D_FOURIER)
        ysr = _ssm_chunked(zs, m_intra, b_pair, c_pair, pw, tab, i)
        xs = _mix_mlp(xs.reshape(bsz, n1, nj, CHUNK, D_MODEL), gri, ysr, bdr, bdi, four_g,
                      w_glu_b, glu_b, ssm_g, w_out_b, post_mix, pre_mlp, w_ff1_b, w_ff2_b,
                      post_mlp, ff_blk=FF_BLOCK, layer=i).reshape(bsz, n1, n2, D_MODEL)
    return xs.reshape(bsz, seq, D_MODEL).astype(x.dtype)
```

```python
import functools
import math

import jax
import jax.numpy as jnp
import numpy as np
from jax.experimental import pallas as pl
from jax.experimental.pallas import tpu as pltpu

EPS = 1e-6
D_MODEL = 1024
D_FOURIER = 512
D_SSM = 512
FOURIER_HEADS = 8
FOURIER_HEAD_DIM = 64
SSM_GROUP = 16
SSM_GROUPS = 32
SSM_STATE = 64
D_FF = 4096

FFT_N1 = 64
CHUNK = 16
LANES = 128
SUBLANES = 8
MXU_DIM = 256
VMEM_LIMIT = 56 * 1024 * 1024

BF16 = jnp.bfloat16
F32 = jnp.float32


def _rms(x, g):
    return x * jax.lax.rsqrt(jnp.mean(x * x, axis=-1, keepdims=True) + EPS) * g


def _dot(a, b):
    return jnp.dot(a, b, preferred_element_type=F32)


def _const_spec(shape):
    nd = len(shape)
    return pl.BlockSpec(shape, lambda *_: (0,) * nd, pipeline_mode=pl.Buffered(1))


def _layer_spec(shape, layer):
    nd = len(shape)
    return pl.BlockSpec((None,) + tuple(shape), lambda *_: (layer,) + (0,) * nd,
                        pipeline_mode=pl.Buffered(1))


KRON_BLK = SUBLANES
N2_BLK = 2 * KRON_BLK
REGROUP_PITCH = N2_BLK + SUBLANES


def _in_proj_kernel(x_ref, g_ref, w_ref, wk_ref, y_ref, zs_ref, zt_ref):
    n1 = x_ref.shape[0]
    x = x_ref[...].reshape(n1 * N2_BLK, D_MODEL)
    h = _rms(x, g_ref[...]).astype(BF16)
    z = _dot(h, w_ref[...])
    zf = z[:, :D_FOURIER].reshape(n1, N2_BLK // KRON_BLK, KRON_BLK, D_FOURIER)
    parts = []
    for a in range(N2_BLK // KRON_BLK):
        za = zf[:, a].reshape(n1 * KRON_BLK, D_FOURIER).astype(BF16)
        ya = _dot(wk_ref[...], za)
        parts.append(ya.reshape(2 * n1, KRON_BLK, D_FOURIER))
    y = jnp.concatenate(parts, axis=1)
    y_ref[...] = y.astype(BF16).reshape(n1, 2, N2_BLK, D_FOURIER)
    zrot = _rotate_rows(z[:, D_FOURIER:])
    for s in range(D_SSM // LANES):
        for a in range(n1):
            zt_ref[s, a * REGROUP_PITCH:a * REGROUP_PITCH + CHUNK, :] = (
                zrot[a * CHUNK:(a + 1) * CHUNK, s * LANES:(s + 1) * LANES])
    for t in range(CHUNK):
        for s in range(D_SSM // LANES):
            zs_ref[t * n1:(t + 1) * n1, s * LANES:(s + 1) * LANES] = (
                zt_ref[s, pl.ds(t, n1, stride=REGROUP_PITCH), :])


def _in_proj(x4, g, w_bf16, wk, layer):
    bsz, n1, n2, _ = x4.shape
    rows = n1 * KRON_BLK
    return pl.pallas_call(
        _in_proj_kernel,
        grid=(bsz, n2 // N2_BLK),
        in_specs=[pl.BlockSpec((None, n1, N2_BLK, D_MODEL), lambda b, j: (b, 0, j, 0)),
                  _layer_spec((1, D_MODEL), layer),
                  _layer_spec((D_MODEL, D_MODEL), layer),
                  _const_spec((2 * rows, rows))],
        out_specs=[pl.BlockSpec((None, n1, 2, N2_BLK, D_FOURIER), lambda b, j: (b, 0, 0, j, 0)),
                   pl.BlockSpec((None, None, N2_BLK * n1, D_SSM), lambda b, j: (b, j, 0, 0))],
        out_shape=[jax.ShapeDtypeStruct((bsz, n1, 2, n2, D_FOURIER), BF16),
                   jax.ShapeDtypeStruct((bsz, n2 // N2_BLK, N2_BLK * n1, D_SSM), F32)],
        scratch_shapes=[pltpu.VMEM((D_SSM // LANES, REGROUP_PITCH * n1, LANES), F32)],
        compiler_params=pltpu.CompilerParams(
            dimension_semantics=("parallel", "parallel"), vmem_limit_bytes=VMEM_LIMIT),
        name="in_proj",
    )(x4, g, w_bf16, wk)


REGROUP_UNROLL = 8


def _fft2_kernel(m_ref, y_ref, g_ref, z_ref, *, kb, n2, pitch):
    nslab = D_FOURIER // LANES
    for j in range(kb):
        z = _dot(m_ref[j], y_ref[j])
        for ri in range(2):
            for s in range(nslab):
                z_ref[ri, s, j * pitch:j * pitch + n2, :] = (
                    z[ri * n2:(ri + 1) * n2, s * LANES:(s + 1) * LANES])

    def regroup(i, _):
        for u in range(REGROUP_UNROLL):
            k2 = i * REGROUP_UNROLL + u
            for ri in range(2):
                for s in range(nslab):
                    rows = [z_ref[ri, s, pl.ds(k2 + a * SUBLANES * pitch, SUBLANES, stride=pitch), :]
                            for a in range(kb // SUBLANES)]
                    g_ref[ri, k2, :, s * LANES:(s + 1) * LANES] = (
                        jnp.concatenate(rows, axis=0).astype(BF16))
        return 0

    jax.lax.fori_loop(0, n2 // REGROUP_UNROLL, regroup, 0)


def _fft_stage2(y, m2, kb):
    b, n1, rows, c = y.shape
    n2 = rows // 2
    pitch = n2 + SUBLANES
    return pl.pallas_call(
        functools.partial(_fft2_kernel, kb=kb, n2=n2, pitch=pitch),
        grid=(n1 // kb, b),
        in_specs=[pl.BlockSpec((kb, rows, rows), lambda j, i: (j, 0, 0)),
                  pl.BlockSpec((None, kb, rows, c), lambda j, i: (i, j, 0, 0))],
        out_specs=pl.BlockSpec((2, None, n2, kb, c), lambda j, i: (0, i, 0, j, 0)),
        out_shape=jax.ShapeDtypeStruct((2, b, n2, n1, c), BF16),
        scratch_shapes=[pltpu.VMEM((2, c // LANES, kb * pitch, LANES), F32)],
        compiler_params=pltpu.CompilerParams(
            dimension_semantics=("parallel", "parallel"), vmem_limit_bytes=VMEM_LIMIT),
        name="fft_stage2",
    )(m2, y)


def _fft_tables(seq):
    n1, n2 = FFT_N1, seq // FFT_N1
    k1 = np.arange(n1)
    ang1 = 2.0 * np.pi * ((k1[:, None] * k1[None, :]) % n1) / n1
    w1 = np.stack([np.cos(ang1), -np.sin(ang1)], axis=1).reshape(2 * n1, n1)
    wk = np.einsum("kn,ab->kanb", w1, np.eye(KRON_BLK)).reshape(2 * n1 * KRON_BLK, n1 * KRON_BLK)
    k2 = np.arange(n2)
    freq = k1[:, None, None] + n1 * k2[None, :, None]
    ang2 = 2.0 * np.pi * ((freq * k2[None, None, :]) % seq) / seq
    mr, mi = np.cos(ang2), -np.sin(ang2)
    m2 = np.concatenate([np.concatenate([mr, -mi], axis=2),
                         np.concatenate([mi, mr], axis=2)], axis=1)
    return jnp.asarray(wk, F32).astype(BF16), jnp.asarray(m2, F32).astype(BF16)


GROUPS_PER_BLOCK = LANES // SSM_GROUP
PAIRS_PER_BLOCK = GROUPS_PER_BLOCK // 2
RELAYOUT_CHUNKS = 8 * SUBLANES
SCAN_STEPS = (1, 2, 4)
SCAN_SLOTS = 2 * (len(SCAN_STEPS) + 1)


def _rotate_rows(v):
    return jnp.concatenate(
        [pltpu.roll(v[:, c:c + LANES], 0, axis=1, stride=SSM_GROUP, stride_axis=0)
         for c in range(0, v.shape[1], LANES)], axis=1)


def _ssm_kernel(z_ref, mi_ref, bs_ref, cs_ref, pw_ref, tab_ref, y_ref,
                u2_ref, s_ref, car_ref, yg_ref, *, nj, nrow):
    gpb, npair = GROUPS_PER_BLOCK, PAIRS_PER_BLOCK
    t16 = RELAYOUT_CHUNKS
    parts = nrow // t16
    lane_blk = jax.lax.broadcasted_iota(jnp.int32, (t16, LANES), 1) // SSM_GROUP
    masks = [lane_blk == m for m in range(gpb)]

    def relayout_in(i, _):
        j, r0 = i // parts, pl.multiple_of((i % parts) * t16, t16)
        ut = [z_ref[j, pl.ds(t * nrow + r0, t16), :] for t in range(CHUNK)]
        c0 = pl.multiple_of(j * nrow + r0, t16)
        for g in range(gpb):
            for hf in range(CHUNK // gpb):
                acc = ut[gpb * hf]
                for tt in range(1, gpb):
                    acc = jnp.where(masks[(g + tt) % gpb], ut[gpb * hf + tt], acc)
                if g:
                    acc = pltpu.roll(acc, LANES - SSM_GROUP * g, axis=1)
                u2_ref[g, pl.ds(c0, t16), hf * LANES:(hf + 1) * LANES] = acc.astype(BF16)
        return 0

    jax.lax.fori_loop(0, nj * parts, relayout_in, 0)

    for q in range(npair):
        up = jnp.concatenate([u2_ref[2 * q], u2_ref[2 * q + 1]], axis=1)
        s_ref[q] = _dot(up, bs_ref[q])

    row = jax.lax.broadcasted_iota(jnp.int32, (SUBLANES, LANES), 0)
    nvreg = nrow // SUBLANES

    def scan_block(tab, sr, si, cr, ci, down):
        keep = (lambda sh: row >= sh) if down else (lambda sh: row <= SUBLANES - 1 - sh)
        amt = (lambda sh: sh) if down else (lambda sh: SUBLANES - sh)
        for k, sh in enumerate(SCAN_STEPS):
            tr = pltpu.roll(sr, amt(sh), axis=0)
            ti = pltpu.roll(si, amt(sh), axis=0)
            ar, ai = tab[2 * k], tab[2 * k + 1]
            sr, si = sr + (ar * tr - ai * ti), si + (ar * ti + ai * tr)
        er, ei = tab[SCAN_SLOTS - 2], tab[SCAN_SLOTS - 1]
        fr = sr + (er * cr - ei * ci)
        fi = si + (er * ci + ei * cr)
        outr = jnp.where(keep(1), pltpu.roll(fr, amt(1), axis=0), cr)
        outi = jnp.where(keep(1), pltpu.roll(fi, amt(1), axis=0), ci)
        edge = SUBLANES - 1 if down else 0
        ncr = jnp.broadcast_to(fr[edge:edge + 1, :], (SUBLANES, LANES))
        nci = jnp.broadcast_to(fi[edge:edge + 1, :], (SUBLANES, LANES))
        return outr, outi, ncr, nci

    def tile_rows(x):
        return jnp.concatenate([x] * nvreg, axis=0)

    zero = jnp.zeros((SUBLANES, LANES), F32)
    for q in range(npair):
        for d in range(2):
            down = d == 0
            lo_r, lo_i = 2 * d * LANES, (2 * d + 1) * LANES
            slab = lambda j: slice(j * nrow, (j + 1) * nrow)
            order = list(range(nj)) if down else list(range(nj - 1, -1, -1))
            a_r, a_i = tile_rows(pw_ref[q, d, 0]), tile_rows(pw_ref[q, d, 1])
            lr = li = None
            for idx, j in enumerate(order):
                sr = s_ref[q, slab(j), lo_r:lo_r + LANES]
                si = s_ref[q, slab(j), lo_i:lo_i + LANES]
                if idx == 0:
                    lr, li = sr, si
                else:
                    lr, li = sr + (a_r * lr - a_i * li), si + (a_r * li + a_i * lr)
                if idx + 1 < nj:
                    car_ref[q, slab(order[idx + 1]), lo_r:lo_r + LANES] = lr
                    car_ref[q, slab(order[idx + 1]), lo_i:lo_i + LANES] = li
            tab = [tab_ref[q, d, k] for k in range(SCAN_SLOTS)]
            cr, ci = zero, zero
            gr_rows, gi_rows = [None] * nvreg, [None] * nvreg
            for v in (range(nvreg) if down else range(nvreg - 1, -1, -1)):
                rows8 = slice(v * SUBLANES, (v + 1) * SUBLANES)
                gr_rows[v], gi_rows[v], cr, ci = scan_block(tab, lr[rows8], li[rows8], cr, ci, down)
            gr, gi = jnp.concatenate(gr_rows, axis=0), jnp.concatenate(gi_rows, axis=0)
            car_ref[q, slab(order[0]), lo_r:lo_r + LANES] = gr
            car_ref[q, slab(order[0]), lo_i:lo_i + LANES] = gi
            for idx, j in enumerate(order[1:], start=1):
                p_r = tile_rows(pw_ref[q, d, 2 * (idx - 1)])
                p_i = tile_rows(pw_ref[q, d, 2 * (idx - 1) + 1])
                car_ref[q, slab(j), lo_r:lo_r + LANES] += p_r * gr - p_i * gi
                car_ref[q, slab(j), lo_i:lo_i + LANES] += p_r * gi + p_i * gr

    kk = CHUNK * SSM_GROUP
    for q in range(npair):
        yi = _dot(car_ref[q].astype(BF16), cs_ref[q])
        for e in range(2):
            g = 2 * q + e
            yg_ref[g] = _dot(u2_ref[g], mi_ref[g]) + yi[:, e * kk:(e + 1) * kk]

    def relayout_out(i, _):
        j, r0 = i // parts, pl.multiple_of((i % parts) * t16, t16)
        c0 = pl.multiple_of(j * nrow + r0, t16)
        yrot = []
        for g in range(gpb):
            halves = []
            for hf in range(CHUNK // gpb):
                v = yg_ref[g, pl.ds(c0, t16), hf * LANES:(hf + 1) * LANES]
                halves.append(pltpu.roll(v, SSM_GROUP * g, axis=1) if g else v)
            yrot.append(halves)
        for t in range(CHUNK):
            hf, tt = divmod(t, gpb)
            acc = yrot[0][hf]
            for g in range(1, gpb):
                acc = jnp.where(masks[(g + tt) % gpb], yrot[g][hf], acc)
            if tt:
                acc = pltpu.roll(acc, LANES - SSM_GROUP * tt, axis=1)
            y_ref[j, pl.ds(r0 * CHUNK + t, t16, stride=CHUNK), :] = acc
        return 0

    jax.lax.fori_loop(0, nj * parts, relayout_out, 0)


def _ssm_chunked(zs, m_intra, b_pair, c_pair, pw, tab, layer):
    bsz, nj, rows, _ = zs.shape
    nrow = rows // CHUNK
    nc = nj * nrow
    gpb, npair = GROUPS_PER_BLOCK, PAIRS_PER_BLOCK
    k = CHUNK * SSM_GROUP
    return pl.pallas_call(
        functools.partial(_ssm_kernel, nj=nj, nrow=nrow),
        grid=(D_SSM // LANES, bsz),
        in_specs=[pl.BlockSpec((None, nj, rows, LANES), lambda j, b: (b, 0, 0, j)),
                  pl.BlockSpec((None, gpb, k, k), lambda j, b: (layer, j, 0, 0)),
                  pl.BlockSpec((None, npair, 2 * k, 2 * k), lambda j, b: (layer, j, 0, 0)),
                  pl.BlockSpec((None, npair, 2 * k, 2 * k), lambda j, b: (layer, j, 0, 0)),
                  pl.BlockSpec((None, npair, 2, pw.shape[3], SUBLANES, LANES),
                               lambda j, b: (layer, j, 0, 0, 0, 0)),
                  pl.BlockSpec((None, npair, 2, SCAN_SLOTS, SUBLANES, LANES),
                               lambda j, b: (layer, j, 0, 0, 0, 0))],
        out_specs=pl.BlockSpec((None, nj, rows, LANES), lambda j, b: (b, 0, 0, j)),
        out_shape=jax.ShapeDtypeStruct((bsz, nj, rows, D_SSM), F32),
        scratch_shapes=[pltpu.VMEM((gpb, nc, k), BF16),
                        pltpu.VMEM((npair, nc, 2 * k), F32),
                        pltpu.VMEM((npair, nc, 2 * k), F32),
                        pltpu.VMEM((gpb, nc, k), F32)],
        compiler_params=pltpu.CompilerParams(
            dimension_semantics=("parallel", "parallel"), vmem_limit_bytes=VMEM_LIMIT),
        name="ssm_chunk",
    )(zs, m_intra, b_pair, c_pair, pw, tab)


def _ssm_tables(lam_re, lam_im, log_dt, b_re, b_im, c_re, c_im, d_skip, *, nj):
    t, h, p, g = CHUNK, SSM_GROUP, SSM_STATE, SSM_GROUPS
    nq = g // 2
    dt = jnp.exp(log_dt.astype(F32))[..., None]
    are = lam_re.astype(F32) * dt
    aim = lam_im.astype(F32) * dt

    def power(expo, a_re, a_im):
        mag = jnp.exp(a_re * expo)
        return mag * jnp.cos(a_im * expo), mag * jnp.sin(a_im * expo)

    lbr, lbi = power(1.0, are, aim)
    den = lam_re * lam_re + lam_im * lam_im
    cr = ((lbr - 1.0) * lam_re + lbi * lam_im) / den
    ci = (lbi * lam_re - (lbr - 1.0) * lam_im) / den
    bbr = cr[..., None] * b_re - ci[..., None] * b_im
    bbi = cr[..., None] * b_im + ci[..., None] * b_re

    def pair_rows(x):
        return x.reshape(2, 2, nq, 2 * p, x.shape[-1]).transpose(2, 0, 1, 3, 4)

    lane_pad = lambda x: jnp.pad(x, [(0, 0)] * (x.ndim - 1) + [(0, LANES - x.shape[-1])])
    steps = jnp.arange(t + 1, dtype=F32)
    pw = jnp.stack(power(steps, are[..., None], aim[..., None]))
    pl_tab = lane_pad(pair_rows(pw))
    pt_tab = jnp.pad(jnp.swapaxes(pair_rows(pw), -1, -2),
                     [(0, 0)] * 3 + [(0, LANES - t - 1), (0, 0)])
    ct_tab = lane_pad(pair_rows(jnp.stack([jnp.swapaxes(c_re.astype(F32), -1, -2),
                                           jnp.swapaxes(c_im.astype(F32), -1, -2)])))
    bt = jnp.stack([jnp.swapaxes(bbr, -1, -2), jnp.swapaxes(bbi, -1, -2)])
    bt = bt.reshape(2, 2, nq, 2, h, p)
    zero = jnp.zeros_like(bt[:, :, :, 0])
    bt_tab = jnp.stack([jnp.concatenate([bt[:, :, :, 0], zero], -1),
                        jnp.concatenate([zero, bt[:, :, :, 1]], -1)], axis=3)
    bt_tab = bt_tab.transpose(2, 0, 1, 3, 4, 5)
    dsk = jnp.pad(d_skip.astype(F32), ((0, 0), (0, t * h - h))).reshape(nq, 2, 1, t * h)

    pair_lanes = lambda x: x.reshape(2, nq, 1, 2 * p)
    ksteps = jnp.arange(1, max(nj, 2), dtype=F32)[:, None, None, None, None] * t
    pwr, pwi = power(ksteps, pair_lanes(are)[None], pair_lanes(aim)[None])
    pw = jnp.stack([pwr, pwi], axis=1).reshape(-1, 2, nq, 1, 2 * p)
    pw = jnp.broadcast_to(pw, pw.shape[:3] + (SUBLANES, 2 * p)).transpose(2, 1, 0, 3, 4)

    r8 = np.arange(SUBLANES)
    expo = np.stack([np.stack([np.full(SUBLANES, k), np.full(SUBLANES, k)]) for k in SCAN_STEPS]
                    + [np.stack([r8 + 1, SUBLANES - r8])])
    expo = jnp.asarray(t * nj * expo, F32)[:, :, None, :, None]
    ar, ai = power(expo, pair_lanes(are)[None], pair_lanes(aim)[None])
    live = np.stack([np.stack([r8 >= k, r8 <= SUBLANES - 1 - k]) for k in SCAN_STEPS]
                    + [np.ones((2, SUBLANES), bool)])
    live = jnp.asarray(live, F32)[:, :, None, :, None]
    ar, ai = ar * live, ai * live
    tab = jnp.stack([ar, ai], axis=1).reshape(SCAN_SLOTS, 2, nq, SUBLANES, 2 * p)
    tab = tab.transpose(2, 1, 0, 3, 4)
    return pl_tab, pt_tab, ct_tab, bt_tab, dsk, pw, tab


def _select_lanes(x, sel):
    hi = x.astype(BF16)
    lo = (x - hi.astype(F32)).astype(BF16)
    return _dot(hi, sel) + _dot(lo, sel)


def _select_rows(sel, x):
    hi = x.astype(BF16)
    lo = (x - hi.astype(F32)).astype(BF16)
    return _dot(sel, hi) + _dot(sel, lo)


def _roll_two_vregs(x, r):
    x0, x1 = x[:, :LANES], x[:, LANES:]
    if r >= LANES:
        x0, x1, r = x1, x0, r - LANES
    if r == 0:
        return jnp.concatenate([x0, x1], axis=1)
    lane = jax.lax.broadcasted_iota(jnp.int32, x0.shape, 1)
    a0, a1 = pltpu.roll(x0, r, axis=1), pltpu.roll(x1, r, axis=1)
    return jnp.concatenate([jnp.where(lane < r, a1, a0), jnp.where(lane < r, a0, a1)], axis=1)


def _ssm_prep_kernel(pl_ref, pt_ref, ct_ref, bt_ref, dsk_ref, sl_ref, sr_ref, e_ref,
                     m_ref, b_ref, c_ref):
    t, h = CHUNK, SSM_GROUP
    kk = t * h
    lane = jax.lax.broadcasted_iota(jnp.int32, (h, kk), 1)
    rowi = jax.lax.broadcasted_iota(jnp.int32, (h, kk), 0)
    half = jax.lax.broadcasted_iota(jnp.int32, (LANES, kk), 0) // SSM_STATE

    cl, cq = [], []
    for d in range(2):
        ctr, cti = _select_lanes(ct_ref[0, d], e_ref[...]), _select_lanes(ct_ref[1, d], e_ref[...])
        for out, pat in ((cl, d), (cq, 2 + d)):
            pr, pi = _select_lanes(pl_ref[0, d], sl_ref[pat]), _select_lanes(pl_ref[1, d], sl_ref[pat])
            out.append((ctr * pr - cti * pi, ctr * pi + cti * pr))

    for d in range(2):
        for ri, val in enumerate((cq[d][0], -cq[d][1])):
            r0 = (2 * d + ri) * LANES
            for gi in range(2):
                c_ref[r0:r0 + LANES, gi * kk:(gi + 1) * kk] = (
                    jnp.where(half == gi, val, 0.0).astype(BF16))

    for d in range(2):
        pwr, pwi = _select_rows(sr_ref[d], pt_ref[0, d]), _select_rows(sr_ref[d], pt_ref[1, d])
        for gi in range(2):
            btr = jnp.concatenate([bt_ref[0, d, gi]] * t, axis=0)
            bti = jnp.concatenate([bt_ref[1, d, gi]] * t, axis=0)
            r0, c0 = gi * kk, 2 * d * LANES
            b_ref[r0:r0 + kk, c0:c0 + LANES] = (btr * pwr - bti * pwi).astype(BF16)
            b_ref[r0:r0 + kk, c0 + LANES:c0 + 2 * LANES] = (btr * pwi + bti * pwr).astype(BF16)

    for gi in range(2):
        gen = [_dot(bt_ref[0, d, gi].astype(BF16), cl[d][0].astype(BF16))
               - _dot(bt_ref[1, d, gi].astype(BF16), cl[d][1].astype(BF16)) for d in range(2)]
        diag = jnp.where(lane < h, gen[1], 0.0) + jnp.where(rowi == lane, dsk_ref[gi], 0.0)
        kf, kb = gen[0] + diag, gen[1]
        for s in range(t):
            blk = jnp.where(lane >= h * s, _roll_two_vregs(kf, h * s), _roll_two_vregs(kb, h * s))
            m_ref[gi, s * h:(s + 1) * h, :] = blk.astype(BF16)


def _ssm_prep_constants():
    t, h = CHUNK, SSM_GROUP
    kk = t * h
    step = np.arange(kk) // h
    k = np.arange(LANES)
    pats = [step, (t - step) % t, step + 1, t - step]
    sel_lane = np.stack([(k[:, None] == pat[None, :]) for pat in pats]).astype(np.float32)
    sel_row = np.stack([(pat[:, None] == k[None, :]) for pat in (t - 1 - step, step)])
    tile_ho = (k[:, None] == (np.arange(kk) % h)[None, :])
    as_bf16 = lambda a: jnp.asarray(a, F32).astype(BF16)
    return as_bf16(sel_lane), as_bf16(sel_row), as_bf16(tile_ho)


def _ssm_prep(pl_tab, pt_tab, ct_tab, bt_tab, dsk):
    depth, nq = pl_tab.shape[:2]
    kk = CHUNK * SSM_GROUP
    sel_lane, sel_row, tile_ho = _ssm_prep_constants()
    tab5 = pl.BlockSpec((None, None, 2, 2, LANES, LANES), lambda l, q: (l, q, 0, 0, 0, 0))
    return pl.pallas_call(
        _ssm_prep_kernel,
        grid=(depth, nq),
        in_specs=[tab5, tab5, tab5,
                  pl.BlockSpec((None, None, 2, 2, 2, SSM_GROUP, LANES),
                               lambda l, q: (l, q, 0, 0, 0, 0, 0)),
                  pl.BlockSpec((None, None, 2, 1, kk), lambda l, q: (l, q, 0, 0, 0)),
                  _const_spec(sel_lane.shape), _const_spec(sel_row.shape),
                  _const_spec(tile_ho.shape)],
        out_specs=[pl.BlockSpec((None, 2, kk, kk), lambda l, q: (l, q, 0, 0)),
                   pl.BlockSpec((None, None, 2 * kk, 2 * kk), lambda l, q: (l, q, 0, 0)),
                   pl.BlockSpec((None, None, 2 * kk, 2 * kk), lambda l, q: (l, q, 0, 0))],
        out_shape=[jax.ShapeDtypeStruct((depth, 2 * nq, kk, kk), BF16),
                   jax.ShapeDtypeStruct((depth, nq, 2 * kk, 2 * kk), BF16),
                   jax.ShapeDtypeStruct((depth, nq, 2 * kk, 2 * kk), BF16)],
        compiler_params=pltpu.CompilerParams(
            dimension_semantics=("parallel", "parallel"), vmem_limit_bytes=VMEM_LIMIT),
        name="ssm_prep",
    )(pl_tab, pt_tab, ct_tab, bt_tab, dsk, sel_lane, sel_row, tile_ho)


MIX_SUBBLOCKS = 2
FF_BLOCK = 1024
FFT2_K1_BLOCK = 2 * SUBLANES


def _mix_mlp_kernel(x_ref, gr_ref, gi_ref, ys_ref, bdr_ref, bdi_ref, fg_ref, wg_ref, bg_ref,
                    sg_ref, wo_ref, pmg_ref, plg_ref, w1_ref, w2_ref, pog_ref, o_ref, *, ff_blk,
                    nsub):
    n1s = x_ref.shape[0] // nsub
    rows = n1s * CHUNK

    def mix(k):
        r = slice(k * rows, (k + 1) * rows)
        s1 = slice(k * n1s, (k + 1) * n1s)
        gr = gr_ref[s1].reshape(rows, D_FOURIER)
        gi = gi_ref[s1].reshape(rows, D_FOURIER)
        yf = jnp.concatenate(
            [_dot(gr[:, c:c + MXU_DIM], bdr_ref[c:c + MXU_DIM, c:c + MXU_DIM])
             + _dot(gi[:, c:c + MXU_DIM], bdi_ref[c:c + MXU_DIM, c:c + MXU_DIM])
             for c in range(0, D_FOURIER, MXU_DIM)], axis=1)
        yf = _rms(yf, fg_ref[...])
        y = ys_ref[r, :]
        gl = jax.nn.gelu(y, approximate=True)
        gate = _dot(gl.astype(BF16), wg_ref[...]) + bg_ref[...]
        ys = _rms(gl * jax.nn.sigmoid(gate), sg_ref[...])
        cat = jnp.concatenate([yf.astype(BF16), ys.astype(BF16)], axis=-1)
        x1 = x_ref[s1].reshape(rows, D_MODEL) + _rms(_dot(cat, wo_ref[...]), pmg_ref[...])
        return x1, _rms(x1, plg_ref[...]).astype(BF16)

    def mlp(k, x1, h):
        acc = jnp.zeros(x1.shape, F32)
        for j in range(D_FF // ff_blk):
            a = _dot(h, w1_ref[:, j * ff_blk:(j + 1) * ff_blk])
            a = jnp.square(jnp.maximum(a, 0.0)).astype(BF16)
            acc = acc + _dot(a, w2_ref[j * ff_blk:(j + 1) * ff_blk, :])
        o_ref[k * n1s:(k + 1) * n1s] = (x1 + _rms(acc, pog_ref[...])).reshape(n1s, CHUNK, D_MODEL)

    nxt = mix(0)
    for k in range(nsub):
        cur = nxt
        if k + 1 < nsub:
            nxt = mix(k + 1)
        mlp(k, *cur)


def _mix_mlp(x5, gri, ys, bdr, bdi, fg, wg, bg, sg, wo, pmg, plg, w1, w2, pog, ff_blk, layer):
    bsz, n1, nj, t, _ = x5.shape
    lspec = functools.partial(_layer_spec, layer=layer)
    slab = lambda c: pl.BlockSpec((None, n1, None, t, c), lambda b, j: (b, 0, j, 0, 0))
    return pl.pallas_call(
        functools.partial(_mix_mlp_kernel, ff_blk=ff_blk, nsub=MIX_SUBBLOCKS),
        grid=(bsz, nj),
        in_specs=[slab(D_MODEL),
                  pl.BlockSpec((None, None, n1, None, t, D_FOURIER),
                               lambda b, j: (0, b, 0, j, 0, 0)),
                  pl.BlockSpec((None, None, n1, None, t, D_FOURIER),
                               lambda b, j: (1, b, 0, j, 0, 0)),
                  pl.BlockSpec((None, None, n1 * t, D_SSM), lambda b, j: (b, j, 0, 0)),
                  lspec((D_FOURIER, D_FOURIER)), lspec((D_FOURIER, D_FOURIER)),
                  lspec((1, D_FOURIER)),
                  lspec((D_SSM, D_SSM)), lspec((1, D_SSM)), lspec((1, D_SSM)),
                  lspec((D_MODEL, D_MODEL)), lspec((1, D_MODEL)),
                  lspec((1, D_MODEL)),
                  lspec((D_MODEL, D_FF)), lspec((D_FF, D_MODEL)),
                  lspec((1, D_MODEL))],
        out_specs=slab(D_MODEL),
        out_shape=jax.ShapeDtypeStruct(x5.shape, F32),
        compiler_params=pltpu.CompilerParams(
            dimension_semantics=("parallel", "parallel"), vmem_limit_bytes=VMEM_LIMIT),
        name="mix_mlp",
    )(x5, gri, gri, ys, bdr, bdi, fg, wg, bg, sg, wo, pmg, plg, w1, w2, pog)


def _fourier_channel_maps(w_f, seq):
    c = np.arange(FOURIER_HEAD_DIM)
    ang = 2.0 * np.pi * ((c[:, None] * c[None, :]) % FOURIER_HEAD_DIM) / FOURIER_HEAD_DIM
    scale = 1.0 / math.sqrt(seq * FOURIER_HEAD_DIM)
    hp = jax.lax.Precision.HIGHEST
    cw = jnp.einsum("cd,hde->hce", jnp.asarray(np.cos(ang) * scale, F32), w_f.astype(F32), precision=hp)
    sw = jnp.einsum("cd,hde->hce", jnp.asarray(np.sin(ang) * scale, F32), w_f.astype(F32), precision=hp)
    eye = jnp.eye(FOURIER_HEADS, dtype=F32)
    bd = lambda m: jnp.einsum("hce,hk->hcke", m, eye).reshape(D_FOURIER, D_FOURIER)
    return bd(cw).astype(BF16), bd(sw).astype(BF16)


def kernel(x, w_in, w_out, pre_mix_g, post_mix_g, pre_mlp_g, post_mlp_g, fourier_out_g, ssm_out_g,
           w_fourier, lam_re, lam_im, log_dt, b_re, b_im, c_re, c_im, d_skip, w_glu, b_glu,
           w_ff1, w_ff2):
    bsz, seq, _ = x.shape
    depth = w_in.shape[0]
    n = bsz * seq
    n1, n2 = FFT_N1, seq // FFT_N1
    nj = n2 // CHUNK
    assert seq % (FFT_N1 * N2_BLK) == 0 and N2_BLK == CHUNK and n1 % RELAYOUT_CHUNKS == 0
    wk_tab, m2_tab = _fft_tables(seq)

    rows = lambda v: v.reshape(depth, 1, -1).astype(F32)
    w_in_b, w_out_b, w_glu_b = w_in.astype(BF16), w_out.astype(BF16), w_glu.astype(BF16)
    w_ff1_b, w_ff2_b = w_ff1.astype(BF16), w_ff2.astype(BF16)
    pl_tab, pt_tab, ct_tab, bt_tab, dsk, pw, tab = jax.vmap(
        functools.partial(_ssm_tables, nj=nj))(
            lam_re, lam_im, log_dt, b_re, b_im, c_re, c_im, d_skip)
    m_intra, b_pair, c_pair = _ssm_prep(pl_tab, pt_tab, ct_tab, bt_tab, dsk)
    bdr, bdi = jax.vmap(functools.partial(_fourier_channel_maps, seq=seq))(w_fourier)
    gains = [rows(v) for v in (pre_mix_g, fourier_out_g, b_glu, ssm_out_g, post_mix_g,
                               pre_mlp_g, post_mlp_g)]
    pre_mix, four_g, glu_b, ssm_g, post_mix, pre_mlp, post_mlp = gains

    xs = x.reshape(bsz, n1, n2, D_MODEL).astype(F32)
    for i in range(depth):
        y, zs = _in_proj(xs, pre_mix, w_in_b, wk_tab, i)
        g = _fft_stage2(y.reshape(bsz, n1, 2 * n2, D_FOURIER), m2_tab, kb=FFT2_K1_BLOCK)
        gri = g.reshape(2, bsz, n1, nj, CHUNK, D_FOURIER)
        ysr = _ssm_chunked(zs, m_intra, b_pair, c_pair, pw, tab, i)
        xs = _mix_mlp(xs.reshape(bsz, n1, nj, CHUNK, D_MODEL), gri, ysr, bdr, bdi, four_g,
                      w_glu_b, glu_b, ssm_g, w_out_b, post_mix, pre_mlp, w_ff1_b, w_ff2_b,
                      post_mlp, ff_blk=FF_BLOCK, layer=i).reshape(bsz, n1, n2, D_MODEL)
    return xs.reshape(bsz, seq, D_MODEL).astype(x.dtype)
```

```python
import functools
import math

import jax
import jax.numpy as jnp
import numpy as np
from jax.experimental import pallas as pl
from jax.experimental.pallas import tpu as pltpu

EPS = 1e-6
D_MODEL = 1024
D_FOURIER = 512
D_SSM = 512
FOURIER_HEADS = 8
FOURIER_HEAD_DIM = 64
SSM_GROUP = 16
SSM_GROUPS = 32
SSM_STATE = 64
D_FF = 4096

FFT_N1 = 64
CHUNK = 16
LANES = 128
SUBLANES = 8
MXU_DIM = 256
VMEM_LIMIT = 56 * 1024 * 1024

BF16 = jnp.bfloat16
F32 = jnp.float32


def _rms(x, g):
    return x * jax.lax.rsqrt(jnp.mean(x * x, axis=-1, keepdims=True) + EPS) * g


def _dot(a, b):
    return jnp.dot(a, b, preferred_element_type=F32)


def _const_spec(shape):
    nd = len(shape)
    return pl.BlockSpec(shape, lambda *_: (0,) * nd, pipeline_mode=pl.Buffered(1))


def _layer_spec(shape, layer):
    nd = len(shape)
    return pl.BlockSpec((None,) + tuple(shape), lambda *_: (layer,) + (0,) * nd,
                        pipeline_mode=pl.Buffered(1))


KRON_BLK = SUBLANES
N2_BLK = 2 * KRON_BLK
REGROUP_PITCH = N2_BLK + 4


def _in_proj_kernel(x_ref, g_ref, w_ref, wk_ref, y_ref, zs_ref, zt_ref):
    n1 = x_ref.shape[0]
    x = x_ref[...].reshape(n1 * N2_BLK, D_MODEL)
    h = _rms(x, g_ref[...]).astype(BF16)
    z = _dot(h, w_ref[...])
    zf = z[:, :D_FOURIER].reshape(n1, N2_BLK // KRON_BLK, KRON_BLK, D_FOURIER)
    parts = []
    for a in range(N2_BLK // KRON_BLK):
        za = zf[:, a].reshape(n1 * KRON_BLK, D_FOURIER).astype(BF16)
        ya = _dot(wk_ref[...], za)
        parts.append(ya.reshape(2 * n1, KRON_BLK, D_FOURIER))
    y = jnp.concatenate(parts, axis=1)
    y_ref[...] = y.astype(BF16).reshape(n1, 2, N2_BLK, D_FOURIER)
    zrot = _rotate_rows(z[:, D_FOURIER:])
    for s in range(D_SSM // LANES):
        for a in range(n1):
            zt_ref[s, a * REGROUP_PITCH:a * REGROUP_PITCH + CHUNK, :] = (
                zrot[a * CHUNK:(a + 1) * CHUNK, s * LANES:(s + 1) * LANES])
    for t in range(CHUNK):
        for s in range(D_SSM // LANES):
            zs_ref[t * n1:(t + 1) * n1, s * LANES:(s + 1) * LANES] = (
                zt_ref[s, pl.ds(t, n1, stride=REGROUP_PITCH), :])


def _in_proj(x4, g, w_bf16, wk, layer):
    bsz, n1, n2, _ = x4.shape
    rows = n1 * KRON_BLK
    return pl.pallas_call(
        _in_proj_kernel,
        grid=(bsz, n2 // N2_BLK),
        in_specs=[pl.BlockSpec((None, n1, N2_BLK, D_MODEL), lambda b, j: (b, 0, j, 0)),
                  _layer_spec((1, D_MODEL), layer),
                  _layer_spec((D_MODEL, D_MODEL), layer),
                  _const_spec((2 * rows, rows))],
        out_specs=[pl.BlockSpec((None, n1, 2, N2_BLK, D_FOURIER), lambda b, j: (b, 0, 0, j, 0)),
                   pl.BlockSpec((None, None, N2_BLK * n1, D_SSM), lambda b, j: (b, j, 0, 0))],
        out_shape=[jax.ShapeDtypeStruct((bsz, n1, 2, n2, D_FOURIER), BF16),
                   jax.ShapeDtypeStruct((bsz, n2 // N2_BLK, N2_BLK * n1, D_SSM), F32)],
        scratch_shapes=[pltpu.VMEM((D_SSM // LANES, REGROUP_PITCH * n1, LANES), F32)],
        compiler_params=pltpu.CompilerParams(
            dimension_semantics=("parallel", "parallel"), vmem_limit_bytes=VMEM_LIMIT),
        name="in_proj",
    )(x4, g, w_bf16, wk)


REGROUP_UNROLL = 8


def _fft2_kernel(m_ref, y_ref, g_ref, z_ref, *, kb, n2, pitch):
    nslab = D_FOURIER // LANES
    for j in range(kb):
        z = _dot(m_ref[j], y_ref[j])
        for ri in range(2):
            for s in range(nslab):
                z_ref[ri, s, j * pitch:j * pitch + n2, :] = (
                    z[ri * n2:(ri + 1) * n2, s * LANES:(s + 1) * LANES])

    def regroup(i, _):
        for u in range(REGROUP_UNROLL):
            k2 = i * REGROUP_UNROLL + u
            for ri in range(2):
                for s in range(nslab):
                    rows = [z_ref[ri, s, pl.ds(k2 + a * SUBLANES * pitch, SUBLANES, stride=pitch), :]
                            for a in range(kb // SUBLANES)]
                    g_ref[ri, k2, :, s * LANES:(s + 1) * LANES] = (
                        jnp.concatenate(rows, axis=0).astype(BF16))
        return 0

    jax.lax.fori_loop(0, n2 // REGROUP_UNROLL, regroup, 0)


def _fft_stage2(y, m2, kb):
    b, n1, rows, c = y.shape
    n2 = rows // 2
    pitch = n2 + SUBLANES
    return pl.pallas_call(
        functools.partial(_fft2_kernel, kb=kb, n2=n2, pitch=pitch),
        grid=(n1 // kb, b),
        in_specs=[pl.BlockSpec((kb, rows, rows), lambda j, i: (j, 0, 0)),
                  pl.BlockSpec((None, kb, rows, c), lambda j, i: (i, j, 0, 0))],
        out_specs=pl.BlockSpec((2, None, n2, kb, c), lambda j, i: (0, i, 0, j, 0)),
        out_shape=jax.ShapeDtypeStruct((2, b, n2, n1, c), BF16),
        scratch_shapes=[pltpu.VMEM((2, c // LANES, kb * pitch, LANES), F32)],
        compiler_params=pltpu.CompilerParams(
            dimension_semantics=("parallel", "parallel"), vmem_limit_bytes=VMEM_LIMIT),
        name="fft_stage2",
    )(m2, y)


def _fft_tables(seq):
    n1, n2 = FFT_N1, seq // FFT_N1
    k1 = np.arange(n1)
    ang1 = 2.0 * np.pi * ((k1[:, None] * k1[None, :]) % n1) / n1
    w1 = np.stack([np.cos(ang1), -np.sin(ang1)], axis=1).reshape(2 * n1, n1)
    wk = np.einsum("kn,ab->kanb", w1, np.eye(KRON_BLK)).reshape(2 * n1 * KRON_BLK, n1 * KRON_BLK)
    k2 = np.arange(n2)
    freq = k1[:, None, None] + n1 * k2[None, :, None]
    ang2 = 2.0 * np.pi * ((freq * k2[None, None, :]) % seq) / seq
    mr, mi = np.cos(ang2), -np.sin(ang2)
    m2 = np.concatenate([np.concatenate([mr, -mi], axis=2),
                         np.concatenate([mi, mr], axis=2)], axis=1)
    return jnp.asarray(wk, F32).astype(BF16), jnp.asarray(m2, F32).astype(BF16)


GROUPS_PER_BLOCK = LANES // SSM_GROUP
PAIRS_PER_BLOCK = GROUPS_PER_BLOCK // 2
RELAYOUT_CHUNKS = 8 * SUBLANES
SCAN_STEPS = (1, 2, 4)
SCAN_SLOTS = 2 * (len(SCAN_STEPS) + 1)


def _rotate_rows(v):
    return jnp.concatenate(
        [pltpu.roll(v[:, c:c + LANES], 0, axis=1, stride=SSM_GROUP, stride_axis=0)
         for c in range(0, v.shape[1], LANES)], axis=1)


def _unrotate_rows(v):
    row = jax.lax.broadcasted_iota(jnp.int32, v.shape, 0)
    cols = []
    for c in range(0, v.shape[1], LANES):
        w = v[:, c:c + LANES]
        for bit in (1, 2, 4):
            w = jnp.where((row[:, :LANES] & bit) != 0,
                          pltpu.roll(w, LANES - SSM_GROUP * bit, axis=1), w)
        cols.append(w)
    return jnp.concatenate(cols, axis=1)


def _ssm_kernel(z_ref, mi_ref, bs_ref, cs_ref, pw_ref, tab_ref, y_ref,
                u2_ref, s_ref, car_ref, yg_ref, *, nj, nrow):
    gpb, npair = GROUPS_PER_BLOCK, PAIRS_PER_BLOCK
    t16 = RELAYOUT_CHUNKS
    parts = nrow // t16
    lane_blk = jax.lax.broadcasted_iota(jnp.int32, (t16, LANES), 1) // SSM_GROUP
    masks = [lane_blk == m for m in range(gpb)]

    def relayout_in(i, _):
        j, r0 = i // parts, pl.multiple_of((i % parts) * t16, t16)
        ut = [z_ref[j, pl.ds(t * nrow + r0, t16), :] for t in range(CHUNK)]
        c0 = pl.multiple_of(j * nrow + r0, t16)
        for g in range(gpb):
            for hf in range(CHUNK // gpb):
                acc = ut[gpb * hf]
                for tt in range(1, gpb):
                    acc = jnp.where(masks[(g + tt) % gpb], ut[gpb * hf + tt], acc)
                if g:
                    acc = pltpu.roll(acc, LANES - SSM_GROUP * g, axis=1)
                u2_ref[g, pl.ds(c0, t16), hf * LANES:(hf + 1) * LANES] = acc.astype(BF16)
        return 0

    jax.lax.fori_loop(0, nj * parts, relayout_in, 0)

    for q in range(npair):
        up = jnp.concatenate([u2_ref[2 * q], u2_ref[2 * q + 1]], axis=1)
        s_ref[q] = _dot(up, bs_ref[q])

    row = jax.lax.broadcasted_iota(jnp.int32, (SUBLANES, LANES), 0)
    nvreg = nrow // SUBLANES

    def scan_block(tab, sr, si, cr, ci, down):
        keep = (lambda sh: row >= sh) if down else (lambda sh: row <= SUBLANES - 1 - sh)
        amt = (lambda sh: sh) if down else (lambda sh: SUBLANES - sh)
        for k, sh in enumerate(SCAN_STEPS):
            tr = pltpu.roll(sr, amt(sh), axis=0)
            ti = pltpu.roll(si, amt(sh), axis=0)
            ar, ai = tab[2 * k], tab[2 * k + 1]
            sr, si = sr + (ar * tr - ai * ti), si + (ar * ti + ai * tr)
        er, ei = tab[SCAN_SLOTS - 2], tab[SCAN_SLOTS - 1]
        fr = sr + (er * cr - ei * ci)
        fi = si + (er * ci + ei * cr)
        outr = jnp.where(keep(1), pltpu.roll(fr, amt(1), axis=0), cr)
        outi = jnp.where(keep(1), pltpu.roll(fi, amt(1), axis=0), ci)
        edge = SUBLANES - 1 if down else 0
        ncr = jnp.broadcast_to(fr[edge:edge + 1, :], (SUBLANES, LANES))
        nci = jnp.broadcast_to(fi[edge:edge + 1, :], (SUBLANES, LANES))
        return outr, outi, ncr, nci

    def tile_rows(x):
        return jnp.concatenate([x] * nvreg, axis=0)

    zero = jnp.zeros((SUBLANES, LANES), F32)
    for q in range(npair):
        for d in range(2):
            down = d == 0
            lo_r, lo_i = 2 * d * LANES, (2 * d + 1) * LANES
            slab = lambda j: slice(j * nrow, (j + 1) * nrow)
            order = list(range(nj)) if down else list(range(nj - 1, -1, -1))
            a_r, a_i = tile_rows(pw_ref[q, d, 0]), tile_rows(pw_ref[q, d, 1])
            lr = li = None
            for idx, j in enumerate(order):
                sr = s_ref[q, slab(j), lo_r:lo_r + LANES]
                si = s_ref[q, slab(j), lo_i:lo_i + LANES]
                if idx == 0:
                    lr, li = sr, si
                else:
                    lr, li = sr + (a_r * lr - a_i * li), si + (a_r * li + a_i * lr)
                if idx + 1 < nj:
                    car_ref[q, slab(order[idx + 1]), lo_r:lo_r + LANES] = lr
                    car_ref[q, slab(order[idx + 1]), lo_i:lo_i + LANES] = li
            tab = [tab_ref[q, d, k] for k in range(SCAN_SLOTS)]
            cr, ci = zero, zero
            gr_rows, gi_rows = [None] * nvreg, [None] * nvreg
            for v in (range(nvreg) if down else range(nvreg - 1, -1, -1)):
                rows8 = slice(v * SUBLANES, (v + 1) * SUBLANES)
                gr_rows[v], gi_rows[v], cr, ci = scan_block(tab, lr[rows8], li[rows8], cr, ci, down)
            gr, gi = jnp.concatenate(gr_rows, axis=0), jnp.concatenate(gi_rows, axis=0)
            car_ref[q, slab(order[0]), lo_r:lo_r + LANES] = gr
            car_ref[q, slab(order[0]), lo_i:lo_i + LANES] = gi
            for idx, j in enumerate(order[1:], start=1):
                p_r = tile_rows(pw_ref[q, d, 2 * (idx - 1)])
                p_i = tile_rows(pw_ref[q, d, 2 * (idx - 1) + 1])
                car_ref[q, slab(j), lo_r:lo_r + LANES] += p_r * gr - p_i * gi
                car_ref[q, slab(j), lo_i:lo_i + LANES] += p_r * gi + p_i * gr

    kk = CHUNK * SSM_GROUP
    for q in range(npair):
        yi = _dot(car_ref[q].astype(BF16), cs_ref[q])
        for e in range(2):
            g = 2 * q + e
            yg_ref[g] = _dot(u2_ref[g], mi_ref[g]) + yi[:, e * kk:(e + 1) * kk]

    def relayout_out(i, _):
        j, r0 = i // parts, pl.multiple_of((i % parts) * t16, t16)
        c0 = pl.multiple_of(j * nrow + r0, t16)
        yrot = []
        for g in range(gpb):
            halves = []
            for hf in range(CHUNK // gpb):
                v = yg_ref[g, pl.ds(c0, t16), hf * LANES:(hf + 1) * LANES]
                halves.append(pltpu.roll(v, SSM_GROUP * g, axis=1) if g else v)
            yrot.append(halves)
        for t in range(CHUNK):
            hf, tt = divmod(t, gpb)
            acc = yrot[0][hf]
            for g in range(1, gpb):
                acc = jnp.where(masks[(g + tt) % gpb], yrot[g][hf], acc)
            y_ref[j, pl.ds(r0 * CHUNK + t, t16, stride=CHUNK), :] = acc
        return 0

    jax.lax.fori_loop(0, nj * parts, relayout_out, 0)


def _ssm_chunked(zs, m_intra, b_pair, c_pair, pw, tab, layer):
    bsz, nj, rows, _ = zs.shape
    nrow = rows // CHUNK
    nc = nj * nrow
    gpb, npair = GROUPS_PER_BLOCK, PAIRS_PER_BLOCK
    k = CHUNK * SSM_GROUP
    return pl.pallas_call(
        functools.partial(_ssm_kernel, nj=nj, nrow=nrow),
        grid=(D_SSM // LANES, bsz),
        in_specs=[pl.BlockSpec((None, nj, rows, LANES), lambda j, b: (b, 0, 0, j)),
                  pl.BlockSpec((None, gpb, k, k), lambda j, b: (layer, j, 0, 0)),
                  pl.BlockSpec((None, npair, 2 * k, 2 * k), lambda j, b: (layer, j, 0, 0)),
                  pl.BlockSpec((None, npair, 2 * k, 2 * k), lambda j, b: (layer, j, 0, 0)),
                  pl.BlockSpec((None, npair, 2, pw.shape[3], SUBLANES, LANES),
                               lambda j, b: (layer, j, 0, 0, 0, 0)),
                  pl.BlockSpec((None, npair, 2, SCAN_SLOTS, SUBLANES, LANES),
                               lambda j, b: (layer, j, 0, 0, 0, 0))],
        out_specs=pl.BlockSpec((None, nj, rows, LANES), lambda j, b: (b, 0, 0, j)),
        out_shape=jax.ShapeDtypeStruct((bsz, nj, rows, D_SSM), F32),
        scratch_shapes=[pltpu.VMEM((gpb, nc, k), BF16),
                        pltpu.VMEM((npair, nc, 2 * k), F32),
                        pltpu.VMEM((npair, nc, 2 * k), F32),
                        pltpu.VMEM((gpb, nc, k), F32)],
        compiler_params=pltpu.CompilerParams(
            dimension_semantics=("parallel", "parallel"), vmem_limit_bytes=VMEM_LIMIT),
        name="ssm_chunk",
    )(zs, m_intra, b_pair, c_pair, pw, tab)


def _ssm_tables(lam_re, lam_im, log_dt, b_re, b_im, c_re, c_im, d_skip, *, nj):
    t, h, p, g = CHUNK, SSM_GROUP, SSM_STATE, SSM_GROUPS
    nq = g // 2
    dt = jnp.exp(log_dt.astype(F32))[..., None]
    are = lam_re.astype(F32) * dt
    aim = lam_im.astype(F32) * dt

    def power(expo, a_re, a_im):
        mag = jnp.exp(a_re * expo)
        return mag * jnp.cos(a_im * expo), mag * jnp.sin(a_im * expo)

    lbr, lbi = power(1.0, are, aim)
    den = lam_re * lam_re + lam_im * lam_im
    cr = ((lbr - 1.0) * lam_re + lbi * lam_im) / den
    ci = (lbi * lam_re - (lbr - 1.0) * lam_im) / den
    bbr = cr[..., None] * b_re - ci[..., None] * b_im
    bbi = cr[..., None] * b_im + ci[..., None] * b_re

    def pair_rows(x):
        return x.reshape(2, 2, nq, 2 * p, x.shape[-1]).transpose(2, 0, 1, 3, 4)

    lane_pad = lambda x: jnp.pad(x, [(0, 0)] * (x.ndim - 1) + [(0, LANES - x.shape[-1])])
    steps = jnp.arange(t + 1, dtype=F32)
    pw = jnp.stack(power(steps, are[..., None], aim[..., None]))
    pl_tab = lane_pad(pair_rows(pw))
    pt_tab = jnp.pad(jnp.swapaxes(pair_rows(pw), -1, -2),
                     [(0, 0)] * 3 + [(0, LANES - t - 1), (0, 0)])
    ct_tab = lane_pad(pair_rows(jnp.stack([jnp.swapaxes(c_re.astype(F32), -1, -2),
                                           jnp.swapaxes(c_im.astype(F32), -1, -2)])))
    bt = jnp.stack([jnp.swapaxes(bbr, -1, -2), jnp.swapaxes(bbi, -1, -2)])
    bt = bt.reshape(2, 2, nq, 2, h, p)
    zero = jnp.zeros_like(bt[:, :, :, 0])
    bt_tab = jnp.stack([jnp.concatenate([bt[:, :, :, 0], zero], -1),
                        jnp.concatenate([zero, bt[:, :, :, 1]], -1)], axis=3)
    bt_tab = bt_tab.transpose(2, 0, 1, 3, 4, 5)
    dsk = jnp.pad(d_skip.astype(F32), ((0, 0), (0, t * h - h))).reshape(nq, 2, 1, t * h)

    pair_lanes = lambda x: x.reshape(2, nq, 1, 2 * p)
    ksteps = jnp.arange(1, max(nj, 2), dtype=F32)[:, None, None, None, None] * t
    pwr, pwi = power(ksteps, pair_lanes(are)[None], pair_lanes(aim)[None])
    pw = jnp.stack([pwr, pwi], axis=1).reshape(-1, 2, nq, 1, 2 * p)
    pw = jnp.broadcast_to(pw, pw.shape[:3] + (SUBLANES, 2 * p)).transpose(2, 1, 0, 3, 4)

    r8 = np.arange(SUBLANES)
    expo = np.stack([np.stack([np.full(SUBLANES, k), np.full(SUBLANES, k)]) for k in SCAN_STEPS]
                    + [np.stack([r8 + 1, SUBLANES - r8])])
    expo = jnp.asarray(t * nj * expo, F32)[:, :, None, :, None]
    ar, ai = power(expo, pair_lanes(are)[None], pair_lanes(aim)[None])
    live = np.stack([np.stack([r8 >= k, r8 <= SUBLANES - 1 - k]) for k in SCAN_STEPS]
                    + [np.ones((2, SUBLANES), bool)])
    live = jnp.asarray(live, F32)[:, :, None, :, None]
    ar, ai = ar * live, ai * live
    tab = jnp.stack([ar, ai], axis=1).reshape(SCAN_SLOTS, 2, nq, SUBLANES, 2 * p)
    tab = tab.transpose(2, 1, 0, 3, 4)
    return pl_tab, pt_tab, ct_tab, bt_tab, dsk, pw, tab


def _select_lanes(x, sel):
    hi = x.astype(BF16)
    lo = (x - hi.astype(F32)).astype(BF16)
    return _dot(hi, sel) + _dot(lo, sel)


def _select_rows(sel, x):
    hi = x.astype(BF16)
    lo = (x - hi.astype(F32)).astype(BF16)
    return _dot(sel, hi) + _dot(sel, lo)


def _roll_two_vregs(x, r):
    x0, x1 = x[:, :LANES], x[:, LANES:]
    if r >= LANES:
        x0, x1, r = x1, x0, r - LANES
    if r == 0:
        return jnp.concatenate([x0, x1], axis=1)
    lane = jax.lax.broadcasted_iota(jnp.int32, x0.shape, 1)
    a0, a1 = pltpu.roll(x0, r, axis=1), pltpu.roll(x1, r, axis=1)
    return jnp.concatenate([jnp.where(lane < r, a1, a0), jnp.where(lane < r, a0, a1)], axis=1)


def _ssm_prep_kernel(pl_ref, pt_ref, ct_ref, bt_ref, dsk_ref, sl_ref, sr_ref, e_ref,
                     m_ref, b_ref, c_ref):
    t, h = CHUNK, SSM_GROUP
    kk = t * h
    lane = jax.lax.broadcasted_iota(jnp.int32, (h, kk), 1)
    rowi = jax.lax.broadcasted_iota(jnp.int32, (h, kk), 0)
    half = jax.lax.broadcasted_iota(jnp.int32, (LANES, kk), 0) // SSM_STATE

    cl, cq = [], []
    for d in range(2):
        ctr, cti = _select_lanes(ct_ref[0, d], e_ref[...]), _select_lanes(ct_ref[1, d], e_ref[...])
        for out, pat in ((cl, d), (cq, 2 + d)):
            pr, pi = _select_lanes(pl_ref[0, d], sl_ref[pat]), _select_lanes(pl_ref[1, d], sl_ref[pat])
            out.append((ctr * pr - cti * pi, ctr * pi + cti * pr))

    for d in range(2):
        for ri, val in enumerate((cq[d][0], -cq[d][1])):
            r0 = (2 * d + ri) * LANES
            for gi in range(2):
                c_ref[r0:r0 + LANES, gi * kk:(gi + 1) * kk] = (
                    jnp.where(half == gi, val, 0.0).astype(BF16))

    for d in range(2):
        pwr, pwi = _select_rows(sr_ref[d], pt_ref[0, d]), _select_rows(sr_ref[d], pt_ref[1, d])
        for gi in range(2):
            btr = jnp.concatenate([bt_ref[0, d, gi]] * t, axis=0)
            bti = jnp.concatenate([bt_ref[1, d, gi]] * t, axis=0)
            r0, c0 = gi * kk, 2 * d * LANES
            b_ref[r0:r0 + kk, c0:c0 + LANES] = (btr * pwr - bti * pwi).astype(BF16)
            b_ref[r0:r0 + kk, c0 + LANES:c0 + 2 * LANES] = (btr * pwi + bti * pwr).astype(BF16)

    for gi in range(2):
        gen = [_dot(bt_ref[0, d, gi].astype(BF16), cl[d][0].astype(BF16))
               - _dot(bt_ref[1, d, gi].astype(BF16), cl[d][1].astype(BF16)) for d in range(2)]
        diag = jnp.where(lane < h, gen[1], 0.0) + jnp.where(rowi == lane, dsk_ref[gi], 0.0)
        kf, kb = gen[0] + diag, gen[1]
        for s in range(t):
            blk = jnp.where(lane >= h * s, _roll_two_vregs(kf, h * s), _roll_two_vregs(kb, h * s))
            m_ref[gi, s * h:(s + 1) * h, :] = blk.astype(BF16)


def _ssm_prep_constants():
    t, h = CHUNK, SSM_GROUP
    kk = t * h
    step = np.arange(kk) // h
    k = np.arange(LANES)
    pats = [step, (t - step) % t, step + 1, t - step]
    sel_lane = np.stack([(k[:, None] == pat[None, :]) for pat in pats]).astype(np.float32)
    sel_row = np.stack([(pat[:, None] == k[None, :]) for pat in (t - 1 - step, step)])
    tile_ho = (k[:, None] == (np.arange(kk) % h)[None, :])
    as_bf16 = lambda a: jnp.asarray(a, F32).astype(BF16)
    return as_bf16(sel_lane), as_bf16(sel_row), as_bf16(tile_ho)


def _ssm_prep(pl_tab, pt_tab, ct_tab, bt_tab, dsk):
    depth, nq = pl_tab.shape[:2]
    kk = CHUNK * SSM_GROUP
    sel_lane, sel_row, tile_ho = _ssm_prep_constants()
    tab5 = pl.BlockSpec((None, None, 2, 2, LANES, LANES), lambda l, q: (l, q, 0, 0, 0, 0))
    return pl.pallas_call(
        _ssm_prep_kernel,
        grid=(depth, nq),
        in_specs=[tab5, tab5, tab5,
                  pl.BlockSpec((None, None, 2, 2, 2, SSM_GROUP, LANES),
                               lambda l, q: (l, q, 0, 0, 0, 0, 0)),
                  pl.BlockSpec((None, None, 2, 1, kk), lambda l, q: (l, q, 0, 0, 0)),
                  _const_spec(sel_lane.shape), _const_spec(sel_row.shape),
                  _const_spec(tile_ho.shape)],
        out_specs=[pl.BlockSpec((None, 2, kk, kk), lambda l, q: (l, q, 0, 0)),
                   pl.BlockSpec((None, None, 2 * kk, 2 * kk), lambda l, q: (l, q, 0, 0)),
                   pl.BlockSpec((None, None, 2 * kk, 2 * kk), lambda l, q: (l, q, 0, 0))],
        out_shape=[jax.ShapeDtypeStruct((depth, 2 * nq, kk, kk), BF16),
                   jax.ShapeDtypeStruct((depth, nq, 2 * kk, 2 * kk), BF16),
                   jax.ShapeDtypeStruct((depth, nq, 2 * kk, 2 * kk), BF16)],
        compiler_params=pltpu.CompilerParams(
            dimension_semantics=("parallel", "parallel"), vmem_limit_bytes=VMEM_LIMIT),
        name="ssm_prep",
    )(pl_tab, pt_tab, ct_tab, bt_tab, dsk, sel_lane, sel_row, tile_ho)


MIX_SUBBLOCKS = 2
FF_BLOCK = 1024
FFT2_K1_BLOCK = 2 * SUBLANES


def _mix_mlp_kernel(x_ref, gr_ref, gi_ref, ys_ref, bdr_ref, bdi_ref, fg_ref, wg_ref, bg_ref,
                    sg_ref, wo_ref, pmg_ref, plg_ref, w1_ref, w2_ref, pog_ref, o_ref, *, ff_blk,
                    nsub):
    n1s = x_ref.shape[0] // nsub
    rows = n1s * CHUNK

    def mix(k):
        r = slice(k * rows, (k + 1) * rows)
        s1 = slice(k * n1s, (k + 1) * n1s)
        gr = gr_ref[s1].reshape(rows, D_FOURIER)
        gi = gi_ref[s1].reshape(rows, D_FOURIER)
        yf = jnp.concatenate(
            [_dot(gr[:, c:c + MXU_DIM], bdr_ref[c:c + MXU_DIM, c:c + MXU_DIM])
             + _dot(gi[:, c:c + MXU_DIM], bdi_ref[c:c + MXU_DIM, c:c + MXU_DIM])
             for c in range(0, D_FOURIER, MXU_DIM)], axis=1)
        yf = _rms(yf, fg_ref[...])
        y = _unrotate_rows(ys_ref[r, :])
        gl = jax.nn.gelu(y, approximate=True)
        gate = _dot(gl.astype(BF16), wg_ref[...]) + bg_ref[...]
        ys = _rms(gl * jax.nn.sigmoid(gate), sg_ref[...])
        cat = jnp.concatenate([yf.astype(BF16), ys.astype(BF16)], axis=-1)
        x1 = x_ref[s1].reshape(rows, D_MODEL) + _rms(_dot(cat, wo_ref[...]), pmg_ref[...])
        return x1, _rms(x1, plg_ref[...]).astype(BF16)

    def mlp(k, x1, h):
        acc = jnp.zeros(x1.shape, F32)
        for j in range(D_FF // ff_blk):
            a = _dot(h, w1_ref[:, j * ff_blk:(j + 1) * ff_blk])
            a = jnp.square(jnp.maximum(a, 0.0)).astype(BF16)
            acc = acc + _dot(a, w2_ref[j * ff_blk:(j + 1) * ff_blk, :])
        o_ref[k * n1s:(k + 1) * n1s] = (x1 + _rms(acc, pog_ref[...])).reshape(n1s, CHUNK, D_MODEL)

    nxt = mix(0)
    for k in range(nsub):
        cur = nxt
        if k + 1 < nsub:
            nxt = mix(k + 1)
        mlp(k, *cur)


def _mix_mlp(x5, gri, ys, bdr, bdi, fg, wg, bg, sg, wo, pmg, plg, w1, w2, pog, ff_blk, layer):
    bsz, n1, nj, t, _ = x5.shape
    lspec = functools.partial(_layer_spec, layer=layer)
    slab = lambda c: pl.BlockSpec((None, n1, None, t, c), lambda b, j: (b, 0, j, 0, 0))
    return pl.pallas_call(
        functools.partial(_mix_mlp_kernel, ff_blk=ff_blk, nsub=MIX_SUBBLOCKS),
        grid=(bsz, nj),
        in_specs=[slab(D_MODEL),
                  pl.BlockSpec((None, None, n1, None, t, D_FOURIER),
                               lambda b, j: (0, b, 0, j, 0, 0)),
                  pl.BlockSpec((None, None, n1, None, t, D_FOURIER),
                               lambda b, j: (1, b, 0, j, 0, 0)),
                  pl.BlockSpec((None, None, n1 * t, D_SSM), lambda b, j: (b, j, 0, 0)),
                  lspec((D_FOURIER, D_FOURIER)), lspec((D_FOURIER, D_FOURIER)),
                  lspec((1, D_FOURIER)),
                  lspec((D_SSM, D_SSM)), lspec((1, D_SSM)), lspec((1, D_SSM)),
                  lspec((D_MODEL, D_MODEL)), lspec((1, D_MODEL)),
                  lspec((1, D_MODEL)),
                  lspec((D_MODEL, D_FF)), lspec((D_FF, D_MODEL)),
                  lspec((1, D_MODEL))],
        out_specs=slab(D_MODEL),
        out_shape=jax.ShapeDtypeStruct(x5.shape, F32),
        compiler_params=pltpu.CompilerParams(
            dimension_semantics=("parallel", "parallel"), vmem_limit_bytes=VMEM_LIMIT),
        name="mix_mlp",
    )(x5, gri, gri, ys, bdr, bdi, fg, wg, bg, sg, wo, pmg, plg, w1, w2, pog)


def _fourier_channel_maps(w_f, seq):
    c = np.arange(FOURIER_HEAD_DIM)
    ang = 2.0 * np.pi * ((c[:, None] * c[None, :]) % FOURIER_HEAD_DIM) / FOURIER_HEAD_DIM
    scale = 1.0 / math.sqrt(seq * FOURIER_HEAD_DIM)
    hp = jax.lax.Precision.HIGHEST
    cw = jnp.einsum("cd,hde->hce", jnp.asarray(np.cos(ang) * scale, F32), w_f.astype(F32), precision=hp)
    sw = jnp.einsum("cd,hde->hce", jnp.asarray(np.sin(ang) * scale, F32), w_f.astype(F32), precision=hp)
    eye = jnp.eye(FOURIER_HEADS, dtype=F32)
    bd = lambda m: jnp.einsum("hce,hk->hcke", m, eye).reshape(D_FOURIER, D_FOURIER)
    return bd(cw).astype(BF16), bd(sw).astype(BF16)


def kernel(x, w_in, w_out, pre_mix_g, post_mix_g, pre_mlp_g, post_mlp_g, fourier_out_g, ssm_out_g,
           w_fourier, lam_re, lam_im, log_dt, b_re, b_im, c_re, c_im, d_skip, w_glu, b_glu,
           w_ff1, w_ff2):
    bsz, seq, _ = x.shape
    depth = w_in.shape[0]
    n = bsz * seq
    n1, n2 = FFT_N1, seq // FFT_N1
    nj = n2 // CHUNK
    assert seq % (FFT_N1 * N2_BLK) == 0 and N2_BLK == CHUNK and n1 % RELAYOUT_CHUNKS == 0
    wk_tab, m2_tab = _fft_tables(seq)

    rows = lambda v: v.reshape(depth, 1, -1).astype(F32)
    w_in_b, w_out_b, w_glu_b = w_in.astype(BF16), w_out.astype(BF16), w_glu.astype(BF16)
    w_ff1_b, w_ff2_b = w_ff1.astype(BF16), w_ff2.astype(BF16)
    pl_tab, pt_tab, ct_tab, bt_tab, dsk, pw, tab = jax.vmap(
        functools.partial(_ssm_tables, nj=nj))(
            lam_re, lam_im, log_dt, b_re, b_im, c_re, c_im, d_skip)
    m_intra, b_pair, c_pair = _ssm_prep(pl_tab, pt_tab, ct_tab, bt_tab, dsk)
    bdr, bdi = jax.vmap(functools.partial(_fourier_channel_maps, seq=seq))(w_fourier)
    gains = [rows(v) for v in (pre_mix_g, fourier_out_g, b_glu, ssm_out_g, post_mix_g,
                               pre_mlp_g, post_mlp_g)]
    pre_mix, four_g, glu_b, ssm_g, post_mix, pre_mlp, post_mlp = gains

    xs = x.reshape(bsz, n1, n2, D_MODEL).astype(F32)
    for i in range(depth):
        y, zs = _in_proj(xs, pre_mix, w_in_b, wk_tab, i)
        g = _fft_stage2(y.reshape(bsz, n1, 2 * n2, D_FOURIER), m2_tab, kb=FFT2_K1_BLOCK)
        gri = g.reshape(2, bsz, n1, nj, CHUNK, D_FOURIER)
        ysr = _ssm_chunked(zs, m_intra, b_pair, c_pair, pw, tab, i)
        xs = _mix_mlp(xs.reshape(bsz, n1, nj, CHUNK, D_MODEL), gri, ysr, bdr, bdi, four_g,
                      w_glu_b, glu_b, ssm_g, w_out_b, post_mix, pre_mlp, w_ff1_b, w_ff2_b,
                      post_mlp, ff_blk=FF_BLOCK, layer=i).reshape(bsz, n1, n2, D_MODEL)
    return xs.reshape(bsz, seq, D_MODEL).astype(x.dtype)
```

```python
import functools
import math

import jax
import jax.numpy as jnp
import numpy as np
from jax.experimental import pallas as pl
from jax.experimental.pallas import tpu as pltpu

EPS = 1e-6
D_MODEL = 1024
D_FOURIER = 512
D_SSM = 512
FOURIER_HEADS = 8
FOURIER_HEAD_DIM = 64
SSM_GROUP = 16
SSM_GROUPS = 32
SSM_STATE = 64
D_FF = 4096

FFT_N1 = 64
CHUNK = 16
LANES = 128
SUBLANES = 8
MXU_DIM = 256
VMEM_LIMIT = 56 * 1024 * 1024

BF16 = jnp.bfloat16
F32 = jnp.float32


def _rms(x, g):
    return x * jax.lax.rsqrt(jnp.mean(x * x, axis=-1, keepdims=True) + EPS) * g


def _dot(a, b):
    return jnp.dot(a, b, preferred_element_type=F32)


def _const_spec(shape):
    nd = len(shape)
    return pl.BlockSpec(shape, lambda *_: (0,) * nd, pipeline_mode=pl.Buffered(1))


def _layer_spec(shape, layer):
    nd = len(shape)
    return pl.BlockSpec((None,) + tuple(shape), lambda *_: (layer,) + (0,) * nd,
                        pipeline_mode=pl.Buffered(1))


KRON_BLK = SUBLANES
N2_BLK = 2 * KRON_BLK
REGROUP_PITCH = N2_BLK + 4


def _in_proj_kernel(x_ref, g_ref, w_ref, wk_ref, y_ref, zs_ref, zt_ref):
    n1 = x_ref.shape[0]
    x = x_ref[...].reshape(n1 * N2_BLK, D_MODEL)
    h = _rms(x, g_ref[...]).astype(BF16)
    z = _dot(h, w_ref[...])
    zf = z[:, :D_FOURIER].reshape(n1, N2_BLK // KRON_BLK, KRON_BLK, D_FOURIER)
    parts = []
    for a in range(N2_BLK // KRON_BLK):
        za = zf[:, a].reshape(n1 * KRON_BLK, D_FOURIER).astype(BF16)
        ya = _dot(wk_ref[...], za)
        parts.append(ya.reshape(2 * n1, KRON_BLK, D_FOURIER))
    y = jnp.concatenate(parts, axis=1)
    y_ref[...] = y.astype(BF16).reshape(n1, 2, N2_BLK, D_FOURIER)
    zrot = _rotate_rows(z[:, D_FOURIER:])
    for s in range(D_SSM // LANES):
        for a in range(n1):
            zt_ref[s, a * REGROUP_PITCH:a * REGROUP_PITCH + CHUNK, :] = (
                zrot[a * CHUNK:(a + 1) * CHUNK, s * LANES:(s + 1) * LANES])
    for t in range(CHUNK):
        for s in range(D_SSM // LANES):
            zs_ref[t * n1:(t + 1) * n1, s * LANES:(s + 1) * LANES] = (
                zt_ref[s, pl.ds(t, n1, stride=REGROUP_PITCH), :])


def _in_proj(x4, g, w_bf16, wk, layer):
    bsz, n1, n2, _ = x4.shape
    rows = n1 * KRON_BLK
    return pl.pallas_call(
        _in_proj_kernel,
        grid=(bsz, n2 // N2_BLK),
        in_specs=[pl.BlockSpec((None, n1, N2_BLK, D_MODEL), lambda b, j: (b, 0, j, 0)),
                  _layer_spec((1, D_MODEL), layer),
                  _layer_spec((D_MODEL, D_MODEL), layer),
                  _const_spec((2 * rows, rows))],
        out_specs=[pl.BlockSpec((None, n1, 2, N2_BLK, D_FOURIER), lambda b, j: (b, 0, 0, j, 0)),
                   pl.BlockSpec((None, None, N2_BLK * n1, D_SSM), lambda b, j: (b, j, 0, 0))],
        out_shape=[jax.ShapeDtypeStruct((bsz, n1, 2, n2, D_FOURIER), BF16),
                   jax.ShapeDtypeStruct((bsz, n2 // N2_BLK, N2_BLK * n1, D_SSM), F32)],
        scratch_shapes=[pltpu.VMEM((D_SSM // LANES, REGROUP_PITCH * n1, LANES), F32)],
        compiler_params=pltpu.CompilerParams(
            dimension_semantics=("parallel", "parallel"), vmem_limit_bytes=VMEM_LIMIT),
        name="in_proj",
    )(x4, g, w_bf16, wk)


REGROUP_UNROLL = 8


def _fft2_kernel(m_ref, y_ref, g_ref, z_ref, *, kb, n2, pitch):
    nslab = D_FOURIER // LANES
    for j in range(kb):
        z = _dot(m_ref[j], y_ref[j])
        for ri in range(2):
            for s in range(nslab):
                z_ref[ri, s, j * pitch:j * pitch + n2, :] = (
                    z[ri * n2:(ri + 1) * n2, s * LANES:(s + 1) * LANES])

    def regroup(i, _):
        for u in range(REGROUP_UNROLL):
            k2 = i * REGROUP_UNROLL + u
            for ri in range(2):
                for s in range(nslab):
                    rows = [z_ref[ri, s, pl.ds(k2 + a * SUBLANES * pitch, SUBLANES, stride=pitch), :]
                            for a in range(kb // SUBLANES)]
                    g_ref[ri, k2, :, s * LANES:(s + 1) * LANES] = (
                        jnp.concatenate(rows, axis=0).astype(BF16))
        return 0

    jax.lax.fori_loop(0, n2 // REGROUP_UNROLL, regroup, 0)


def _fft_stage2(y, m2, kb):
    b, n1, rows, c = y.shape
    n2 = rows // 2
    pitch = n2 + 4
    return pl.pallas_call(
        functools.partial(_fft2_kernel, kb=kb, n2=n2, pitch=pitch),
        grid=(n1 // kb, b),
        in_specs=[pl.BlockSpec((kb, rows, rows), lambda j, i: (j, 0, 0)),
                  pl.BlockSpec((None, kb, rows, c), lambda j, i: (i, j, 0, 0))],
        out_specs=pl.BlockSpec((2, None, n2, kb, c), lambda j, i: (0, i, 0, j, 0)),
        out_shape=jax.ShapeDtypeStruct((2, b, n2, n1, c), BF16),
        scratch_shapes=[pltpu.VMEM((2, c // LANES, kb * pitch, LANES), F32)],
        compiler_params=pltpu.CompilerParams(
            dimension_semantics=("parallel", "parallel"), vmem_limit_bytes=VMEM_LIMIT),
        name="fft_stage2",
    )(m2, y)


def _fft_tables(seq):
    n1, n2 = FFT_N1, seq // FFT_N1
    k1 = np.arange(n1)
    ang1 = 2.0 * np.pi * ((k1[:, None] * k1[None, :]) % n1) / n1
    w1 = np.stack([np.cos(ang1), -np.sin(ang1)], axis=1).reshape(2 * n1, n1)
    wk = np.einsum("kn,ab->kanb", w1, np.eye(KRON_BLK)).reshape(2 * n1 * KRON_BLK, n1 * KRON_BLK)
    k2 = np.arange(n2)
    freq = k1[:, None, None] + n1 * k2[None, :, None]
    ang2 = 2.0 * np.pi * ((freq * k2[None, None, :]) % seq) / seq
    mr, mi = np.cos(ang2), -np.sin(ang2)
    m2 = np.concatenate([np.concatenate([mr, -mi], axis=2),
                         np.concatenate([mi, mr], axis=2)], axis=1)
    return jnp.asarray(wk, F32).astype(BF16), jnp.asarray(m2, F32).astype(BF16)


GROUPS_PER_BLOCK = LANES // SSM_GROUP
PAIRS_PER_BLOCK = GROUPS_PER_BLOCK // 2
RELAYOUT_CHUNKS = 8 * SUBLANES
SCAN_STEPS = (1, 2, 4)
SCAN_SLOTS = 2 * (len(SCAN_STEPS) + 1)


def _rotate_rows(v):
    return jnp.concatenate(
        [pltpu.roll(v[:, c:c + LANES], 0, axis=1, stride=SSM_GROUP, stride_axis=0)
         for c in range(0, v.shape[1], LANES)], axis=1)


def _unrotate_rows(v):
    row = jax.lax.broadcasted_iota(jnp.int32, v.shape, 0)
    cols = []
    for c in range(0, v.shape[1], LANES):
        w = v[:, c:c + LANES]
        for bit in (1, 2, 4):
            w = jnp.where((row[:, :LANES] & bit) != 0,
                          pltpu.roll(w, LANES - SSM_GROUP * bit, axis=1), w)
        cols.append(w)
    return jnp.concatenate(cols, axis=1)


def _ssm_kernel(z_ref, mi_ref, bs_ref, cs_ref, pw_ref, tab_ref, y_ref,
                u2_ref, s_ref, car_ref, yg_ref, *, nj, nrow):
    gpb, npair = GROUPS_PER_BLOCK, PAIRS_PER_BLOCK
    t16 = RELAYOUT_CHUNKS
    parts = nrow // t16
    lane_blk = jax.lax.broadcasted_iota(jnp.int32, (t16, LANES), 1) // SSM_GROUP
    masks = [lane_blk == m for m in range(gpb)]

    def relayout_in(i, _):
        j, r0 = i // parts, pl.multiple_of((i % parts) * t16, t16)
        ut = [z_ref[j, pl.ds(t * nrow + r0, t16), :] for t in range(CHUNK)]
        c0 = pl.multiple_of(j * nrow + r0, t16)
        for g in range(gpb):
            for hf in range(CHUNK // gpb):
                acc = ut[gpb * hf]
                for tt in range(1, gpb):
                    acc = jnp.where(masks[(g + tt) % gpb], ut[gpb * hf + tt], acc)
                if g:
                    acc = pltpu.roll(acc, LANES - SSM_GROUP * g, axis=1)
                u2_ref[g, pl.ds(c0, t16), hf * LANES:(hf + 1) * LANES] = acc.astype(BF16)
        return 0

    jax.lax.fori_loop(0, nj * parts, relayout_in, 0)

    for q in range(npair):
        up = jnp.concatenate([u2_ref[2 * q], u2_ref[2 * q + 1]], axis=1)
        s_ref[q] = _dot(up, bs_ref[q])

    row = jax.lax.broadcasted_iota(jnp.int32, (SUBLANES, LANES), 0)
    nvreg = nrow // SUBLANES

    def scan_block(tab, sr, si, cr, ci, down):
        keep = (lambda sh: row >= sh) if down else (lambda sh: row <= SUBLANES - 1 - sh)
        amt = (lambda sh: sh) if down else (lambda sh: SUBLANES - sh)
        for k, sh in enumerate(SCAN_STEPS):
            tr = pltpu.roll(sr, amt(sh), axis=0)
            ti = pltpu.roll(si, amt(sh), axis=0)
            ar, ai = tab[2 * k], tab[2 * k + 1]
            sr, si = sr + (ar * tr - ai * ti), si + (ar * ti + ai * tr)
        er, ei = tab[SCAN_SLOTS - 2], tab[SCAN_SLOTS - 1]
        fr = sr + (er * cr - ei * ci)
        fi = si + (er * ci + ei * cr)
        outr = jnp.where(keep(1), pltpu.roll(fr, amt(1), axis=0), cr)
        outi = jnp.where(keep(1), pltpu.roll(fi, amt(1), axis=0), ci)
        edge = SUBLANES - 1 if down else 0
        ncr = jnp.broadcast_to(fr[edge:edge + 1, :], (SUBLANES, LANES))
        nci = jnp.broadcast_to(fi[edge:edge + 1, :], (SUBLANES, LANES))
        return outr, outi, ncr, nci

    def tile_rows(x):
        return jnp.concatenate([x] * nvreg, axis=0)

    zero = jnp.zeros((SUBLANES, LANES), F32)
    for q in range(npair):
        for d in range(2):
            down = d == 0
            lo_r, lo_i = 2 * d * LANES, (2 * d + 1) * LANES
            slab = lambda j: slice(j * nrow, (j + 1) * nrow)
            order = list(range(nj)) if down else list(range(nj - 1, -1, -1))
            a_r, a_i = tile_rows(pw_ref[q, d, 0]), tile_rows(pw_ref[q, d, 1])
            lr = li = None
            for idx, j in enumerate(order):
                sr = s_ref[q, slab(j), lo_r:lo_r + LANES]
                si = s_ref[q, slab(j), lo_i:lo_i + LANES]
                if idx == 0:
                    lr, li = sr, si
                else:
                    lr, li = sr + (a_r * lr - a_i * li), si + (a_r * li + a_i * lr)
                if idx + 1 < nj:
                    car_ref[q, slab(order[idx + 1]), lo_r:lo_r + LANES] = lr
                    car_ref[q, slab(order[idx + 1]), lo_i:lo_i + LANES] = li
            tab = [tab_ref[q, d, k] for k in range(SCAN_SLOTS)]
            cr, ci = zero, zero
            gr_rows, gi_rows = [None] * nvreg, [None] * nvreg
            for v in (range(nvreg) if down else range(nvreg - 1, -1, -1)):
                rows8 = slice(v * SUBLANES, (v + 1) * SUBLANES)
                gr_rows[v], gi_rows[v], cr, ci = scan_block(tab, lr[rows8], li[rows8], cr, ci, down)
            gr, gi = jnp.concatenate(gr_rows, axis=0), jnp.concatenate(gi_rows, axis=0)
            car_ref[q, slab(order[0]), lo_r:lo_r + LANES] = gr
            car_ref[q, slab(order[0]), lo_i:lo_i + LANES] = gi
            for idx, j in enumerate(order[1:], start=1):
                p_r = tile_rows(pw_ref[q, d, 2 * (idx - 1)])
                p_i = tile_rows(pw_ref[q, d, 2 * (idx - 1) + 1])
                car_ref[q, slab(j), lo_r:lo_r + LANES] += p_r * gr - p_i * gi
                car_ref[q, slab(j), lo_i:lo_i + LANES] += p_r * gi + p_i * gr

    kk = CHUNK * SSM_GROUP
    for q in range(npair):
        yi = _dot(car_ref[q].astype(BF16), cs_ref[q])
        for e in range(2):
            g = 2 * q + e
            yg_ref[g] = _dot(u2_ref[g], mi_ref[g]) + yi[:, e * kk:(e + 1) * kk]

    def relayout_out(i, _):
        j, r0 = i // parts, pl.multiple_of((i % parts) * t16, t16)
        c0 = pl.multiple_of(j * nrow + r0, t16)
        yrot = []
        for g in range(gpb):
            halves = []
            for hf in range(CHUNK // gpb):
                v = yg_ref[g, pl.ds(c0, t16), hf * LANES:(hf + 1) * LANES]
                halves.append(pltpu.roll(v, SSM_GROUP * g, axis=1) if g else v)
            yrot.append(halves)
        for t in range(CHUNK):
            hf, tt = divmod(t, gpb)
            acc = yrot[0][hf]
            for g in range(1, gpb):
                acc = jnp.where(masks[(g + tt) % gpb], yrot[g][hf], acc)
            y_ref[j, pl.ds(r0 * CHUNK + t, t16, stride=CHUNK), :] = acc
        return 0

    jax.lax.fori_loop(0, nj * parts, relayout_out, 0)


def _ssm_chunked(zs, m_intra, b_pair, c_pair, pw, tab, layer):
    bsz, nj, rows, _ = zs.shape
    nrow = rows // CHUNK
    nc = nj * nrow
    gpb, npair = GROUPS_PER_BLOCK, PAIRS_PER_BLOCK
    k = CHUNK * SSM_GROUP
    return pl.pallas_call(
        functools.partial(_ssm_kernel, nj=nj, nrow=nrow),
        grid=(D_SSM // LANES, bsz),
        in_specs=[pl.BlockSpec((None, nj, rows, LANES), lambda j, b: (b, 0, 0, j)),
                  pl.BlockSpec((None, gpb, k, k), lambda j, b: (layer, j, 0, 0)),
                  pl.BlockSpec((None, npair, 2 * k, 2 * k), lambda j, b: (layer, j, 0, 0)),
                  pl.BlockSpec((None, npair, 2 * k, 2 * k), lambda j, b: (layer, j, 0, 0)),
                  pl.BlockSpec((None, npair, 2, pw.shape[3], SUBLANES, LANES),
                               lambda j, b: (layer, j, 0, 0, 0, 0)),
                  pl.BlockSpec((None, npair, 2, SCAN_SLOTS, SUBLANES, LANES),
                               lambda j, b: (layer, j, 0, 0, 0, 0))],
        out_specs=pl.BlockSpec((None, nj, rows, LANES), lambda j, b: (b, 0, 0, j)),
        out_shape=jax.ShapeDtypeStruct((bsz, nj, rows, D_SSM), F32),
        scratch_shapes=[pltpu.VMEM((gpb, nc, k), BF16),
                        pltpu.VMEM((npair, nc, 2 * k), F32),
                        pltpu.VMEM((npair, nc, 2 * k), F32),
                        pltpu.VMEM((gpb, nc, k), F32)],
        compiler_params=pltpu.CompilerParams(
            dimension_semantics=("parallel", "parallel"), vmem_limit_bytes=VMEM_LIMIT),
        name="ssm_chunk",
    )(zs, m_intra, b_pair, c_pair, pw, tab)


def _ssm_tables(lam_re, lam_im, log_dt, b_re, b_im, c_re, c_im, d_skip, *, nj):
    t, h, p, g = CHUNK, SSM_GROUP, SSM_STATE, SSM_GROUPS
    nq = g // 2
    dt = jnp.exp(log_dt.astype(F32))[..., None]
    are = lam_re.astype(F32) * dt
    aim = lam_im.astype(F32) * dt

    def power(expo, a_re, a_im):
        mag = jnp.exp(a_re * expo)
        return mag * jnp.cos(a_im * expo), mag * jnp.sin(a_im * expo)

    lbr, lbi = power(1.0, are, aim)
    den = lam_re * lam_re + lam_im * lam_im
    cr = ((lbr - 1.0) * lam_re + lbi * lam_im) / den
    ci = (lbi * lam_re - (lbr - 1.0) * lam_im) / den
    bbr = cr[..., None] * b_re - ci[..., None] * b_im
    bbi = cr[..., None] * b_im + ci[..., None] * b_re

    def pair_rows(x):
        return x.reshape(2, 2, nq, 2 * p, x.shape[-1]).transpose(2, 0, 1, 3, 4)

    lane_pad = lambda x: jnp.pad(x, [(0, 0)] * (x.ndim - 1) + [(0, LANES - x.shape[-1])])
    steps = jnp.arange(t + 1, dtype=F32)
    pw = jnp.stack(power(steps, are[..., None], aim[..., None]))
    pl_tab = lane_pad(pair_rows(pw))
    pt_tab = jnp.pad(jnp.swapaxes(pair_rows(pw), -1, -2),
                     [(0, 0)] * 3 + [(0, LANES - t - 1), (0, 0)])
    ct_tab = lane_pad(pair_rows(jnp.stack([jnp.swapaxes(c_re.astype(F32), -1, -2),
                                           jnp.swapaxes(c_im.astype(F32), -1, -2)])))
    bt = jnp.stack([jnp.swapaxes(bbr, -1, -2), jnp.swapaxes(bbi, -1, -2)])
    bt = bt.reshape(2, 2, nq, 2, h, p)
    zero = jnp.zeros_like(bt[:, :, :, 0])
    bt_tab = jnp.stack([jnp.concatenate([bt[:, :, :, 0], zero], -1),
                        jnp.concatenate([zero, bt[:, :, :, 1]], -1)], axis=3)
    bt_tab = bt_tab.transpose(2, 0, 1, 3, 4, 5)
    dsk = jnp.pad(d_skip.astype(F32), ((0, 0), (0, t * h - h))).reshape(nq, 2, 1, t * h)

    pair_lanes = lambda x: x.reshape(2, nq, 1, 2 * p)
    ksteps = jnp.arange(1, max(nj, 2), dtype=F32)[:, None, None, None, None] * t
    pwr, pwi = power(ksteps, pair_lanes(are)[None], pair_lanes(aim)[None])
    pw = jnp.stack([pwr, pwi], axis=1).reshape(-1, 2, nq, 1, 2 * p)
    pw = jnp.broadcast_to(pw, pw.shape[:3] + (SUBLANES, 2 * p)).transpose(2, 1, 0, 3, 4)

    r8 = np.arange(SUBLANES)
    expo = np.stack([np.stack([np.full(SUBLANES, k), np.full(SUBLANES, k)]) for k in SCAN_STEPS]
                    + [np.stack([r8 + 1, SUBLANES - r8])])
    expo = jnp.asarray(t * nj * expo, F32)[:, :, None, :, None]
    ar, ai = power(expo, pair_lanes(are)[None], pair_lanes(aim)[None])
    live = np.stack([np.stack([r8 >= k, r8 <= SUBLANES - 1 - k]) for k in SCAN_STEPS]
                    + [np.ones((2, SUBLANES), bool)])
    live = jnp.asarray(live, F32)[:, :, None, :, None]
    ar, ai = ar * live, ai * live
    tab = jnp.stack([ar, ai], axis=1).reshape(SCAN_SLOTS, 2, nq, SUBLANES, 2 * p)
    tab = tab.transpose(2, 1, 0, 3, 4)
    return pl_tab, pt_tab, ct_tab, bt_tab, dsk, pw, tab


def _select_lanes(x, sel):
    hi = x.astype(BF16)
    lo = (x - hi.astype(F32)).astype(BF16)
    return _dot(hi, sel) + _dot(lo, sel)


def _select_rows(sel, x):
    hi = x.astype(BF16)
    lo = (x - hi.astype(F32)).astype(BF16)
    return _dot(sel, hi) + _dot(sel, lo)


def _roll_two_vregs(x, r):
    x0, x1 = x[:, :LANES], x[:, LANES:]
    if r >= LANES:
        x0, x1, r = x1, x0, r - LANES
    if r == 0:
        return jnp.concatenate([x0, x1], axis=1)
    lane = jax.lax.broadcasted_iota(jnp.int32, x0.shape, 1)
    a0, a1 = pltpu.roll(x0, r, axis=1), pltpu.roll(x1, r, axis=1)
    return jnp.concatenate([jnp.where(lane < r, a1, a0), jnp.where(lane < r, a0, a1)], axis=1)


def _ssm_prep_kernel(pl_ref, pt_ref, ct_ref, bt_ref, dsk_ref, sl_ref, sr_ref, e_ref,
                     m_ref, b_ref, c_ref):
    t, h = CHUNK, SSM_GROUP
    kk = t * h
    lane = jax.lax.broadcasted_iota(jnp.int32, (h, kk), 1)
    rowi = jax.lax.broadcasted_iota(jnp.int32, (h, kk), 0)
    half = jax.lax.broadcasted_iota(jnp.int32, (LANES, kk), 0) // SSM_STATE

    cl, cq = [], []
    for d in range(2):
        ctr, cti = _select_lanes(ct_ref[0, d], e_ref[...]), _select_lanes(ct_ref[1, d], e_ref[...])
        for out, pat in ((cl, d), (cq, 2 + d)):
            pr, pi = _select_lanes(pl_ref[0, d], sl_ref[pat]), _select_lanes(pl_ref[1, d], sl_ref[pat])
            out.append((ctr * pr - cti * pi, ctr * pi + cti * pr))

    for d in range(2):
        for ri, val in enumerate((cq[d][0], -cq[d][1])):
            r0 = (2 * d + ri) * LANES
            for gi in range(2):
                c_ref[r0:r0 + LANES, gi * kk:(gi + 1) * kk] = (
                    jnp.where(half == gi, val, 0.0).astype(BF16))

    for d in range(2):
        pwr, pwi = _select_rows(sr_ref[d], pt_ref[0, d]), _select_rows(sr_ref[d], pt_ref[1, d])
        for gi in range(2):
            btr = jnp.concatenate([bt_ref[0, d, gi]] * t, axis=0)
            bti = jnp.concatenate([bt_ref[1, d, gi]] * t, axis=0)
            r0, c0 = gi * kk, 2 * d * LANES
            b_ref[r0:r0 + kk, c0:c0 + LANES] = (btr * pwr - bti * pwi).astype(BF16)
            b_ref[r0:r0 + kk, c0 + LANES:c0 + 2 * LANES] = (btr * pwi + bti * pwr).astype(BF16)

    for gi in range(2):
        gen = [_dot(bt_ref[0, d, gi].astype(BF16), cl[d][0].astype(BF16))
               - _dot(bt_ref[1, d, gi].astype(BF16), cl[d][1].astype(BF16)) for d in range(2)]
        diag = jnp.where(lane < h, gen[1], 0.0) + jnp.where(rowi == lane, dsk_ref[gi], 0.0)
        kf, kb = gen[0] + diag, gen[1]
        for s in range(t):
            blk = jnp.where(lane >= h * s, _roll_two_vregs(kf, h * s), _roll_two_vregs(kb, h * s))
            m_ref[gi, s * h:(s + 1) * h, :] = blk.astype(BF16)


def _ssm_prep_constants():
    t, h = CHUNK, SSM_GROUP
    kk = t * h
    step = np.arange(kk) // h
    k = np.arange(LANES)
    pats = [step, (t - step) % t, step + 1, t - step]
    sel_lane = np.stack([(k[:, None] == pat[None, :]) for pat in pats]).astype(np.float32)
    sel_row = np.stack([(pat[:, None] == k[None, :]) for pat in (t - 1 - step, step)])
    tile_ho = (k[:, None] == (np.arange(kk) % h)[None, :])
    as_bf16 = lambda a: jnp.asarray(a, F32).astype(BF16)
    return as_bf16(sel_lane), as_bf16(sel_row), as_bf16(tile_ho)


def _ssm_prep(pl_tab, pt_tab, ct_tab, bt_tab, dsk):
    depth, nq = pl_tab.shape[:2]
    kk = CHUNK * SSM_GROUP
    sel_lane, sel_row, tile_ho = _ssm_prep_constants()
    tab5 = pl.BlockSpec((None, None, 2, 2, LANES, LANES), lambda l, q: (l, q, 0, 0, 0, 0))
    return pl.pallas_call(
        _ssm_prep_kernel,
        grid=(depth, nq),
        in_specs=[tab5, tab5, tab5,
                  pl.BlockSpec((None, None, 2, 2, 2, SSM_GROUP, LANES),
                               lambda l, q: (l, q, 0, 0, 0, 0, 0)),
                  pl.BlockSpec((None, None, 2, 1, kk), lambda l, q: (l, q, 0, 0, 0)),
                  _const_spec(sel_lane.shape), _const_spec(sel_row.shape),
                  _const_spec(tile_ho.shape)],
        out_specs=[pl.BlockSpec((None, 2, kk, kk), lambda l, q: (l, q, 0, 0)),
                   pl.BlockSpec((None, None, 2 * kk, 2 * kk), lambda l, q: (l, q, 0, 0)),
                   pl.BlockSpec((None, None, 2 * kk, 2 * kk), lambda l, q: (l, q, 0, 0))],
        out_shape=[jax.ShapeDtypeStruct((depth, 2 * nq, kk, kk), BF16),
                   jax.ShapeDtypeStruct((depth, nq, 2 * kk, 2 * kk), BF16),
                   jax.ShapeDtypeStruct((depth, nq, 2 * kk, 2 * kk), BF16)],
        compiler_params=pltpu.CompilerParams(
            dimension_semantics=("parallel", "parallel"), vmem_limit_bytes=VMEM_LIMIT),
        name="ssm_prep",
    )(pl_tab, pt_tab, ct_tab, bt_tab, dsk, sel_lane, sel_row, tile_ho)


MIX_SUBBLOCKS = 2
FF_BLOCK = 1024
FFT2_K1_BLOCK = 2 * SUBLANES


def _mix_mlp_kernel(x_ref, gr_ref, gi_ref, ys_ref, bdr_ref, bdi_ref, fg_ref, wg_ref, bg_ref,
                    sg_ref, wo_ref, pmg_ref, plg_ref, w1_ref, w2_ref, pog_ref, o_ref, *, ff_blk,
                    nsub):
    n1s = x_ref.shape[0] // nsub
    rows = n1s * CHUNK

    def mix(k):
        r = slice(k * rows, (k + 1) * rows)
        s1 = slice(k * n1s, (k + 1) * n1s)
        gr = gr_ref[s1].reshape(rows, D_FOURIER)
        gi = gi_ref[s1].reshape(rows, D_FOURIER)
        yf = jnp.concatenate(
            [_dot(gr[:, c:c + MXU_DIM], bdr_ref[c:c + MXU_DIM, c:c + MXU_DIM])
             + _dot(gi[:, c:c + MXU_DIM], bdi_ref[c:c + MXU_DIM, c:c + MXU_DIM])
             for c in range(0, D_FOURIER, MXU_DIM)], axis=1)
        yf = _rms(yf, fg_ref[...])
        y = _unrotate_rows(ys_ref[r, :])
        gl = jax.nn.gelu(y, approximate=True)
        gate = _dot(gl.astype(BF16), wg_ref[...]) + bg_ref[...]
        ys = _rms(gl * jax.nn.sigmoid(gate), sg_ref[...])
        cat = jnp.concatenate([yf.astype(BF16), ys.astype(BF16)], axis=-1)
        x1 = x_ref[s1].reshape(rows, D_MODEL) + _rms(_dot(cat, wo_ref[...]), pmg_ref[...])
        return x1, _rms(x1, plg_ref[...]).astype(BF16)

    def mlp(k, x1, h):
        acc = jnp.zeros(x1.shape, F32)
        for j in range(D_FF // ff_blk):
            a = _dot(h, w1_ref[:, j * ff_blk:(j + 1) * ff_blk])
            a = jnp.square(jnp.maximum(a, 0.0)).astype(BF16)
            acc = acc + _dot(a, w2_ref[j * ff_blk:(j + 1) * ff_blk, :])
        o_ref[k * n1s:(k + 1) * n1s] = (x1 + _rms(acc, pog_ref[...])).reshape(n1s, CHUNK, D_MODEL)

    nxt = mix(0)
    for k in range(nsub):
        cur = nxt
        if k + 1 < nsub:
            nxt = mix(k + 1)
        mlp(k, *cur)


def _mix_mlp(x5, gri, ys, bdr, bdi, fg, wg, bg, sg, wo, pmg, plg, w1, w2, pog, ff_blk, layer):
    bsz, n1, nj, t, _ = x5.shape
    lspec = functools.partial(_layer_spec, layer=layer)
    slab = lambda c: pl.BlockSpec((None, n1, None, t, c), lambda b, j: (b, 0, j, 0, 0))
    return pl.pallas_call(
        functools.partial(_mix_mlp_kernel, ff_blk=ff_blk, nsub=MIX_SUBBLOCKS),
        grid=(bsz, nj),
        in_specs=[slab(D_MODEL),
                  pl.BlockSpec((None, None, n1, None, t, D_FOURIER),
                               lambda b, j: (0, b, 0, j, 0, 0)),
                  pl.BlockSpec((None, None, n1, None, t, D_FOURIER),
                               lambda b, j: (1, b, 0, j, 0, 0)),
                  pl.BlockSpec((None, None, n1 * t, D_SSM), lambda b, j: (b, j, 0, 0)),
                  lspec((D_FOURIER, D_FOURIER)), lspec((D_FOURIER, D_FOURIER)),
                  lspec((1, D_FOURIER)),
                  lspec((D_SSM, D_SSM)), lspec((1, D_SSM)), lspec((1, D_SSM)),
                  lspec((D_MODEL, D_MODEL)), lspec((1, D_MODEL)),
                  lspec((1, D_MODEL)),
                  lspec((D_MODEL, D_FF)), lspec((D_FF, D_MODEL)),
                  lspec((1, D_MODEL))],
        out_specs=slab(D_MODEL),
        out_shape=jax.ShapeDtypeStruct(x5.shape, F32),
        compiler_params=pltpu.CompilerParams(
            dimension_semantics=("parallel", "parallel"), vmem_limit_bytes=VMEM_LIMIT),
        name="mix_mlp",
    )(x5, gri, gri, ys, bdr, bdi, fg, wg, bg, sg, wo, pmg, plg, w1, w2, pog)


def _fourier_channel_maps(w_f, seq):
    c = np.arange(FOURIER_HEAD_DIM)
    ang = 2.0 * np.pi * ((c[:, None] * c[None, :]) % FOURIER_HEAD_DIM) / FOURIER_HEAD_DIM
    scale = 1.0 / math.sqrt(seq * FOURIER_HEAD_DIM)
    hp = jax.lax.Precision.HIGHEST
    cw = jnp.einsum("cd,hde->hce", jnp.asarray(np.cos(ang) * scale, F32), w_f.astype(F32), precision=hp)
    sw = jnp.einsum("cd,hde->hce", jnp.asarray(np.sin(ang) * scale, F32), w_f.astype(F32), precision=hp)
    eye = jnp.eye(FOURIER_HEADS, dtype=F32)
    bd = lambda m: jnp.einsum("hce,hk->hcke", m, eye).reshape(D_FOURIER, D_FOURIER)
    return bd(cw).astype(BF16), bd(sw).astype(BF16)


def kernel(x, w_in, w_out, pre_mix_g, post_mix_g, pre_mlp_g, post_mlp_g, fourier_out_g, ssm_out_g,
           w_fourier, lam_re, lam_im, log_dt, b_re, b_im, c_re, c_im, d_skip, w_glu, b_glu,
           w_ff1, w_ff2):
    bsz, seq, _ = x.shape
    depth = w_in.shape[0]
    n = bsz * seq
    n1, n2 = FFT_N1, seq // FFT_N1
    nj = n2 // CHUNK
    assert seq % (FFT_N1 * N2_BLK) == 0 and N2_BLK == CHUNK and n1 % RELAYOUT_CHUNKS == 0
    wk_tab, m2_tab = _fft_tables(seq)

    rows = lambda v: v.reshape(depth, 1, -1).astype(F32)
    w_in_b, w_out_b, w_glu_b = w_in.astype(BF16), w_out.astype(BF16), w_glu.astype(BF16)
    w_ff1_b, w_ff2_b = w_ff1.astype(BF16), w_ff2.astype(BF16)
    pl_tab, pt_tab, ct_tab, bt_tab, dsk, pw, tab = jax.vmap(
        functools.partial(_ssm_tables, nj=nj))(
            lam_re, lam_im, log_dt, b_re, b_im, c_re, c_im, d_skip)
    m_intra, b_pair, c_pair = _ssm_prep(pl_tab, pt_tab, ct_tab, bt_tab, dsk)
    bdr, bdi = jax.vmap(functools.partial(_fourier_channel_maps, seq=seq))(w_fourier)
    gains = [rows(v) for v in (pre_mix_g, fourier_out_g, b_glu, ssm_out_g, post_mix_g,
                               pre_mlp_g, post_mlp_g)]
    pre_mix, four_g, glu_b, ssm_g, post_mix, pre_mlp, post_mlp = gains

    xs = x.reshape(bsz, n1, n2, D_MODEL).astype(F32)
    for i in range(depth):
        y, zs = _in_proj(xs, pre_mix, w_in_b, wk_tab, i)
        g = _fft_stage2(y.reshape(bsz, n1, 2 * n2, D_FOURIER), m2_tab, kb=FFT2_K1_BLOCK)
        gri = g.reshape(2, bsz, n1, nj, CHUNK, D_FOURIER)
        ysr = _ssm_chunked(zs, m_intra, b_pair, c_pair, pw, tab, i)
        xs = _mix_mlp(xs.reshape(bsz, n1, nj, CHUNK, D_MODEL), gri, ysr, bdr, bdi, four_g,
                      w_glu_b, glu_b, ssm_g, w_out_b, post_mix, pre_mlp, w_ff1_b, w_ff2_b,
                      post_mlp, ff_blk=FF_BLOCK, layer=i).reshape(bsz, n1, n2, D_MODEL)
    return xs.reshape(bsz, seq, D_MODEL).astype(x.dtype)
```
